```python
import jax, jax.numpy as jnp
from jax import lax
import numpy as np

D_MODEL = 2048
BATCH = 4
SEQ = 2048
DEPTH = 1
DEC_BATCH = 128
DEC_SEQ = 1
PAST_LEN = 16384
PAGE_SIZE = 128

MIX_DIM = D_MODEL
CONV_DIM = MIX_DIM // 2
CONV_K = 3
MLSTM_HEADS = 4
MLSTM_DIM = MIX_DIM - CONV_DIM
MLSTM_DV = MLSTM_DIM // MLSTM_HEADS
MLSTM_DK = MLSTM_DV // 2
MLSTM_CHUNK = 128
IN_DIM = 3 * CONV_DIM + MLSTM_HEADS * (2 * MLSTM_DK + 2 * MLSTM_DV + 2)
N_EXPERTS = 32
TOP_K = 4
D_FF = D_MODEL
SWIGLU_LIMIT = 7.0
SWIGLU_ALPHA = 1.702
MOE_BLOCK = 128
PLE_DIM = 256
F_GATE_BIAS = 3.0
EPS = 1e-6

kernel_name = "hybrid_conv_mlstm_moe_ple_step"


def rmsnorm(x, g):
    xf = x.astype(jnp.float32)
    y = xf * lax.rsqrt(jnp.mean(xf * xf, axis=-1, keepdims=True) + EPS)
    return (y * g.astype(jnp.float32)).astype(x.dtype)


def mlstm_chunkwise(q, k, v, i_pre, lf, C, n, m):
    Bsz, H, T, _ = q.shape
    L = min(MLSTM_CHUNK, T)
    nc = -(-T // L)
    pad = nc * L - T
    if pad:
        pw = ((0, 0), (0, 0), (0, pad))
        q = jnp.pad(q, pw + ((0, 0),))
        k = jnp.pad(k, pw + ((0, 0),))
        v = jnp.pad(v, pw + ((0, 0),))
        i_pre = jnp.pad(i_pre, pw, constant_values=-jnp.inf)
        lf = jnp.pad(lf, pw)

    def to_chunks(t):
        return jnp.moveaxis(t.reshape(t.shape[:2] + (nc, L) + t.shape[3:]), 2, 0)

    causal = jnp.tril(jnp.ones((L, L), bool))

    def step(carry, xs):
        C, n, m = carry
        qc, kc, vc, ic, fc = xs
        b = jnp.cumsum(fc, axis=-1)
        dmat = jnp.where(causal, b[..., :, None] - b[..., None, :] + ic[..., None, :], -jnp.inf)
        inter = b + m[..., None]
        m_t = jnp.maximum(inter, dmat.max(-1))
        w_int = jnp.exp(inter - m_t)
        s = jnp.einsum('bhtd,bhsd->bhts', qc, kc) * jnp.exp(dmat - m_t[..., None])
        num = jnp.einsum('bhts,bhse->bhte', s, vc) + w_int[..., None] * jnp.einsum('bhtd,bhde->bhte', qc, C)
        den = s.sum(-1) + w_int * jnp.einsum('bhtd,bhd->bht', qc, n)
        h = num / jnp.maximum(jnp.abs(den), jnp.exp(-m_t))[..., None]
        b_end = b[..., -1]
        g = b_end[..., None] - b + ic
        m_new = jnp.maximum(b_end + m, g.max(-1))
        decay = jnp.exp(b_end + m - m_new)
        wg = jnp.exp(g - m_new[..., None])
        C_new = decay[..., None, None] * C + jnp.einsum('bhs,bhsd,bhse->bhde', wg, kc, vc)
        n_new = decay[..., None] * n + jnp.einsum('bhs,bhsd->bhd', wg, kc)
        return (C_new, n_new, m_new), h

    (C, n, m), hs = lax.scan(step, (C, n, m), tuple(map(to_chunks, (q, k, v, i_pre, lf))))
    h = jnp.moveaxis(hs, 0, 2).reshape(Bsz, H, nc * L, -1)[:, :, :T]
    return h, C, n, m


def mixer(a, conv_s, C0, n0, m0, w_in, b_gates, w_conv, g_conv_out, g_mlstm_out, w_out):
    Bsz, T, _ = a.shape
    H, DK, DV = MLSTM_HEADS, MLSTM_DK, MLSTM_DV
    z = a @ w_in
    sizes = (CONV_DIM, CONV_DIM, CONV_DIM, H * DK, H * DK, H * DV, H * DV, H, H)
    zb, zc, zh, zq, zk, zv, zo, zi, zf = jnp.split(z, np.cumsum(sizes)[:-1].tolist(), axis=-1)

    u = zc * zh
    ext = jnp.concatenate([conv_s.astype(u.dtype), u], axis=1)
    conv = sum(w_conv[j] * ext[:, j:j + T] for j in range(CONV_K))
    y_conv = rmsnorm(zb * conv, g_conv_out)
    new_conv = ext[:, -(CONV_K - 1):]

    f32 = jnp.float32
    heads = lambda t, d: t.astype(f32).reshape(Bsz, T, H, d).transpose(0, 2, 1, 3)
    q = heads(zq, DK)
    k = heads(zk, DK) * (DK ** -0.5)
    v = heads(zv, DV)
    bg = b_gates.astype(f32)
    i_pre = (zi.astype(f32) + bg[:H]).transpose(0, 2, 1)
    lf = jax.nn.log_sigmoid(zf.astype(f32) + bg[H:]).transpose(0, 2, 1)
    h, C1, n1, m1 = mlstm_chunkwise(q, k, v, i_pre, lf, C0.astype(f32), n0.astype(f32), m0.astype(f32))
    h = h * lax.rsqrt(jnp.mean(h * h, axis=-1, keepdims=True) + EPS) * g_mlstm_out.astype(f32)[:, None, :]
    h = h.transpose(0, 2, 1, 3).reshape(Bsz, T, MLSTM_DIM)
    y_mlstm = (jax.nn.sigmoid(zo.astype(f32)) * h).astype(a.dtype)

    out = jnp.concatenate([y_conv, y_mlstm], axis=-1) @ w_out
    return out, new_conv, C1, n1, m1


def moe(x, w_router, b_router, w_gate_up, b_gate_up, w_down, b_down):
    lead = x.shape[:-1]
    xt = x.reshape(-1, D_MODEL)
    n_tok = xt.shape[0]
    logits = xt.astype(jnp.float32) @ w_router.astype(jnp.float32) + b_router.astype(jnp.float32)
    top_logit, top_e = lax.top_k(logits, TOP_K)
    gate = jax.nn.softmax(top_logit, axis=-1)
    n_assign = n_tok * TOP_K
    e_flat = top_e.reshape(-1)
    tok_flat = jnp.repeat(jnp.arange(n_tok, dtype=jnp.int32), TOP_K)
    order = jnp.argsort(e_flat)
    e_sorted = e_flat[order]
    counts = jnp.bincount(e_flat, length=N_EXPERTS)
    padded = (counts + MOE_BLOCK - 1) // MOE_BLOCK * MOE_BLOCK
    start = jnp.cumsum(counts) - counts
    pend = jnp.cumsum(padded)
    pstart = pend - padded
    dest = pstart[e_sorted] + jnp.arange(n_assign, dtype=jnp.int32) - start[e_sorted]
    n_blocks = -(-n_assign // MOE_BLOCK) + N_EXPERTS
    n_slots = n_blocks * MOE_BLOCK
    slot_tok = jnp.zeros((n_slots,), jnp.int32).at[dest].set(tok_flat[order])
    slot_gate = jnp.zeros((n_slots,), jnp.float32).at[dest].set(gate.reshape(-1)[order])
    block_e = jnp.minimum(jnp.searchsorted(pend, jnp.arange(n_blocks) * MOE_BLOCK, side='right'), N_EXPERTS - 1)
    xb = xt[slot_tok].reshape(n_blocks, MOE_BLOCK, D_MODEL)

    def expert_block(args):
        xblk, e = args
        gu = xblk @ w_gate_up[e] + b_gate_up[e]
        g = jnp.minimum(gu[:, :D_FF], SWIGLU_LIMIT)
        up = jnp.clip(gu[:, D_FF:], -SWIGLU_LIMIT, SWIGLU_LIMIT)
        act = g * jax.nn.sigmoid(SWIGLU_ALPHA * g) * (up + 1)
        return act @ w_down[e] + b_down[e]

    yb = lax.map(expert_block, (xb, block_e)).reshape(n_slots, D_MODEL)
    y = jax.ops.segment_sum(yb * slot_gate[:, None].astype(yb.dtype), slot_tok, num_segments=n_tok)
    return y.reshape(lead + (D_MODEL,)).astype(x.dtype)


def trunk(x, p, conv_s, C_s, n_s, m_s, g_mix, w_in, b_gates, w_conv, g_conv_out, g_mlstm_out, w_out,
          g_ffn, w_router, b_router, w_gate_up, b_gate_up, w_down, b_down, g_ple, w_ple_gate, w_ple_proj,
          g_final):
    h = x
    convs, Cs, ns, ms = [], [], [], []
    for l in range(DEPTH):
        mix, cv, C1, n1, m1 = mixer(rmsnorm(h, g_mix[l]), conv_s[l], C_s[l], n_s[l], m_s[l], w_in[l],
                                    b_gates[l], w_conv[l], g_conv_out[l], g_mlstm_out[l], w_out[l])
        h = h + mix
        h = h + moe(rmsnorm(h, g_ffn[l]), w_router[l], b_router[l], w_gate_up[l], b_gate_up[l],
                    w_down[l], b_down[l])
        ple_gate = jax.nn.sigmoid(rmsnorm(h, g_ple[l]) @ w_ple_gate[l])
        h = h + ple_gate * (p[l].astype(h.dtype) @ w_ple_proj[l])
        convs.append(cv); Cs.append(C1); ns.append(n1); ms.append(m1)
    return (rmsnorm(h, g_final), jnp.stack(convs), jnp.stack(Cs), jnp.stack(ns), jnp.stack(ms))


def setup_inputs(seed: int = 0) -> dict:
    key = jax.random.key(seed)
    ks = jax.random.split(key, 32)
    f32 = jnp.float32
    nrm = lambda k, shape, s: jax.random.normal(k, shape, f32) * s
    H = MLSTM_HEADS
    return {
        "x_prompt": nrm(ks[0], (BATCH, SEQ, D_MODEL), 1.0),
        "x_sample": nrm(ks[1], (DEC_BATCH, DEC_SEQ, D_MODEL), 1.0),
        "state_conv": nrm(ks[2], (DEPTH, DEC_BATCH, CONV_K - 1, CONV_DIM), 1.0),
        "state_mlstm_C": nrm(ks[3], (DEPTH, DEC_BATCH, H, MLSTM_DK, MLSTM_DV), 0.5),
        "state_mlstm_n": nrm(ks[4], (DEPTH, DEC_BATCH, H, MLSTM_DK), 0.5),
        "state_mlstm_m": nrm(ks[5], (DEPTH, DEC_BATCH, H), 1.0),
        "p_prompt": nrm(ks[6], (DEPTH, BATCH, SEQ, PLE_DIM), 1.0),
        "p_sample": nrm(ks[7], (DEPTH, DEC_BATCH, DEC_SEQ, PLE_DIM), 1.0),
        "g_mix": 1.0 + nrm(ks[8], (DEPTH, D_MODEL), 0.01),
        "w_in": nrm(ks[9], (DEPTH, D_MODEL, IN_DIM), D_MODEL ** -0.5),
        "b_gates": jnp.concatenate([nrm(ks[10], (DEPTH, H), 0.1),
                                    F_GATE_BIAS + nrm(ks[11], (DEPTH, H), 0.1)], axis=-1),
        "w_conv": nrm(ks[12], (DEPTH, CONV_K, CONV_DIM), CONV_K ** -0.5),
        "g_conv_out": 1.0 + nrm(ks[13], (DEPTH, CONV_DIM), 0.01),
        "g_mlstm_out": 1.0 + nrm(ks[14], (DEPTH, H, MLSTM_DV), 0.01),
        "w_out": nrm(ks[15], (DEPTH, MIX_DIM, D_MODEL), MIX_DIM ** -0.5),
        "g_ffn": 1.0 + nrm(ks[16], (DEPTH, D_MODEL), 0.01),
        "w_router": nrm(ks[17], (DEPTH, D_MODEL, N_EXPERTS), D_MODEL ** -0.5),
        "b_router": nrm(ks[18], (DEPTH, N_EXPERTS), 0.01),
        "w_gate_up": nrm(ks[19], (DEPTH, N_EXPERTS, D_MODEL, 2 * D_FF), D_MODEL ** -0.5),
        "b_gate_up": nrm(ks[20], (DEPTH, N_EXPERTS, 2 * D_FF), 0.01),
        "w_down": nrm(ks[21], (DEPTH, N_EXPERTS, D_FF, D_MODEL), D_FF ** -0.5),
        "b_down": nrm(ks[22], (DEPTH, N_EXPERTS, D_MODEL), 0.01),
        "g_ple": 1.0 + nrm(ks[23], (DEPTH, D_MODEL), 0.01),
        "w_ple_gate": nrm(ks[24], (DEPTH, D_MODEL, D_MODEL), D_MODEL ** -0.5),
        "w_ple_proj": nrm(ks[25], (DEPTH, PLE_DIM, D_MODEL), PLE_DIM ** -0.5),
        "g_final": 1.0 + nrm(ks[26], (D_MODEL,), 0.01),
    }


def reference(x_prompt, x_sample, state_conv, state_mlstm_C, state_mlstm_n, state_mlstm_m, p_prompt,
              p_sample, g_mix, w_in, b_gates, w_conv, g_conv_out, g_mlstm_out, w_out, g_ffn, w_router,
              b_router, w_gate_up, b_gate_up, w_down, b_down, g_ple, w_ple_gate, w_ple_proj, g_final):
    weights = (g_mix, w_in, b_gates, w_conv, g_conv_out, g_mlstm_out, w_out, g_ffn, w_router, b_router,
               w_gate_up, b_gate_up, w_down, b_down, g_ple, w_ple_gate, w_ple_proj, g_final)
    nb = x_prompt.shape[0]
    f32 = jnp.float32
    conv0 = jnp.zeros((DEPTH, nb, CONV_K - 1, CONV_DIM), f32)
    C0 = jnp.zeros((DEPTH, nb, MLSTM_HEADS, MLSTM_DK, MLSTM_DV), f32)
    n0 = jnp.zeros((DEPTH, nb, MLSTM_HEADS, MLSTM_DK), f32)
    m0 = jnp.zeros((DEPTH, nb, MLSTM_HEADS), f32)
    y_prompt, conv_p, C_p, n_p, m_p = trunk(x_prompt, p_prompt, conv0, C0, n0, m0, *weights)
    y_sample, conv_s, C_s, n_s, m_s = trunk(x_sample, p_sample, state_conv, state_mlstm_C, state_mlstm_n,
                                            state_mlstm_m, *weights)
    return (y_prompt, y_sample, conv_p, C_p, n_p, m_p, conv_s, C_s, n_s, m_s)
```

```python
import functools

import jax
import jax.numpy as jnp
from jax import lax
from jax.experimental import pallas as pl
from jax.experimental.pallas import tpu as pltpu

F32 = jnp.float32
BF16 = jnp.bfloat16

D_MODEL = 2048
CONV_DIM = 1024
HEADS = 4
DK = 128
DV = 256
CHUNK = 128
N_EXPERTS = 32
TOP_K = 4
D_FF = 2048
PLE_DIM = 256
SWIGLU_LIMIT = 7.0
SWIGLU_ALPHA = 1.702
EPS = 1e-6
LANES = 128
NEG_BIG = -1e30
MAIN_COLS = 3 * CONV_DIM + HEADS * (2 * DK + 2 * DV)

COL_B, COL_C, COL_H = 0, CONV_DIM, 2 * CONV_DIM
COL_Q = 3 * CONV_DIM
COL_K = COL_Q + HEADS * DK
COL_V = COL_K + HEADS * DK
COL_O = COL_V + HEADS * DV

VMEM_LIMIT = 56 * 1024 * 1024

MOE_TM = 512
MOE_TF = 512


def _cparams(sem, vmem=VMEM_LIMIT):
    return pltpu.CompilerParams(dimension_semantics=sem, vmem_limit_bytes=vmem)


def _rms(x, g):
    return x * lax.rsqrt(jnp.mean(x * x, axis=-1, keepdims=True) + EPS) * g


def _split3(x):
    x1 = x.astype(BF16)
    r1 = x - x1.astype(F32)
    x2 = r1.astype(BF16)
    x3 = (r1 - x2.astype(F32)).astype(BF16)
    return x1, x2, x3


def _log_sigmoid(x):
    return jnp.minimum(x, 0.0) - jnp.log(1.0 + jnp.exp(-jnp.abs(x)))


def _inproj_kernel(x_ref, g_ref, w_ref, wg_ref, z_ref, zg_ref, a_scr):
    @pl.when(pl.program_id(1) == 0)
    def _():
        a = _rms(x_ref[...], g_ref[...]).astype(BF16)
        a_scr[...] = a
        zg_ref[...] = jnp.dot(a, wg_ref[...], preferred_element_type=F32)

    z_ref[...] = jnp.dot(a_scr[...], w_ref[...], preferred_element_type=F32)


def _inproj(x, g, w_main, w_gates, tm, tn):
    n_tok = x.shape[0]
    return pl.pallas_call(
        _inproj_kernel,
        grid=(n_tok // tm, MAIN_COLS // tn),
        in_specs=[
            pl.BlockSpec((tm, D_MODEL), lambda i, j: (i, 0)),
            pl.BlockSpec((1, D_MODEL), lambda i, j: (0, 0)),
            pl.BlockSpec((D_MODEL, tn), lambda i, j: (0, j)),
            pl.BlockSpec((D_MODEL, LANES), lambda i, j: (0, 0)),
        ],
        out_specs=[
            pl.BlockSpec((tm, tn), lambda i, j: (i, j)),
            pl.BlockSpec((tm, LANES), lambda i, j: (i, 0)),
        ],
        out_shape=[
            jax.ShapeDtypeStruct((n_tok, MAIN_COLS), F32),
            jax.ShapeDtypeStruct((n_tok, LANES), F32),
        ],
        scratch_shapes=[pltpu.VMEM((tm, D_MODEL), BF16)],
        compiler_params=_cparams(("parallel", "arbitrary")),
        name="inproj",
    )(x, g, w_main, w_gates)


def _conv_prompt_kernel(zb_ref, zc_ref, zh_ref, wc_ref, g_ref, y_ref, st_ref, carry):
    tt = zb_ref.shape[0]

    @pl.when(pl.program_id(1) == 0)
    def _():
        carry[...] = jnp.zeros_like(carry)

    u = zc_ref[...] * zh_ref[...]
    prev = carry[...]
    row = lax.broadcasted_iota(jnp.int32, (tt, 1), 0)
    u1 = jnp.where(row == 0, prev[7:8], pltpu.roll(u, 1, axis=0))
    u2 = jnp.where(row == 0, prev[6:7], jnp.where(row == 1, prev[7:8], pltpu.roll(u, 2, axis=0)))
    wc = wc_ref[...]
    conv = wc[0:1] * u2 + wc[1:2] * u1 + wc[2:3] * u
    y_ref[...] = _rms(zb_ref[...] * conv, g_ref[...]).astype(BF16)
    carry[...] = u[tt - 8:tt]
    st_ref[0] = u[tt - 2:tt]


def _conv_prompt(z, w_conv, g_conv, batch, seq, tt):
    nt = seq // tt
    n_tok = batch * seq
    cb = CONV_DIM
    return pl.pallas_call(
        _conv_prompt_kernel,
        grid=(batch, nt),
        in_specs=[
            pl.BlockSpec((tt, cb), lambda b, t: (b * nt + t, COL_B // cb)),
            pl.BlockSpec((tt, cb), lambda b, t: (b * nt + t, COL_C // cb)),
            pl.BlockSpec((tt, cb), lambda b, t: (b * nt + t, COL_H // cb)),
            pl.BlockSpec((8, cb), lambda b, t: (0, 0)),
            pl.BlockSpec((1, cb), lambda b, t: (0, 0)),
        ],
        out_specs=[
            pl.BlockSpec((tt, cb), lambda b, t: (b * nt + t, 0)),
            pl.BlockSpec((1, 2, cb), lambda b, t: (b, 0, 0)),
        ],
        out_shape=[
            jax.ShapeDtypeStruct((n_tok, cb), BF16),
            jax.ShapeDtypeStruct((batch, 2, cb), F32),
        ],
        scratch_shapes=[pltpu.VMEM((8, cb), F32)],
        compiler_params=_cparams(("parallel", "arbitrary")),
        name="conv_prompt",
    )(z, z, z, w_conv, g_conv)


def _tri_dot(tri_bf16, x, tri_first):
    out = None
    for part in _split3(x):
        d = (jnp.dot(tri_bf16, part, preferred_element_type=F32) if tri_first
             else jnp.dot(part, tri_bf16, preferred_element_type=F32))
        out = d if out is None else out + d
    return out


def _mlstm_prompt_kernel(q_ref, k_ref, v_ref, o_ref, zg_ref, bg_ref, gn_ref,
                         y_ref, c_out, n_out, m_out, c_scr, n_scr, m_scr):
    L = CHUNK

    @pl.when(pl.program_id(1) == 0)
    def _():
        c_scr[...] = jnp.zeros_like(c_scr)
        n_scr[...] = jnp.zeros_like(n_scr)
        m_scr[...] = jnp.zeros_like(m_scr)

    lane = lax.broadcasted_iota(jnp.int32, (L, LANES), 1)
    r_i = lax.broadcasted_iota(jnp.int32, (L, L), 0)
    c_i = lax.broadcasted_iota(jnp.int32, (L, L), 1)
    causal = c_i <= r_i
    tril = causal.astype(BF16)
    triu = (r_i <= c_i).astype(BF16)

    zg = zg_ref[...] + bg_ref[...]
    gates = jnp.where(lane < HEADS, zg, _log_sigmoid(zg))
    gates_t = gates.T
    bcol_all = _tri_dot(tril, gates, True)
    brow_all = _tri_dot(triu, gates_t, False)

    for h in range(HEADS):
        b_col = bcol_all[:, HEADS + h:HEADS + h + 1]
        b_row = brow_all[HEADS + h:HEADS + h + 1, :]
        i_col = gates[:, h:h + 1]
        i_row = gates_t[h:h + 1, :]
        b_end = b_col[L - 1:L, :]
        m0 = m_scr[h:h + 1, 0:1]
        n0 = n_scr[h:h + 1, :]
        c0 = c_scr[h]

        q = q_ref[:, h * DK:(h + 1) * DK]
        k = k_ref[:, h * DK:(h + 1) * DK] * (DK ** -0.5)
        v = v_ref[:, h * DV:(h + 1) * DV]
        qb, vb = q.astype(BF16), v.astype(BF16)

        dmat = jnp.where(causal, b_col - b_row + i_row, NEG_BIG)
        inter = b_col + m0
        m_t = jnp.maximum(inter, jnp.max(dmat, axis=1, keepdims=True))
        w_int = jnp.exp(inter - m_t)
        p = jnp.where(causal, jnp.exp(dmat - m_t), 0.0)
        s = lax.dot_general(qb, k.astype(BF16), (((1,), (1,)), ((), ())),
                            preferred_element_type=F32) * p
        num = (jnp.dot(s.astype(BF16), vb, preferred_element_type=F32)
               + w_int * jnp.dot(qb, c0.astype(BF16), preferred_element_type=F32))
        den = jnp.sum(s, axis=1, keepdims=True) + w_int * jnp.sum(q * n0, axis=1, keepdims=True)
        hh = num / jnp.maximum(jnp.abs(den), jnp.exp(-m_t))
        hn = hh * lax.rsqrt(jnp.mean(hh * hh, axis=-1, keepdims=True) + EPS)
        hn = hn * gn_ref[:, h * DV:(h + 1) * DV]
        y = jax.nn.sigmoid(o_ref[:, h * DV:(h + 1) * DV]) * hn
        y_ref[:, h * DV:(h + 1) * DV] = y.astype(BF16)

        g_col = b_end - b_col + i_col
        g_row = b_end - b_row + i_row
        m_new = jnp.maximum(b_end + m0, jnp.max(g_row, axis=1, keepdims=True))
        decay = jnp.exp(b_end + m0 - m_new)
        kw = k * jnp.exp(g_col - m_new)
        c_new = decay * c0 + lax.dot_general(kw.astype(BF16), vb, (((0,), (0,)), ((), ())),
                                             preferred_element_type=F32)
        n_new = decay * n0 + jnp.sum(kw, axis=0, keepdims=True)
        c_scr[h] = c_new
        n_scr[h:h + 1, :] = n_new
        m_scr[h:h + 1, :] = jnp.broadcast_to(m_new, (1, LANES))
        c_out[0, h] = c_new
        n_out[0, h:h + 1, :] = n_new
        m_out[0, h:h + 1, :] = jnp.broadcast_to(m_new, (1, LANES))


def _mlstm_prompt(z, zg, b_gates, g_norm, batch, seq):
    nc = seq // CHUNK
    n_tok = batch * seq
    qw, vw = HEADS * DK, HEADS * DV
    return pl.pallas_call(
        _mlstm_prompt_kernel,
        grid=(batch, nc),
        in_specs=[
            pl.BlockSpec((CHUNK, qw), lambda b, c: (b * nc + c, COL_Q // qw)),
            pl.BlockSpec((CHUNK, qw), lambda b, c: (b * nc + c, COL_K // qw)),
            pl.BlockSpec((CHUNK, vw), lambda b, c: (b * nc + c, COL_V // vw)),
            pl.BlockSpec((CHUNK, vw), lambda b, c: (b * nc + c, COL_O // vw)),
            pl.BlockSpec((CHUNK, LANES), lambda b, c: (b * nc + c, 0)),
            pl.BlockSpec((1, LANES), lambda b, c: (0, 0)),
            pl.BlockSpec((1, vw), lambda b, c: (0, 0)),
        ],
        out_specs=[
            pl.BlockSpec((CHUNK, vw), lambda b, c: (b * nc + c, 0)),
            pl.BlockSpec((1, HEADS, DK, DV), lambda b, c: (b, 0, 0, 0)),
            pl.BlockSpec((1, HEADS, DK), lambda b, c: (b, 0, 0)),
            pl.BlockSpec((1, HEADS, LANES), lambda b, c: (b, 0, 0)),
        ],
        out_shape=[
            jax.ShapeDtypeStruct((n_tok, vw), BF16),
            jax.ShapeDtypeStruct((batch, HEADS, DK, DV), F32),
            jax.ShapeDtypeStruct((batch, HEADS, DK), F32),
            jax.ShapeDtypeStruct((batch, HEADS, LANES), F32),
        ],
        scratch_shapes=[
            pltpu.VMEM((HEADS, DK, DV), F32),
            pltpu.VMEM((8, DK), F32),
            pltpu.VMEM((8, LANES), F32),
        ],
        compiler_params=_cparams(("parallel", "arbitrary")),
        name="mlstm_prompt",
    )(z, z, z, z, zg, b_gates, g_norm)


def _mixer_sample_kernel(zb_ref, zc_ref, zh_ref, q_ref, k_ref, v_ref, o_ref, zg_ref,
                         st_ref, c_ref, n_ref, m_ref, wc_ref, gc_ref, bg_ref, gn_ref,
                         yc_ref, ym_ref, st_out, c_out, n_out, m_out):
    sb = zb_ref.shape[0]
    u = zc_ref[...] * zh_ref[...]
    s0 = st_ref[:, 0:CONV_DIM]
    s1 = st_ref[:, CONV_DIM:2 * CONV_DIM]
    wc = wc_ref[...]
    conv = wc[0:1] * s0 + wc[1:2] * s1 + wc[2:3] * u
    yc_ref[...] = _rms(zb_ref[...] * conv, gc_ref[...]).astype(BF16)
    st_out[:, 0:CONV_DIM] = s1
    st_out[:, CONV_DIM:2 * CONV_DIM] = u

    lane_row = lax.broadcasted_iota(jnp.int32, (1, LANES), 1)
    eye = (lax.broadcasted_iota(jnp.int32, (DK, DK), 0)
           == lax.broadcasted_iota(jnp.int32, (DK, DK), 1)).astype(F32)

    def to_col(row):
        return jnp.sum(eye * row, axis=1, keepdims=True)

    def pick(row, j):
        return jnp.sum(jnp.where(lane_row == j, row, 0.0), axis=1, keepdims=True)

    def per_sample(s, carry):
        zg = zg_ref[s] + bg_ref[...]
        lf_all = _log_sigmoid(zg)
        m_row = m_ref[s]
        m_new_row = jnp.zeros((1, LANES), F32)
        for h in range(HEADS):
            q = q_ref[s, :, h * DK:(h + 1) * DK]
            k = k_ref[s, :, h * DK:(h + 1) * DK] * (DK ** -0.5)
            v = v_ref[s, :, h * DV:(h + 1) * DV]
            o = o_ref[s, :, h * DV:(h + 1) * DV]
            i_pre = pick(zg, h)
            lf = pick(lf_all, HEADS + h)
            m0 = pick(m_row, h)
            c0 = c_ref[s, h]
            n0 = n_ref[s, h:h + 1, :]
            inter = lf + m0
            m_t = jnp.maximum(inter, i_pre)
            w_int = jnp.exp(inter - m_t)
            p = jnp.exp(i_pre - m_t)
            sc = jnp.sum(q * k, axis=1, keepdims=True) * p
            qc = jnp.sum(to_col(q) * c0, axis=0, keepdims=True)
            num = sc * v + w_int * qc
            den = sc + w_int * jnp.sum(q * n0, axis=1, keepdims=True)
            hh = num / jnp.maximum(jnp.abs(den), jnp.exp(-m_t))
            hn = hh * lax.rsqrt(jnp.mean(hh * hh, axis=-1, keepdims=True) + EPS)
            hn = hn * gn_ref[:, h * DV:(h + 1) * DV]
            ym_ref[s, :, h * DV:(h + 1) * DV] = jax.nn.sigmoid(o) * hn
            c_out[s, h] = w_int * c0 + (p * to_col(k)) * v
            n_out[s, h:h + 1, :] = w_int * n0 + p * k
            m_new_row = jnp.where(lane_row == h, m_t, m_new_row)
        m_out[s] = m_new_row
        return carry

    lax.fori_loop(0, sb, per_sample, 0)


def _mixer_sample(z, zs3, zg3, st, c_s, n_s, m3, w_conv, g_conv, b_gates, g_norm, row0, n_dec, sb):
    cb, qw, vw = CONV_DIM, HEADS * DK, HEADS * DV
    r0 = row0 // sb
    row = lambda i: r0 + i
    return pl.pallas_call(
        _mixer_sample_kernel,
        grid=(n_dec // sb,),
        in_specs=[
            pl.BlockSpec((sb, cb), lambda i: (row(i), COL_B // cb)),
            pl.BlockSpec((sb, cb), lambda i: (row(i), COL_C // cb)),
            pl.BlockSpec((sb, cb), lambda i: (row(i), COL_H // cb)),
            pl.BlockSpec((sb, 1, qw), lambda i: (i, 0, COL_Q // qw)),
            pl.BlockSpec((sb, 1, qw), lambda i: (i, 0, COL_K // qw)),
            pl.BlockSpec((sb, 1, vw), lambda i: (i, 0, COL_V // vw)),
            pl.BlockSpec((sb, 1, vw), lambda i: (i, 0, COL_O // vw)),
            pl.BlockSpec((sb, 1, LANES), lambda i: (i, 0, 0)),
            pl.BlockSpec((sb, 2 * cb), lambda i: (i, 0)),
            pl.BlockSpec((sb, HEADS, DK, DV), lambda i: (i, 0, 0, 0)),
            pl.BlockSpec((sb, HEADS, DK), lambda i: (i, 0, 0)),
            pl.BlockSpec((sb, 1, LANES), lambda i: (i, 0, 0)),
            pl.BlockSpec((8, cb), lambda i: (0, 0)),
            pl.BlockSpec((1, cb), lambda i: (0, 0)),
            pl.BlockSpec((1, LANES), lambda i: (0, 0)),
            pl.BlockSpec((1, vw), lambda i: (0, 0)),
        ],
        out_specs=[
            pl.BlockSpec((sb, cb), lambda i: (i, 0)),
            pl.BlockSpec((sb, 1, vw), lambda i: (i, 0, 0)),
            pl.BlockSpec((sb, 2 * cb), lambda i: (i, 0)),
            pl.BlockSpec((sb, HEADS, DK, DV), lambda i: (i, 0, 0, 0)),
            pl.BlockSpec((sb, HEADS, DK), lambda i: (i, 0, 0)),
            pl.BlockSpec((sb, 1, LANES), lambda i: (i, 0, 0)),
        ],
        out_shape=[
            jax.ShapeDtypeStruct((n_dec, cb), BF16),
            jax.ShapeDtypeStruct((n_dec, 1, vw), F32),
            jax.ShapeDtypeStruct((n_dec, 2 * cb), F32),
            jax.ShapeDtypeStruct((n_dec, HEADS, DK, DV), F32),
            jax.ShapeDtypeStruct((n_dec, HEADS, DK), F32),
            jax.ShapeDtypeStruct((n_dec, 1, LANES), F32),
        ],
        compiler_params=_cparams(("arbitrary",)),
        name="mixer_sample",
    )(z, z, z, zs3, zs3, zs3, zs3, zg3, st, c_s, n_s, m3, w_conv, g_conv, b_gates, g_norm)


def _outproj_router_kernel(yc_ref, ym_ref, x_ref, wo1_ref, wo2_ref, g_ref, wr_hi_ref, wr_lo_ref, br_ref,
                           h1_ref, xn_ref, e_ref, gate_ref, rank_ref, cnt_ref, carry):
    tm = x_ref.shape[0]

    @pl.when(pl.program_id(0) == 0)
    def _():
        carry[...] = jnp.zeros_like(carry)

    mix = (jnp.dot(yc_ref[...], wo1_ref[...], preferred_element_type=F32)
           + jnp.dot(ym_ref[...], wo2_ref[...], preferred_element_type=F32))
    h1 = x_ref[...] + mix
    h1_ref[...] = h1
    xn = _rms(h1, g_ref[...])
    xn_ref[...] = xn

    xh = xn.astype(BF16)
    xl = (xn - xh.astype(F32)).astype(BF16)
    logits = (jnp.dot(xh, wr_hi_ref[...], preferred_element_type=F32)
              + jnp.dot(xl, wr_hi_ref[...], preferred_element_type=F32)
              + jnp.dot(xh, wr_lo_ref[...], preferred_element_type=F32)) + br_ref[...]
    lane = lax.broadcasted_iota(jnp.int32, (tm, LANES), 1)
    lane_f = lane.astype(F32)
    work = jnp.where(lane < N_EXPERTS, logits, NEG_BIG)

    tops, idxs = [], []
    chosen = jnp.zeros((tm, LANES), F32)
    for _ in range(TOP_K):
        mx = jnp.max(work, axis=1, keepdims=True)
        idx = jnp.min(jnp.where(work == mx, lane_f, float(LANES)), axis=1, keepdims=True)
        sel = lane_f == idx
        tops.append(mx)
        idxs.append(idx)
        chosen = jnp.where(sel, 1.0, chosen)
        work = jnp.where(sel, NEG_BIG, work)

    exps = [jnp.exp(t - tops[0]) for t in tops]
    denom = exps[0] + exps[1] + exps[2] + exps[3]

    r_i = lax.broadcasted_iota(jnp.int32, (tm, tm), 0)
    c_i = lax.broadcasted_iota(jnp.int32, (tm, tm), 1)
    before = jnp.dot((c_i < r_i).astype(BF16), chosen.astype(BF16),
                     preferred_element_type=F32) + carry[...]

    e_out = jnp.zeros((tm, LANES), F32)
    g_out = jnp.zeros((tm, LANES), F32)
    r_out = jnp.zeros((tm, LANES), F32)
    for kk in range(TOP_K):
        rank = jnp.sum(jnp.where(lane_f == idxs[kk], before, 0.0), axis=1, keepdims=True)
        e_out = jnp.where(lane == kk, idxs[kk], e_out)
        g_out = jnp.where(lane == kk, exps[kk] / denom, g_out)
        r_out = jnp.where(lane == kk, rank, r_out)
    e_ref[...] = e_out.astype(jnp.int32)
    gate_ref[...] = g_out
    rank_ref[...] = r_out.astype(jnp.int32)

    new_carry = carry[...] + jnp.sum(chosen, axis=0, keepdims=True)
    carry[...] = new_carry
    cnt_ref[...] = new_carry


def _outproj_router(y_conv, y_mlstm, x, wo1, wo2, g_ffn, wr_hi, wr_lo, b_router, tm):
    n_tok = x.shape[0]
    full = lambda shape: pl.BlockSpec(shape, lambda i: (0,) * len(shape))
    tile = lambda w: pl.BlockSpec((tm, w), lambda i: (i, 0))
    return pl.pallas_call(
        _outproj_router_kernel,
        grid=(n_tok // tm,),
        in_specs=[
            tile(CONV_DIM), tile(HEADS * DV), tile(D_MODEL),
            full((CONV_DIM, D_MODEL)), full((HEADS * DV, D_MODEL)), full((1, D_MODEL)),
            full((D_MODEL, LANES)), full((D_MODEL, LANES)), full((1, LANES)),
        ],
        out_specs=[tile(D_MODEL), tile(D_MODEL), tile(LANES), tile(LANES), tile(LANES), full((1, LANES))],
        out_shape=[
            jax.ShapeDtypeStruct((n_tok, D_MODEL), F32),
            jax.ShapeDtypeStruct((n_tok, D_MODEL), F32),
            jax.ShapeDtypeStruct((n_tok, LANES), jnp.int32),
            jax.ShapeDtypeStruct((n_tok, LANES), F32),
            jax.ShapeDtypeStruct((n_tok, LANES), jnp.int32),
            jax.ShapeDtypeStruct((1, LANES), F32),
        ],
        scratch_shapes=[pltpu.VMEM((1, LANES), F32)],
        compiler_params=_cparams(("arbitrary",)),
        name="outproj_router",
    )(y_conv, y_mlstm, x, wo1, wo2, g_ffn, wr_hi, wr_lo, b_router)


def _dispatch_kernel(dest_ref, xn_ref, xs_in, xs_out, sem):
    del xs_in
    tm = xn_ref.shape[0]
    base = pl.program_id(0) * tm

    def row_copy(r, d):
        return pltpu.make_async_copy(xn_ref.at[pl.ds(r, 1)], xs_out.at[pl.ds(d, 1)], sem)

    def issue(r, c):
        for kk in range(TOP_K):
            row_copy(r, dest_ref[(base + r) * TOP_K + kk]).start()
        return c

    def drain(r, c):
        for kk in range(TOP_K):
            row_copy(r, dest_ref[(base + r) * TOP_K + kk]).wait()
        return c

    lax.fori_loop(0, tm, issue, 0)
    lax.fori_loop(0, tm, drain, 0)


def _dispatch(dest_flat, xn, xs_init, tm):
    n_tok = xn.shape[0]
    return pl.pallas_call(
        _dispatch_kernel,
        grid_spec=pltpu.PrefetchScalarGridSpec(
            num_scalar_prefetch=1,
            grid=(n_tok // tm,),
            in_specs=[
                pl.BlockSpec((tm, D_MODEL), lambda i, d: (i, 0)),
                pl.BlockSpec(memory_space=pl.ANY),
            ],
            out_specs=pl.BlockSpec(memory_space=pl.ANY),
            scratch_shapes=[pltpu.SemaphoreType.DMA(())],
        ),
        out_shape=jax.ShapeDtypeStruct(xs_init.shape, xs_init.dtype),
        input_output_aliases={2: 0},
        compiler_params=_cparams(("arbitrary",)),
        name="dispatch",
    )(dest_flat, xn, xs_init)


def _expert_kernel(be_ref, br_ref, bv_ref, x_ref, wg_ref, wu_ref, bg_ref, bu_ref, wd_ref, bd_ref,
                   y_ref, xb_scr, acc):
    i, f = pl.program_id(0), pl.program_id(1)
    nf = pl.num_programs(1)

    @pl.when(bv_ref[i] == 1)
    def _():
        @pl.when(f == 0)
        def _():
            xb_scr[...] = x_ref[...].astype(BF16)
            acc[...] = jnp.broadcast_to(bd_ref[0], acc.shape)

        xb = xb_scr[...]
        g = jnp.dot(xb, wg_ref[0].astype(BF16), preferred_element_type=F32) + bg_ref[0]
        u = jnp.dot(xb, wu_ref[0].astype(BF16), preferred_element_type=F32) + bu_ref[0]
        g = jnp.minimum(g, SWIGLU_LIMIT)
        u = jnp.clip(u, -SWIGLU_LIMIT, SWIGLU_LIMIT)
        act = g * jax.nn.sigmoid(SWIGLU_ALPHA * g) * (u + 1.0)
        acc[...] += jnp.dot(act.astype(BF16), wd_ref[0].astype(BF16), preferred_element_type=F32)

        @pl.when(f == nf - 1)
        def _():
            y_ref[...] = acc[...]

    @pl.when((bv_ref[i] == 0) & (f == nf - 1))
    def _():
        y_ref[...] = jnp.zeros_like(y_ref)


def _experts(blk_e, blk_row, blk_valid, xs, w_gate_up, b_gate_up, w_down, b_down, tm, tf):
    n_slots = xs.shape[0]
    nb, nf = n_slots // tm, D_FF // tf

    def fidx(i, f, bv):
        return jnp.where(bv[i] == 1, f, nf - 1)

    return pl.pallas_call(
        _expert_kernel,
        grid_spec=pltpu.PrefetchScalarGridSpec(
            num_scalar_prefetch=3,
            grid=(nb, nf),
            in_specs=[
                pl.BlockSpec((tm, D_MODEL), lambda i, f, be, br, bv: (br[i], 0)),
                pl.BlockSpec((1, D_MODEL, tf), lambda i, f, be, br, bv: (be[i], 0, fidx(i, f, bv))),
                pl.BlockSpec((1, D_MODEL, tf), lambda i, f, be, br, bv: (be[i], 0, nf + fidx(i, f, bv))),
                pl.BlockSpec((1, 1, tf), lambda i, f, be, br, bv: (be[i], 0, fidx(i, f, bv))),
                pl.BlockSpec((1, 1, tf), lambda i, f, be, br, bv: (be[i], 0, nf + fidx(i, f, bv))),
                pl.BlockSpec((1, tf, D_MODEL), lambda i, f, be, br, bv: (be[i], fidx(i, f, bv), 0)),
                pl.BlockSpec((1, 1, D_MODEL), lambda i, f, be, br, bv: (be[i], 0, 0)),
            ],
            out_specs=pl.BlockSpec((tm, D_MODEL), lambda i, f, be, br, bv: (i, 0)),
            scratch_shapes=[pltpu.VMEM((tm, D_MODEL), BF16), pltpu.VMEM((tm, D_MODEL), F32)],
        ),
        out_shape=jax.ShapeDtypeStruct((n_slots, D_MODEL), F32),
        compiler_params=_cparams(("arbitrary", "arbitrary")),
        name="experts",
    )(blk_e, blk_row, blk_valid, xs, w_gate_up, w_gate_up, b_gate_up, b_gate_up, w_down, b_down)


def _combine_kernel(dest_ref, h1_ref, gate_ref, p_ref, yb_ref, wpg_ref, wpp_ref, gp_ref, gf_ref,
                    out_ref, gbuf, sem):
    tm = h1_ref.shape[0]
    base = pl.program_id(0) * tm

    def row_copy(r, kk):
        d = dest_ref[(base + r) * TOP_K + kk]
        return pltpu.make_async_copy(yb_ref.at[pl.ds(d, 1)], gbuf.at[kk, pl.ds(r, 1)], sem)

    def issue(r, c):
        for kk in range(TOP_K):
            row_copy(r, kk).start()
        return c

    def drain(r, c):
        for kk in range(TOP_K):
            row_copy(r, kk).wait()
        return c

    lax.fori_loop(0, tm, issue, 0)
    lax.fori_loop(0, tm, drain, 0)

    gate = gate_ref[...]
    moe = gate[:, 0:1] * gbuf[0]
    for kk in range(1, TOP_K):
        moe = moe + gate[:, kk:kk + 1] * gbuf[kk]
    h2 = h1_ref[...] + moe
    a = _rms(h2, gp_ref[...]).astype(BF16)
    pg = jax.nn.sigmoid(jnp.dot(a, wpg_ref[...], preferred_element_type=F32))
    pp = jnp.dot(p_ref[...].astype(BF16), wpp_ref[...], preferred_element_type=F32)
    h3 = h2 + pg * pp
    out_ref[...] = _rms(h3, gf_ref[...])


def _combine(dest_flat, h1, gate, p, yb, wpg, wpp, g_ple, g_final, tm):
    n_tok = h1.shape[0]
    full = lambda shape: pl.BlockSpec(shape, lambda i, d: (0,) * len(shape))
    tile = lambda w: pl.BlockSpec((tm, w), lambda i, d: (i, 0))
    return pl.pallas_call(
        _combine_kernel,
        grid_spec=pltpu.PrefetchScalarGridSpec(
            num_scalar_prefetch=1,
            grid=(n_tok // tm,),
            in_specs=[
                tile(D_MODEL), tile(LANES), tile(PLE_DIM),
                pl.BlockSpec(memory_space=pl.ANY),
                full((D_MODEL, D_MODEL)), full((PLE_DIM, D_MODEL)), full((1, D_MODEL)), full((1, D_MODEL)),
            ],
            out_specs=tile(D_MODEL),
            scratch_shapes=[pltpu.VMEM((TOP_K, tm, D_MODEL), F32), pltpu.SemaphoreType.DMA(())],
        ),
        out_shape=jax.ShapeDtypeStruct((n_tok, D_MODEL), F32),
        compiler_params=_cparams(("arbitrary",)),
        name="combine",
    )(dest_flat, h1, gate, p, yb, wpg, wpp, g_ple, g_final)


def kernel(x_prompt, x_sample, state_conv, state_mlstm_C, state_mlstm_n, state_mlstm_m, p_prompt, p_sample,
           g_mix, w_in, b_gates, w_conv, g_conv_out, g_mlstm_out, w_out, g_ffn, w_router, b_router,
           w_gate_up, b_gate_up, w_down, b_down, g_ple, w_ple_gate, w_ple_proj, g_final):
    batch, seq, _ = x_prompt.shape
    n_dec = x_sample.shape[0]
    assert w_in.shape[0] == 1 and x_sample.shape[1] == 1
    n_prompt = batch * seq
    n_tok = n_prompt + n_dec

    x = jnp.concatenate([x_prompt.reshape(n_prompt, D_MODEL), x_sample.reshape(n_dec, D_MODEL)], axis=0)
    p = jnp.concatenate([p_prompt[0].reshape(n_prompt, PLE_DIM), p_sample[0].reshape(n_dec, PLE_DIM)], axis=0)

    row = lambda a: a.reshape(1, -1)
    pad_lanes = lambda a: jnp.pad(a, ((0, 0), (0, LANES - a.shape[1])))
    w_main = w_in[0, :, :MAIN_COLS].astype(BF16)
    w_gates = pad_lanes(w_in[0, :, MAIN_COLS:]).astype(BF16)
    bg = pad_lanes(row(b_gates[0]))
    wc = jnp.pad(w_conv[0], ((0, 8 - w_conv.shape[1]), (0, 0)))
    g_norm = row(g_mlstm_out[0])

    z, zg = _inproj(x, row(g_mix[0]), w_main, w_gates, tm=832, tn=1024)

    yc_p, conv_p = _conv_prompt(z, wc, row(g_conv_out[0]), batch, seq, tt=512)
    ym_p, c_p, n_p, m_p = _mlstm_prompt(z, zg, bg, g_norm, batch, seq)
    yc_s, ym_s, conv_s, c_s, n_s, m_s = _mixer_sample(
        z, z[n_prompt:].reshape(n_dec, 1, MAIN_COLS), zg[n_prompt:].reshape(n_dec, 1, LANES),
        state_conv[0].reshape(n_dec, 2 * CONV_DIM), state_mlstm_C[0], state_mlstm_n[0],
        pad_lanes(state_mlstm_m[0]).reshape(n_dec, 1, LANES), wc, row(g_conv_out[0]), bg, g_norm,
        row0=n_prompt, n_dec=n_dec, sb=16)
    m_s = m_s[:, 0, :HEADS]
    y_conv = jnp.concatenate([yc_p, yc_s], axis=0)
    y_mlstm = jnp.concatenate([ym_p, ym_s.reshape(n_dec, HEADS * DV).astype(BF16)], axis=0)

    wo = w_out[0].astype(BF16)
    wr = pad_lanes(w_router[0])
    wr_hi = wr.astype(BF16)
    wr_lo = (wr - wr_hi.astype(F32)).astype(BF16)
    h1, xn, top_e, gate, rank, cnt = _outproj_router(
        y_conv, y_mlstm, x, wo[:CONV_DIM], wo[CONV_DIM:], row(g_ffn[0]), wr_hi, wr_lo,
        pad_lanes(row(b_router[0])), tm=320)

    tm = MOE_TM
    nb = -(-(n_tok * TOP_K) // tm) + N_EXPERTS
    counts = cnt[0, :N_EXPERTS].astype(jnp.int32)
    padded = (counts + tm - 1) // tm * tm
    pend = jnp.cumsum(padded)
    pstart = pend - padded
    n_used = pend[-1] // tm
    blk = jnp.minimum(jnp.arange(nb, dtype=jnp.int32), n_used - 1)
    blk_e = jnp.minimum(jnp.searchsorted(pend, blk * tm, side='right'), N_EXPERTS - 1).astype(jnp.int32)
    blk_valid = (jnp.arange(nb) < n_used).astype(jnp.int32)
    dest = (pstart[top_e[:, :TOP_K]] + rank[:, :TOP_K]).reshape(-1).astype(jnp.int32)

    xs = _dispatch(dest, xn, jnp.zeros((nb * tm, D_MODEL), F32), tm=640)
    yb = _experts(blk_e, blk, blk_valid, xs, w_gate_up[0], b_gate_up[0].reshape(N_EXPERTS, 1, 2 * D_FF),
                  w_down[0], b_down[0].reshape(N_EXPERTS, 1, D_MODEL), tm, MOE_TF)

    out = _combine(dest, h1, gate, p, yb, w_ple_gate[0].astype(BF16), w_ple_proj[0].astype(BF16),
                   row(g_ple[0]), row(g_final), tm=320)

    y_prompt = out[:n_prompt].reshape(batch, seq, D_MODEL)
    y_sample = out[n_prompt:].reshape(n_dec, 1, D_MODEL)
    return (y_prompt, y_sample,
            conv_p[None], c_p[None], n_p[None], m_p[None, :, :, 0],
            conv_s.reshape(1, n_dec, 2, CONV_DIM), c_s[None], n_s[None], m_s[None])
```

```python
import functools

import jax
import jax.numpy as jnp
from jax import lax
from jax.experimental import pallas as pl
from jax.experimental.pallas import tpu as pltpu

F32 = jnp.float32
BF16 = jnp.bfloat16

D_MODEL = 2048
CONV_DIM = 1024
HEADS = 4
DK = 128
DV = 256
CHUNK = 128
N_EXPERTS = 32
TOP_K = 4
D_FF = 2048
PLE_DIM = 256
SWIGLU_LIMIT = 7.0
SWIGLU_ALPHA = 1.702
EPS = 1e-6
LANES = 128
NEG_BIG = -1e30
MAIN_COLS = 3 * CONV_DIM + HEADS * (2 * DK + 2 * DV)

COL_B, COL_C, COL_H = 0, CONV_DIM, 2 * CONV_DIM
COL_Q = 3 * CONV_DIM
COL_K = COL_Q + HEADS * DK
COL_V = COL_K + HEADS * DK
COL_O = COL_V + HEADS * DV

VMEM_LIMIT = 56 * 1024 * 1024

SUB = 128
MOE_TMB = 9 * SUB
MOE_TF = 256


def _cparams(sem, vmem=VMEM_LIMIT):
    return pltpu.CompilerParams(dimension_semantics=sem, vmem_limit_bytes=vmem)


def _rms(x, g):
    return x * lax.rsqrt(jnp.mean(x * x, axis=-1, keepdims=True) + EPS) * g


def _split3(x):
    x1 = x.astype(BF16)
    r1 = x - x1.astype(F32)
    x2 = r1.astype(BF16)
    x3 = (r1 - x2.astype(F32)).astype(BF16)
    return x1, x2, x3


def _log_sigmoid(x):
    return jnp.minimum(x, 0.0) - jnp.log(1.0 + jnp.exp(-jnp.abs(x)))


def _inproj_kernel(x_ref, g_ref, w_ref, wg_ref, z_ref, zg_ref, a_scr):
    @pl.when(pl.program_id(1) == 0)
    def _():
        a = _rms(x_ref[...], g_ref[...]).astype(BF16)
        a_scr[...] = a
        zg_ref[...] = jnp.dot(a, wg_ref[...], preferred_element_type=F32)

    z_ref[...] = jnp.dot(a_scr[...], w_ref[...], preferred_element_type=F32)


def _inproj(x, g, w_main, w_gates, tm, tn):
    n_tok = x.shape[0]
    return pl.pallas_call(
        _inproj_kernel,
        grid=(n_tok // tm, MAIN_COLS // tn),
        in_specs=[
            pl.BlockSpec((tm, D_MODEL), lambda i, j: (i, 0)),
            pl.BlockSpec((1, D_MODEL), lambda i, j: (0, 0)),
            pl.BlockSpec((D_MODEL, tn), lambda i, j: (0, j)),
            pl.BlockSpec((D_MODEL, LANES), lambda i, j: (0, 0)),
        ],
        out_specs=[
            pl.BlockSpec((tm, tn), lambda i, j: (i, j)),
            pl.BlockSpec((tm, LANES), lambda i, j: (i, 0)),
        ],
        out_shape=[
            jax.ShapeDtypeStruct((n_tok, MAIN_COLS), F32),
            jax.ShapeDtypeStruct((n_tok, LANES), F32),
        ],
        scratch_shapes=[pltpu.VMEM((tm, D_MODEL), BF16)],
        compiler_params=_cparams(("parallel", "arbitrary")),
        name="inproj",
    )(x, g, w_main, w_gates)


def _conv_prompt_kernel(zb_ref, zc_ref, zh_ref, wc_ref, g_ref, y_ref, st_ref, carry):
    tt = zb_ref.shape[0]

    @pl.when(pl.program_id(1) == 0)
    def _():
        carry[...] = jnp.zeros_like(carry)

    u = zc_ref[...] * zh_ref[...]
    prev = carry[...]
    row = lax.broadcasted_iota(jnp.int32, (tt, 1), 0)
    u1 = jnp.where(row == 0, prev[7:8], pltpu.roll(u, 1, axis=0))
    u2 = jnp.where(row == 0, prev[6:7], jnp.where(row == 1, prev[7:8], pltpu.roll(u, 2, axis=0)))
    wc = wc_ref[...]
    conv = wc[0:1] * u2 + wc[1:2] * u1 + wc[2:3] * u
    y_ref[...] = _rms(zb_ref[...] * conv, g_ref[...]).astype(BF16)
    carry[...] = u[tt - 8:tt]
    st_ref[0] = u[tt - 2:tt]


def _conv_prompt(z, w_conv, g_conv, batch, seq, tt):
    nt = seq // tt
    n_tok = batch * seq
    cb = CONV_DIM
    return pl.pallas_call(
        _conv_prompt_kernel,
        grid=(batch, nt),
        in_specs=[
            pl.BlockSpec((tt, cb), lambda b, t: (b * nt + t, COL_B // cb)),
            pl.BlockSpec((tt, cb), lambda b, t: (b * nt + t, COL_C // cb)),
            pl.BlockSpec((tt, cb), lambda b, t: (b * nt + t, COL_H // cb)),
            pl.BlockSpec((8, cb), lambda b, t: (0, 0)),
            pl.BlockSpec((1, cb), lambda b, t: (0, 0)),
        ],
        out_specs=[
            pl.BlockSpec((tt, cb), lambda b, t: (b * nt + t, 0)),
            pl.BlockSpec((1, 2, cb), lambda b, t: (b, 0, 0)),
        ],
        out_shape=[
            jax.ShapeDtypeStruct((n_tok, cb), BF16),
            jax.ShapeDtypeStruct((batch, 2, cb), F32),
        ],
        scratch_shapes=[pltpu.VMEM((8, cb), F32)],
        compiler_params=_cparams(("parallel", "arbitrary")),
        name="conv_prompt",
    )(z, z, z, w_conv, g_conv)


def _tri_dot(tri_bf16, x, tri_first):
    out = None
    for part in _split3(x):
        d = (jnp.dot(tri_bf16, part, preferred_element_type=F32) if tri_first
             else jnp.dot(part, tri_bf16, preferred_element_type=F32))
        out = d if out is None else out + d
    return out


def _mlstm_prompt_kernel(q_ref, k_ref, v_ref, o_ref, zg_ref, bg_ref, gn_ref,
                         y_ref, c_out, n_out, m_out, c_scr, n_scr, m_scr):
    L = CHUNK

    @pl.when(pl.program_id(1) == 0)
    def _():
        c_scr[...] = jnp.zeros_like(c_scr)
        n_scr[...] = jnp.zeros_like(n_scr)
        m_scr[...] = jnp.zeros_like(m_scr)

    lane = lax.broadcasted_iota(jnp.int32, (L, LANES), 1)
    r_i = lax.broadcasted_iota(jnp.int32, (L, L), 0)
    c_i = lax.broadcasted_iota(jnp.int32, (L, L), 1)
    causal = c_i <= r_i
    tril = causal.astype(BF16)
    triu = (r_i <= c_i).astype(BF16)

    zg = zg_ref[...] + bg_ref[...]
    gates = jnp.where(lane < HEADS, zg, _log_sigmoid(zg))
    gates_t = gates.T
    bcol_all = _tri_dot(tril, gates, True)
    brow_all = _tri_dot(triu, gates_t, False)

    for h in range(HEADS):
        b_col = bcol_all[:, HEADS + h:HEADS + h + 1]
        b_row = brow_all[HEADS + h:HEADS + h + 1, :]
        i_col = gates[:, h:h + 1]
        i_row = gates_t[h:h + 1, :]
        b_end = b_col[L - 1:L, :]
        m0 = m_scr[h:h + 1, 0:1]
        n0 = n_scr[h:h + 1, :]
        c0 = c_scr[h]

        q = q_ref[:, h * DK:(h + 1) * DK]
        k = k_ref[:, h * DK:(h + 1) * DK] * (DK ** -0.5)
        v = v_ref[:, h * DV:(h + 1) * DV]
        qb, vb = q.astype(BF16), v.astype(BF16)

        dmat = jnp.where(causal, b_col - b_row + i_row, NEG_BIG)
        inter = b_col + m0
        m_t = jnp.maximum(inter, jnp.max(dmat, axis=1, keepdims=True))
        w_int = jnp.exp(inter - m_t)
        p = jnp.where(causal, jnp.exp(dmat - m_t), 0.0)
        s = lax.dot_general(qb, k.astype(BF16), (((1,), (1,)), ((), ())),
                            preferred_element_type=F32) * p
        num = (jnp.dot(s.astype(BF16), vb, preferred_element_type=F32)
               + w_int * jnp.dot(qb, c0.astype(BF16), preferred_element_type=F32))
        den = jnp.sum(s, axis=1, keepdims=True) + w_int * jnp.sum(q * n0, axis=1, keepdims=True)
        hh = num / jnp.maximum(jnp.abs(den), jnp.exp(-m_t))
        hn = hh * lax.rsqrt(jnp.mean(hh * hh, axis=-1, keepdims=True) + EPS)
        hn = hn * gn_ref[:, h * DV:(h + 1) * DV]
        y = jax.nn.sigmoid(o_ref[:, h * DV:(h + 1) * DV]) * hn
        y_ref[:, h * DV:(h + 1) * DV] = y.astype(BF16)

        g_col = b_end - b_col + i_col
        g_row = b_end - b_row + i_row
        m_new = jnp.maximum(b_end + m0, jnp.max(g_row, axis=1, keepdims=True))
        decay = jnp.exp(b_end + m0 - m_new)
        kw = k * jnp.exp(g_col - m_new)
        c_new = decay * c0 + lax.dot_general(kw.astype(BF16), vb, (((0,), (0,)), ((), ())),
                                             preferred_element_type=F32)
        n_new = decay * n0 + jnp.sum(kw, axis=0, keepdims=True)
        c_scr[h] = c_new
        n_scr[h:h + 1, :] = n_new
        m_scr[h:h + 1, :] = jnp.broadcast_to(m_new, (1, LANES))
        c_out[0, h] = c_new
        n_out[0, h:h + 1, :] = n_new
        m_out[0, h:h + 1, :] = jnp.broadcast_to(m_new, (1, LANES))


def _mlstm_prompt(z, zg, b_gates, g_norm, batch, seq):
    nc = seq // CHUNK
    n_tok = batch * seq
    qw, vw = HEADS * DK, HEADS * DV
    return pl.pallas_call(
        _mlstm_prompt_kernel,
        grid=(batch, nc),
        in_specs=[
            pl.BlockSpec((CHUNK, qw), lambda b, c: (b * nc + c, COL_Q // qw)),
            pl.BlockSpec((CHUNK, qw), lambda b, c: (b * nc + c, COL_K // qw)),
            pl.BlockSpec((CHUNK, vw), lambda b, c: (b * nc + c, COL_V // vw)),
            pl.BlockSpec((CHUNK, vw), lambda b, c: (b * nc + c, COL_O // vw)),
            pl.BlockSpec((CHUNK, LANES), lambda b, c: (b * nc + c, 0)),
            pl.BlockSpec((1, LANES), lambda b, c: (0, 0)),
            pl.BlockSpec((1, vw), lambda b, c: (0, 0)),
        ],
        out_specs=[
            pl.BlockSpec((CHUNK, vw), lambda b, c: (b * nc + c, 0)),
            pl.BlockSpec((1, HEADS, DK, DV), lambda b, c: (b, 0, 0, 0)),
            pl.BlockSpec((1, HEADS, DK), lambda b, c: (b, 0, 0)),
            pl.BlockSpec((1, HEADS, LANES), lambda b, c: (b, 0, 0)),
        ],
        out_shape=[
            jax.ShapeDtypeStruct((n_tok, vw), BF16),
            jax.ShapeDtypeStruct((batch, HEADS, DK, DV), F32),
            jax.ShapeDtypeStruct((batch, HEADS, DK), F32),
            jax.ShapeDtypeStruct((batch, HEADS, LANES), F32),
        ],
        scratch_shapes=[
            pltpu.VMEM((HEADS, DK, DV), F32),
            pltpu.VMEM((8, DK), F32),
            pltpu.VMEM((8, LANES), F32),
        ],
        compiler_params=_cparams(("parallel", "arbitrary")),
        name="mlstm_prompt",
    )(z, z, z, z, zg, b_gates, g_norm)


def _mixer_sample_kernel(zb_ref, zc_ref, zh_ref, q_ref, k_ref, v_ref, o_ref, zg_ref,
                         st_ref, c_ref, n_ref, m_ref, wc_ref, gc_ref, bg_ref, gn_ref,
                         yc_ref, ym_ref, st_out, c_out, n_out, m_out):
    sb = zb_ref.shape[0]
    u = zc_ref[...] * zh_ref[...]
    s0 = st_ref[:, 0:CONV_DIM]
    s1 = st_ref[:, CONV_DIM:2 * CONV_DIM]
    wc = wc_ref[...]
    conv = wc[0:1] * s0 + wc[1:2] * s1 + wc[2:3] * u
    yc_ref[...] = _rms(zb_ref[...] * conv, gc_ref[...]).astype(BF16)
    st_out[:, 0:CONV_DIM] = s1
    st_out[:, CONV_DIM:2 * CONV_DIM] = u

    lane_row = lax.broadcasted_iota(jnp.int32, (1, LANES), 1)
    eye = (lax.broadcasted_iota(jnp.int32, (DK, DK), 0)
           == lax.broadcasted_iota(jnp.int32, (DK, DK), 1)).astype(F32)

    def to_col(row):
        return jnp.sum(eye * row, axis=1, keepdims=True)

    def pick(row, j):
        return jnp.sum(jnp.where(lane_row == j, row, 0.0), axis=1, keepdims=True)

    def per_sample(s, carry):
        zg = zg_ref[s] + bg_ref[...]
        lf_all = _log_sigmoid(zg)
        m_row = m_ref[s]
        m_new_row = jnp.zeros((1, LANES), F32)
        for h in range(HEADS):
            q = q_ref[s, :, h * DK:(h + 1) * DK]
            k = k_ref[s, :, h * DK:(h + 1) * DK] * (DK ** -0.5)
            v = v_ref[s, :, h * DV:(h + 1) * DV]
            o = o_ref[s, :, h * DV:(h + 1) * DV]
            i_pre = pick(zg, h)
            lf = pick(lf_all, HEADS + h)
            m0 = pick(m_row, h)
            c0 = c_ref[s, h]
            n0 = n_ref[s, h:h + 1, :]
            inter = lf + m0
            m_t = jnp.maximum(inter, i_pre)
            w_int = jnp.exp(inter - m_t)
            p = jnp.exp(i_pre - m_t)
            sc = jnp.sum(q * k, axis=1, keepdims=True) * p
            qc = jnp.sum(to_col(q) * c0, axis=0, keepdims=True)
            num = sc * v + w_int * qc
            den = sc + w_int * jnp.sum(q * n0, axis=1, keepdims=True)
            hh = num / jnp.maximum(jnp.abs(den), jnp.exp(-m_t))
            hn = hh * lax.rsqrt(jnp.mean(hh * hh, axis=-1, keepdims=True) + EPS)
            hn = hn * gn_ref[:, h * DV:(h + 1) * DV]
            ym_ref[s, :, h * DV:(h + 1) * DV] = jax.nn.sigmoid(o) * hn
            c_out[s, h] = w_int * c0 + (p * to_col(k)) * v
            n_out[s, h:h + 1, :] = w_int * n0 + p * k
            m_new_row = jnp.where(lane_row == h, m_t, m_new_row)
        m_out[s] = m_new_row
        return carry

    lax.fori_loop(0, sb, per_sample, 0)


def _mixer_sample(z, zs3, zg3, st, c_s, n_s, m3, w_conv, g_conv, b_gates, g_norm, row0, n_dec, sb):
    cb, qw, vw = CONV_DIM, HEADS * DK, HEADS * DV
    r0 = row0 // sb
    row = lambda i: r0 + i
    return pl.pallas_call(
        _mixer_sample_kernel,
        grid=(n_dec // sb,),
        in_specs=[
            pl.BlockSpec((sb, cb), lambda i: (row(i), COL_B // cb)),
            pl.BlockSpec((sb, cb), lambda i: (row(i), COL_C // cb)),
            pl.BlockSpec((sb, cb), lambda i: (row(i), COL_H // cb)),
            pl.BlockSpec((sb, 1, qw), lambda i: (i, 0, COL_Q // qw)),
            pl.BlockSpec((sb, 1, qw), lambda i: (i, 0, COL_K // qw)),
            pl.BlockSpec((sb, 1, vw), lambda i: (i, 0, COL_V // vw)),
            pl.BlockSpec((sb, 1, vw), lambda i: (i, 0, COL_O // vw)),
            pl.BlockSpec((sb, 1, LANES), lambda i: (i, 0, 0)),
            pl.BlockSpec((sb, 2 * cb), lambda i: (i, 0)),
            pl.BlockSpec((sb, HEADS, DK, DV), lambda i: (i, 0, 0, 0)),
            pl.BlockSpec((sb, HEADS, DK), lambda i: (i, 0, 0)),
            pl.BlockSpec((sb, 1, LANES), lambda i: (i, 0, 0)),
            pl.BlockSpec((8, cb), lambda i: (0, 0)),
            pl.BlockSpec((1, cb), lambda i: (0, 0)),
            pl.BlockSpec((1, LANES), lambda i: (0, 0)),
            pl.BlockSpec((1, vw), lambda i: (0, 0)),
        ],
        out_specs=[
            pl.BlockSpec((sb, cb), lambda i: (i, 0)),
            pl.BlockSpec((sb, 1, vw), lambda i: (i, 0, 0)),
            pl.BlockSpec((sb, 2 * cb), lambda i: (i, 0)),
            pl.BlockSpec((sb, HEADS, DK, DV), lambda i: (i, 0, 0, 0)),
            pl.BlockSpec((sb, HEADS, DK), lambda i: (i, 0, 0)),
            pl.BlockSpec((sb, 1, LANES), lambda i: (i, 0, 0)),
        ],
        out_shape=[
            jax.ShapeDtypeStruct((n_dec, cb), BF16),
            jax.ShapeDtypeStruct((n_dec, 1, vw), F32),
            jax.ShapeDtypeStruct((n_dec, 2 * cb), F32),
            jax.ShapeDtypeStruct((n_dec, HEADS, DK, DV), F32),
            jax.ShapeDtypeStruct((n_dec, HEADS, DK), F32),
            jax.ShapeDtypeStruct((n_dec, 1, LANES), F32),
        ],
        compiler_params=_cparams(("arbitrary",)),
        name="mixer_sample",
    )(z, z, z, zs3, zs3, zs3, zs3, zg3, st, c_s, n_s, m3, w_conv, g_conv, b_gates, g_norm)


def _outproj_router_kernel(yc_ref, ym_ref, x_ref, wo1_ref, wo2_ref, g_ref, wr_hi_ref, wr_lo_ref, br_ref,
                           h1_ref, xn_ref, e_ref, gate_ref, rank_ref, cnt_ref, carry):
    tm = x_ref.shape[0]

    @pl.when(pl.program_id(0) == 0)
    def _():
        carry[...] = jnp.zeros_like(carry)

    mix = (jnp.dot(yc_ref[...], wo1_ref[...], preferred_element_type=F32)
           + jnp.dot(ym_ref[...], wo2_ref[...], preferred_element_type=F32))
    h1 = x_ref[...] + mix
    h1_ref[...] = h1
    xn = _rms(h1, g_ref[...])
    xn_ref[...] = xn

    xh = xn.astype(BF16)
    xl = (xn - xh.astype(F32)).astype(BF16)
    logits = (jnp.dot(xh, wr_hi_ref[...], preferred_element_type=F32)
              + jnp.dot(xl, wr_hi_ref[...], preferred_element_type=F32)
              + jnp.dot(xh, wr_lo_ref[...], preferred_element_type=F32)) + br_ref[...]
    lane = lax.broadcasted_iota(jnp.int32, (tm, LANES), 1)
    lane_f = lane.astype(F32)
    work = jnp.where(lane < N_EXPERTS, logits, NEG_BIG)

    tops, idxs = [], []
    chosen = jnp.zeros((tm, LANES), F32)
    for _ in range(TOP_K):
        mx = jnp.max(work, axis=1, keepdims=True)
        idx = jnp.min(jnp.where(work == mx, lane_f, float(LANES)), axis=1, keepdims=True)
        sel = lane_f == idx
        tops.append(mx)
        idxs.append(idx)
        chosen = jnp.where(sel, 1.0, chosen)
        work = jnp.where(sel, NEG_BIG, work)

    exps = [jnp.exp(t - tops[0]) for t in tops]
    denom = exps[0] + exps[1] + exps[2] + exps[3]

    r_i = lax.broadcasted_iota(jnp.int32, (tm, tm), 0)
    c_i = lax.broadcasted_iota(jnp.int32, (tm, tm), 1)
    before = jnp.dot((c_i < r_i).astype(BF16), chosen.astype(BF16),
                     preferred_element_type=F32) + carry[...]

    e_out = jnp.zeros((tm, LANES), F32)
    g_out = jnp.zeros((tm, LANES), F32)
    r_out = jnp.zeros((tm, LANES), F32)
    for kk in range(TOP_K):
        rank = jnp.sum(jnp.where(lane_f == idxs[kk], before, 0.0), axis=1, keepdims=True)
        e_out = jnp.where(lane == kk, idxs[kk], e_out)
        g_out = jnp.where(lane == kk, exps[kk] / denom, g_out)
        r_out = jnp.where(lane == kk, rank, r_out)
    e_ref[...] = e_out.astype(jnp.int32)
    gate_ref[...] = g_out
    rank_ref[...] = r_out.astype(jnp.int32)

    new_carry = carry[...] + jnp.sum(chosen, axis=0, keepdims=True)
    carry[...] = new_carry
    cnt_ref[...] = new_carry


def _outproj_router(y_conv, y_mlstm, x, wo1, wo2, g_ffn, wr_hi, wr_lo, b_router, tm):
    n_tok = x.shape[0]
    full = lambda shape: pl.BlockSpec(shape, lambda i: (0,) * len(shape))
    tile = lambda w: pl.BlockSpec((tm, w), lambda i: (i, 0))
    return pl.pallas_call(
        _outproj_router_kernel,
        grid=(n_tok // tm,),
        in_specs=[
            tile(CONV_DIM), tile(HEADS * DV), tile(D_MODEL),
            full((CONV_DIM, D_MODEL)), full((HEADS * DV, D_MODEL)), full((1, D_MODEL)),
            full((D_MODEL, LANES)), full((D_MODEL, LANES)), full((1, LANES)),
        ],
        out_specs=[tile(D_MODEL), tile(D_MODEL), tile(LANES), tile(LANES), tile(LANES), full((1, LANES))],
        out_shape=[
            jax.ShapeDtypeStruct((n_tok, D_MODEL), F32),
            jax.ShapeDtypeStruct((n_tok, D_MODEL), F32),
            jax.ShapeDtypeStruct((n_tok, LANES), jnp.int32),
            jax.ShapeDtypeStruct((n_tok, LANES), F32),
            jax.ShapeDtypeStruct((n_tok, LANES), jnp.int32),
            jax.ShapeDtypeStruct((1, LANES), F32),
        ],
        scratch_shapes=[pltpu.VMEM((1, LANES), F32)],
        compiler_params=_cparams(("arbitrary",)),
        name="outproj_router",
    )(y_conv, y_mlstm, x, wo1, wo2, g_ffn, wr_hi, wr_lo, b_router)


def _zero_fill_rows(dst, zero_scr, sem, start, count, act):
    low = count & 7
    for t in range(7):
        @pl.when(t < low)
        def _(t=t):
            act(pltpu.make_async_copy(zero_scr.at[pl.ds(0, 1)], dst.at[pl.ds(start + t, 1)], sem))

    off = start + low
    bit = 8
    while bit < SUB:
        take = count & bit

        @pl.when(take != 0)
        def _(off=off, bit=bit):
            row = pl.multiple_of(off, 8)
            act(pltpu.make_async_copy(zero_scr.at[pl.ds(0, bit)], dst.at[pl.ds(row, bit)], sem))

        off = off + take
        bit *= 2


def _zero_fill_chunks(dst, zero_scr, sem, start, act):
    def body(j, c):
        row = pl.multiple_of(start + j * SUB, SUB)
        act(pltpu.make_async_copy(zero_scr, dst.at[pl.ds(row, SUB)], sem))
        return c

    lax.fori_loop(0, (dst.shape[0] - start) // SUB, body, 0)


def _dispatch_kernel(dest_ref, cnt_ref, pstart_ref, xn_ref, xs_out, zero_scr, sem, zsem):
    tm = xn_ref.shape[0]
    step = pl.program_id(0)
    base = step * tm

    def fill(act):
        for e in range(N_EXPERTS):
            c = cnt_ref[e]
            _zero_fill_rows(xs_out, zero_scr, zsem, pstart_ref[e] + c, (SUB - c % SUB) % SUB, act)
        _zero_fill_chunks(xs_out, zero_scr, zsem, pstart_ref[N_EXPERTS], act)

    @pl.when(step == 0)
    def _():
        zero_scr[...] = jnp.zeros_like(zero_scr)
        fill(lambda cp: cp.start())

    def row_copy(r, d):
        return pltpu.make_async_copy(xn_ref.at[pl.ds(r, 1)], xs_out.at[pl.ds(d, 1)], sem)

    def issue(r, c):
        for kk in range(TOP_K):
            row_copy(r, dest_ref[(base + r) * TOP_K + kk]).start()
        return c

    def drain(r, c):
        for kk in range(TOP_K):
            row_copy(r, dest_ref[(base + r) * TOP_K + kk]).wait()
        return c

    lax.fori_loop(0, tm, issue, 0)
    lax.fori_loop(0, tm, drain, 0)

    @pl.when(step == 0)
    def _():
        fill(lambda cp: cp.wait())


def _dispatch(dest_flat, counts, pstart, xn, n_slots, tm):
    n_tok = xn.shape[0]
    return pl.pallas_call(
        _dispatch_kernel,
        grid_spec=pltpu.PrefetchScalarGridSpec(
            num_scalar_prefetch=3,
            grid=(n_tok // tm,),
            in_specs=[pl.BlockSpec((tm, D_MODEL), lambda i, d, c, p: (i, 0))],
            out_specs=pl.BlockSpec(memory_space=pl.ANY),
            scratch_shapes=[pltpu.VMEM((SUB, D_MODEL), F32), pltpu.SemaphoreType.DMA(()),
                            pltpu.SemaphoreType.DMA(())],
        ),
        out_shape=jax.ShapeDtypeStruct((n_slots, D_MODEL), F32),
        compiler_params=_cparams(("arbitrary",)),
        name="dispatch",
    )(dest_flat, counts, pstart, xn)


def _expert_kernel(se_ref, st_ref, sn_ref, end_ref, x_ref, wg_ref, wu_ref, bg_ref, bu_ref, wd_ref, bd_ref,
                   y_ref, xb_scr, acc, zero_scr, sem, zsem):
    i, f = pl.program_id(0), pl.program_id(1)
    nf = pl.num_programs(1)
    tmb = acc.shape[0]
    n = sn_ref[i]
    start = st_ref[i]

    @pl.when((i == 0) & (f == 0))
    def _():
        zero_scr[...] = jnp.zeros_like(zero_scr)
        _zero_fill_chunks(y_ref, zero_scr, zsem, end_ref[0], lambda cp: cp.start())
        _zero_fill_chunks(y_ref, zero_scr, zsem, end_ref[0], lambda cp: cp.wait())

    def y_copy(o, m):
        row = pl.multiple_of(start + o, SUB)
        return pltpu.make_async_copy(acc.at[pl.ds(o, m)], y_ref.at[pl.ds(row, m)], sem)

    def for_each_sub_block(fn):
        for o in range(0, tmb, 2 * SUB):
            rem = n - o
            if o + 2 * SUB <= tmb:
                pl.when(rem > SUB)(functools.partial(fn, o, 2 * SUB))
            pl.when((rem > 0) & (rem <= SUB))(functools.partial(fn, o, SUB))

    @pl.when(n > 0)
    def _():
        @pl.when(f == 0)
        def _():
            xb_scr[...] = x_ref[...].astype(BF16)
            acc[...] = jnp.broadcast_to(bd_ref[0], acc.shape)

        wg = wg_ref[0].astype(BF16)
        wu = wu_ref[0].astype(BF16)
        wd = wd_ref[0].astype(BF16)
        bg = bg_ref[0]
        bu = bu_ref[0]

        def sub_block(o, m):
            xb = xb_scr[o:o + m]
            g = jnp.dot(xb, wg, preferred_element_type=F32) + bg
            u = jnp.dot(xb, wu, preferred_element_type=F32) + bu
            g = jnp.minimum(g, SWIGLU_LIMIT)
            u = jnp.clip(u, -SWIGLU_LIMIT, SWIGLU_LIMIT)
            act = g * jax.nn.sigmoid(SWIGLU_ALPHA * g) * (u + 1.0)
            acc[o:o + m] += jnp.dot(act.astype(BF16), wd, preferred_element_type=F32)

            @pl.when(f == nf - 1)
            def _():
                y_copy(o, m).start()

        for_each_sub_block(sub_block)

        @pl.when(f == nf - 1)
        def _():
            for_each_sub_block(lambda o, m: y_copy(o, m).wait())


def _experts(sb_e, sb_start, sb_n, end, xs, w_gate_up, b_gate_up, w_down, b_down, nsb, tmb, tf):
    n_slots = xs.shape[0]
    nf = D_FF // tf

    def fidx(i, f, sn):
        return jnp.where(sn[i] > 0, f, nf - 1)

    return pl.pallas_call(
        _expert_kernel,
        grid_spec=pltpu.PrefetchScalarGridSpec(
            num_scalar_prefetch=4,
            grid=(nsb, nf),
            in_specs=[
                pl.BlockSpec((pl.Element(tmb), pl.Element(D_MODEL)),
                             lambda i, f, se, st, sn, en: (pl.multiple_of(st[i], SUB), 0)),
                pl.BlockSpec((1, D_MODEL, tf), lambda i, f, se, st, sn, en: (se[i], 0, fidx(i, f, sn))),
                pl.BlockSpec((1, D_MODEL, tf), lambda i, f, se, st, sn, en: (se[i], 0, nf + fidx(i, f, sn))),
                pl.BlockSpec((1, 1, tf), lambda i, f, se, st, sn, en: (se[i], 0, fidx(i, f, sn))),
                pl.BlockSpec((1, 1, tf), lambda i, f, se, st, sn, en: (se[i], 0, nf + fidx(i, f, sn))),
                pl.BlockSpec((1, tf, D_MODEL), lambda i, f, se, st, sn, en: (se[i], fidx(i, f, sn), 0)),
                pl.BlockSpec((1, 1, D_MODEL), lambda i, f, se, st, sn, en: (se[i], 0, 0)),
            ],
            out_specs=pl.BlockSpec(memory_space=pl.ANY),
            scratch_shapes=[pltpu.VMEM((tmb, D_MODEL), BF16), pltpu.VMEM((tmb, D_MODEL), F32),
                            pltpu.VMEM((SUB, D_MODEL), F32),
                            pltpu.SemaphoreType.DMA(()), pltpu.SemaphoreType.DMA(())],
        ),
        out_shape=jax.ShapeDtypeStruct((n_slots, D_MODEL), F32),
        compiler_params=_cparams(("arbitrary", "arbitrary")),
        name="experts",
    )(sb_e, sb_start, sb_n, end, xs, w_gate_up, w_gate_up, b_gate_up, b_gate_up, w_down, b_down)


def _combine_kernel(dest_ref, h1_ref, gate_ref, p_ref, yb_ref, wpg_ref, wpp_ref, gp_ref, gf_ref,
                    out_ref, gbuf, sem):
    tm = h1_ref.shape[0]
    base = pl.program_id(0) * tm

    def row_copy(r, kk):
        d = dest_ref[(base + r) * TOP_K + kk]
        return pltpu.make_async_copy(yb_ref.at[pl.ds(d, 1)], gbuf.at[kk, pl.ds(r, 1)], sem)

    def issue(r, c):
        for kk in range(TOP_K):
            row_copy(r, kk).start()
        return c

    def drain(r, c):
        for kk in range(TOP_K):
            row_copy(r, kk).wait()
        return c

    lax.fori_loop(0, tm, issue, 0)
    lax.fori_loop(0, tm, drain, 0)

    gate = gate_ref[...]
    moe = gate[:, 0:1] * gbuf[0]
    for kk in range(1, TOP_K):
        moe = moe + gate[:, kk:kk + 1] * gbuf[kk]
    h2 = h1_ref[...] + moe
    a = _rms(h2, gp_ref[...]).astype(BF16)
    pg = jax.nn.sigmoid(jnp.dot(a, wpg_ref[...], preferred_element_type=F32))
    pp = jnp.dot(p_ref[...].astype(BF16), wpp_ref[...], preferred_element_type=F32)
    h3 = h2 + pg * pp
    out_ref[...] = _rms(h3, gf_ref[...])


def _combine(dest_flat, h1, gate, p, yb, wpg, wpp, g_ple, g_final, tm):
    n_tok = h1.shape[0]
    full = lambda shape: pl.BlockSpec(shape, lambda i, d: (0,) * len(shape))
    tile = lambda w: pl.BlockSpec((tm, w), lambda i, d: (i, 0))
    return pl.pallas_call(
        _combine_kernel,
        grid_spec=pltpu.PrefetchScalarGridSpec(
            num_scalar_prefetch=1,
            grid=(n_tok // tm,),
            in_specs=[
                tile(D_MODEL), tile(LANES), tile(PLE_DIM),
                pl.BlockSpec(memory_space=pl.ANY),
                full((D_MODEL, D_MODEL)), full((PLE_DIM, D_MODEL)), full((1, D_MODEL)), full((1, D_MODEL)),
            ],
            out_specs=tile(D_MODEL),
            scratch_shapes=[pltpu.VMEM((TOP_K, tm, D_MODEL), F32), pltpu.SemaphoreType.DMA(())],
        ),
        out_shape=jax.ShapeDtypeStruct((n_tok, D_MODEL), F32),
        compiler_params=_cparams(("arbitrary",)),
        name="combine",
    )(dest_flat, h1, gate, p, yb, wpg, wpp, g_ple, g_final)


def kernel(x_prompt, x_sample, state_conv, state_mlstm_C, state_mlstm_n, state_mlstm_m, p_prompt, p_sample,
           g_mix, w_in, b_gates, w_conv, g_conv_out, g_mlstm_out, w_out, g_ffn, w_router, b_router,
           w_gate_up, b_gate_up, w_down, b_down, g_ple, w_ple_gate, w_ple_proj, g_final):
    batch, seq, _ = x_prompt.shape
    n_dec = x_sample.shape[0]
    assert w_in.shape[0] == 1 and x_sample.shape[1] == 1
    n_prompt = batch * seq
    n_tok = n_prompt + n_dec

    x = jnp.concatenate([x_prompt.reshape(n_prompt, D_MODEL), x_sample.reshape(n_dec, D_MODEL)], axis=0)
    p = jnp.concatenate([p_prompt[0].reshape(n_prompt, PLE_DIM), p_sample[0].reshape(n_dec, PLE_DIM)], axis=0)

    row = lambda a: a.reshape(1, -1)
    pad_lanes = lambda a: jnp.pad(a, ((0, 0), (0, LANES - a.shape[1])))
    w_main = w_in[0, :, :MAIN_COLS].astype(BF16)
    w_gates = pad_lanes(w_in[0, :, MAIN_COLS:]).astype(BF16)
    bg = pad_lanes(row(b_gates[0]))
    wc = jnp.pad(w_conv[0], ((0, 8 - w_conv.shape[1]), (0, 0)))
    g_norm = row(g_mlstm_out[0])

    z, zg = _inproj(x, row(g_mix[0]), w_main, w_gates, tm=832, tn=1024)

    yc_p, conv_p = _conv_prompt(z, wc, row(g_conv_out[0]), batch, seq, tt=512)
    ym_p, c_p, n_p, m_p = _mlstm_prompt(z, zg, bg, g_norm, batch, seq)
    yc_s, ym_s, conv_s, c_s, n_s, m_s = _mixer_sample(
        z, z[n_prompt:].reshape(n_dec, 1, MAIN_COLS), zg[n_prompt:].reshape(n_dec, 1, LANES),
        state_conv[0].reshape(n_dec, 2 * CONV_DIM), state_mlstm_C[0], state_mlstm_n[0],
        pad_lanes(state_mlstm_m[0]).reshape(n_dec, 1, LANES), wc, row(g_conv_out[0]), bg, g_norm,
        row0=n_prompt, n_dec=n_dec, sb=16)
    m_s = m_s[:, 0, :HEADS]
    y_conv = jnp.concatenate([yc_p, yc_s], axis=0)
    y_mlstm = jnp.concatenate([ym_p, ym_s.reshape(n_dec, HEADS * DV).astype(BF16)], axis=0)

    wo = w_out[0].astype(BF16)
    wr = pad_lanes(w_router[0])
    wr_hi = wr.astype(BF16)
    wr_lo = (wr - wr_hi.astype(F32)).astype(BF16)
    h1, xn, top_e, gate, rank, cnt = _outproj_router(
        y_conv, y_mlstm, x, wo[:CONV_DIM], wo[CONV_DIM:], row(g_ffn[0]), wr_hi, wr_lo,
        pad_lanes(row(b_router[0])), tm=320)

    n_assign = n_tok * TOP_K
    tmb = MOE_TMB
    nsb = n_assign // tmb + N_EXPERTS
    n_slots = -(-(n_assign + N_EXPERTS * (SUB - 1)) // SUB) * SUB + tmb
    counts = cnt[0, :N_EXPERTS].astype(jnp.int32)
    padded = (counts + SUB - 1) // SUB * SUB
    pend = jnp.cumsum(padded)
    pstart = jnp.concatenate([jnp.zeros((1,), jnp.int32), pend]).astype(jnp.int32)
    sb_per_e = (counts + tmb - 1) // tmb
    sb_cum = jnp.cumsum(sb_per_e)
    n_used = sb_cum[-1]
    sb = jnp.minimum(jnp.arange(nsb, dtype=jnp.int32), n_used - 1)
    sb_e = jnp.minimum(jnp.sum((sb_cum[None, :] <= sb[:, None]).astype(jnp.int32), axis=1), N_EXPERTS - 1)
    sb_j = sb - (sb_cum - sb_per_e)[sb_e]
    sb_start = (pstart[sb_e] + sb_j * tmb).astype(jnp.int32)
    sb_n = jnp.where(jnp.arange(nsb) < n_used, jnp.clip(counts[sb_e] - sb_j * tmb, 0, tmb), 0).astype(jnp.int32)
    dest = (pstart[top_e[:, :TOP_K]] + rank[:, :TOP_K]).reshape(-1).astype(jnp.int32)

    xs = _dispatch(dest, counts, pstart, xn, n_slots, tm=640)
    yb = _experts(sb_e.astype(jnp.int32), sb_start, sb_n, pstart[N_EXPERTS:], xs, w_gate_up[0],
                  b_gate_up[0].reshape(N_EXPERTS, 1, 2 * D_FF), w_down[0],
                  b_down[0].reshape(N_EXPERTS, 1, D_MODEL), nsb, tmb, MOE_TF)

    out = _combine(dest, h1, gate, p, yb, w_ple_gate[0].astype(BF16), w_ple_proj[0].astype(BF16),
                   row(g_ple[0]), row(g_final), tm=320)

    y_prompt = out[:n_prompt].reshape(batch, seq, D_MODEL)
    y_sample = out[n_prompt:].reshape(n_dec, 1, D_MODEL)
    return (y_prompt, y_sample,
            conv_p[None], c_p[None], n_p[None], m_p[None, :, :, 0],
            conv_s.reshape(1, n_dec, 2, CONV_DIM), c_s[None], n_s[None], m_s[None])
```

```python
import functools

import jax
import jax.numpy as jnp
from jax import lax
from jax.experimental import pallas as pl
from jax.experimental.pallas import tpu as pltpu

F32 = jnp.float32
BF16 = jnp.bfloat16

D_MODEL = 2048
CONV_DIM = 1024
HEADS = 4
DK = 128
DV = 256
CHUNK = 128
N_EXPERTS = 32
TOP_K = 4
D_FF = 2048
PLE_DIM = 256
SWIGLU_LIMIT = 7.0
SWIGLU_ALPHA = 1.702
EPS = 1e-6
LANES = 128
NEG_BIG = -1e30
MAIN_COLS = 3 * CONV_DIM + HEADS * (2 * DK + 2 * DV)

COL_B, COL_C, COL_H = 0, CONV_DIM, 2 * CONV_DIM
COL_Q = 3 * CONV_DIM
COL_K = COL_Q + HEADS * DK
COL_V = COL_K + HEADS * DK
COL_O = COL_V + HEADS * DV

VMEM_LIMIT = 56 * 1024 * 1024

SUB = 128
MOE_TMB = 9 * SUB
MOE_TF = 256


def _cparams(sem, vmem=VMEM_LIMIT):
    return pltpu.CompilerParams(dimension_semantics=sem, vmem_limit_bytes=vmem)


def _rms(x, g):
    return x * lax.rsqrt(jnp.mean(x * x, axis=-1, keepdims=True) + EPS) * g


def _split3(x):
    x1 = x.astype(BF16)
    r1 = x - x1.astype(F32)
    x2 = r1.astype(BF16)
    x3 = (r1 - x2.astype(F32)).astype(BF16)
    return x1, x2, x3


def _log_sigmoid(x):
    return jnp.minimum(x, 0.0) - jnp.log(1.0 + jnp.exp(-jnp.abs(x)))


def _inproj_kernel(x_ref, g_ref, w_ref, wg_ref, z_ref, zg_ref, a_scr):
    @pl.when(pl.program_id(1) == 0)
    def _():
        a = _rms(x_ref[...], g_ref[...]).astype(BF16)
        a_scr[...] = a
        zg_ref[...] = jnp.dot(a, wg_ref[...], preferred_element_type=F32)

    z_ref[...] = jnp.dot(a_scr[...], w_ref[...], preferred_element_type=F32)


def _inproj(x, g, w_main, w_gates, tm, tn):
    n_tok = x.shape[0]
    return pl.pallas_call(
        _inproj_kernel,
        grid=(n_tok // tm, MAIN_COLS // tn),
        in_specs=[
            pl.BlockSpec((tm, D_MODEL), lambda i, j: (i, 0)),
            pl.BlockSpec((1, D_MODEL), lambda i, j: (0, 0)),
            pl.BlockSpec((D_MODEL, tn), lambda i, j: (0, j)),
            pl.BlockSpec((D_MODEL, LANES), lambda i, j: (0, 0)),
        ],
        out_specs=[
            pl.BlockSpec((tm, tn), lambda i, j: (i, j)),
            pl.BlockSpec((tm, LANES), lambda i, j: (i, 0)),
        ],
        out_shape=[
            jax.ShapeDtypeStruct((n_tok, MAIN_COLS), F32),
            jax.ShapeDtypeStruct((n_tok, LANES), F32),
        ],
        scratch_shapes=[pltpu.VMEM((tm, D_MODEL), BF16)],
        compiler_params=_cparams(("parallel", "arbitrary")),
        name="inproj",
    )(x, g, w_main, w_gates)


def _conv_prompt_kernel(zb_ref, zc_ref, zh_ref, wc_ref, g_ref, y_ref, st_ref, carry):
    tt = zb_ref.shape[0]

    @pl.when(pl.program_id(1) == 0)
    def _():
        carry[...] = jnp.zeros_like(carry)

    u = zc_ref[...] * zh_ref[...]
    prev = carry[...]
    row = lax.broadcasted_iota(jnp.int32, (tt, 1), 0)
    u1 = jnp.where(row == 0, prev[7:8], pltpu.roll(u, 1, axis=0))
    u2 = jnp.where(row == 0, prev[6:7], jnp.where(row == 1, prev[7:8], pltpu.roll(u, 2, axis=0)))
    wc = wc_ref[...]
    conv = wc[0:1] * u2 + wc[1:2] * u1 + wc[2:3] * u
    y_ref[...] = _rms(zb_ref[...] * conv, g_ref[...]).astype(BF16)
    carry[...] = u[tt - 8:tt]
    st_ref[0] = u[tt - 2:tt]


def _conv_prompt(z, w_conv, g_conv, batch, seq, tt):
    nt = seq // tt
    n_tok = batch * seq
    cb = CONV_DIM
    return pl.pallas_call(
        _conv_prompt_kernel,
        grid=(batch, nt),
        in_specs=[
            pl.BlockSpec((tt, cb), lambda b, t: (b * nt + t, COL_B // cb)),
            pl.BlockSpec((tt, cb), lambda b, t: (b * nt + t, COL_C // cb)),
            pl.BlockSpec((tt, cb), lambda b, t: (b * nt + t, COL_H // cb)),
            pl.BlockSpec((8, cb), lambda b, t: (0, 0)),
            pl.BlockSpec((1, cb), lambda b, t: (0, 0)),
        ],
        out_specs=[
            pl.BlockSpec((tt, cb), lambda b, t: (b * nt + t, 0)),
            pl.BlockSpec((1, 2, cb), lambda b, t: (b, 0, 0)),
        ],
        out_shape=[
            jax.ShapeDtypeStruct((n_tok, cb), BF16),
            jax.ShapeDtypeStruct((batch, 2, cb), F32),
        ],
        scratch_shapes=[pltpu.VMEM((8, cb), F32)],
        compiler_params=_cparams(("parallel", "arbitrary")),
        name="conv_prompt",
    )(z, z, z, w_conv, g_conv)


def _tri_dot(tri_bf16, x, tri_first):
    out = None
    for part in _split3(x):
        d = (jnp.dot(tri_bf16, part, preferred_element_type=F32) if tri_first
             else jnp.dot(part, tri_bf16, preferred_element_type=F32))
        out = d if out is None else out + d
    return out


def _mlstm_prompt_kernel(q_ref, k_ref, v_ref, o_ref, zg_ref, bg_ref, gn_ref,
                         y_ref, c_out, n_out, m_out, c_scr, n_scr, m_scr):
    L = CHUNK

    @pl.when(pl.program_id(1) == 0)
    def _():
        c_scr[...] = jnp.zeros_like(c_scr)
        n_scr[...] = jnp.zeros_like(n_scr)
        m_scr[...] = jnp.zeros_like(m_scr)

    lane = lax.broadcasted_iota(jnp.int32, (L, LANES), 1)
    r_i = lax.broadcasted_iota(jnp.int32, (L, L), 0)
    c_i = lax.broadcasted_iota(jnp.int32, (L, L), 1)
    causal = c_i <= r_i
    tril = causal.astype(BF16)
    triu = (r_i <= c_i).astype(BF16)

    zg = zg_ref[...] + bg_ref[...]
    gates = jnp.where(lane < HEADS, zg, _log_sigmoid(zg))
    gates_t = gates.T
    bcol_all = _tri_dot(tril, gates, True)
    brow_all = _tri_dot(triu, gates_t, False)

    for h in range(HEADS):
        b_col = bcol_all[:, HEADS + h:HEADS + h + 1]
        b_row = brow_all[HEADS + h:HEADS + h + 1, :]
        i_col = gates[:, h:h + 1]
        i_row = gates_t[h:h + 1, :]
        b_end = b_col[L - 1:L, :]
        m0 = m_scr[h:h + 1, 0:1]
        n0 = n_scr[h:h + 1, :]
        c0 = c_scr[h]

        q = q_ref[:, h * DK:(h + 1) * DK]
        k = k_ref[:, h * DK:(h + 1) * DK] * (DK ** -0.5)
        v = v_ref[:, h * DV:(h + 1) * DV]
        qb, vb = q.astype(BF16), v.astype(BF16)

        dmat = jnp.where(causal, b_col - b_row + i_row, NEG_BIG)
        inter = b_col + m0
        m_t = jnp.maximum(inter, jnp.max(dmat, axis=1, keepdims=True))
        w_int = jnp.exp(inter - m_t)
        p = jnp.where(causal, jnp.exp(dmat - m_t), 0.0)
        s = lax.dot_general(qb, k.astype(BF16), (((1,), (1,)), ((), ())),
                            preferred_element_type=F32) * p
        num = (jnp.dot(s.astype(BF16), vb, preferred_element_type=F32)
               + w_int * jnp.dot(qb, c0.astype(BF16), preferred_element_type=F32))
        den = jnp.sum(s, axis=1, keepdims=True) + w_int * jnp.sum(q * n0, axis=1, keepdims=True)
        hh = num / jnp.maximum(jnp.abs(den), jnp.exp(-m_t))
        hn = hh * lax.rsqrt(jnp.mean(hh * hh, axis=-1, keepdims=True) + EPS)
        hn = hn * gn_ref[:, h * DV:(h + 1) * DV]
        y = jax.nn.sigmoid(o_ref[:, h * DV:(h + 1) * DV]) * hn
        y_ref[:, h * DV:(h + 1) * DV] = y.astype(BF16)

        g_col = b_end - b_col + i_col
        g_row = b_end - b_row + i_row
        m_new = jnp.maximum(b_end + m0, jnp.max(g_row, axis=1, keepdims=True))
        decay = jnp.exp(b_end + m0 - m_new)
        kw = k * jnp.exp(g_col - m_new)
        c_new = decay * c0 + lax.dot_general(kw.astype(BF16), vb, (((0,), (0,)), ((), ())),
                                             preferred_element_type=F32)
        n_new = decay * n0 + jnp.sum(kw, axis=0, keepdims=True)
        c_scr[h] = c_new
        n_scr[h:h + 1, :] = n_new
        m_scr[h:h + 1, :] = jnp.broadcast_to(m_new, (1, LANES))
        c_out[0, h] = c_new
        n_out[0, h:h + 1, :] = n_new
        m_out[0, h:h + 1, :] = jnp.broadcast_to(m_new, (1, LANES))


def _mlstm_prompt(z, zg, b_gates, g_norm, batch, seq):
    nc = seq // CHUNK
    n_tok = batch * seq
    qw, vw = HEADS * DK, HEADS * DV
    return pl.pallas_call(
        _mlstm_prompt_kernel,
        grid=(batch, nc),
        in_specs=[
            pl.BlockSpec((CHUNK, qw), lambda b, c: (b * nc + c, COL_Q // qw)),
            pl.BlockSpec((CHUNK, qw), lambda b, c: (b * nc + c, COL_K // qw)),
            pl.BlockSpec((CHUNK, vw), lambda b, c: (b * nc + c, COL_V // vw)),
            pl.BlockSpec((CHUNK, vw), lambda b, c: (b * nc + c, COL_O // vw)),
            pl.BlockSpec((CHUNK, LANES), lambda b, c: (b * nc + c, 0)),
            pl.BlockSpec((1, LANES), lambda b, c: (0, 0)),
            pl.BlockSpec((1, vw), lambda b, c: (0, 0)),
        ],
        out_specs=[
            pl.BlockSpec((CHUNK, vw), lambda b, c: (b * nc + c, 0)),
            pl.BlockSpec((1, HEADS, DK, DV), lambda b, c: (b, 0, 0, 0)),
            pl.BlockSpec((1, HEADS, DK), lambda b, c: (b, 0, 0)),
            pl.BlockSpec((1, HEADS, LANES), lambda b, c: (b, 0, 0)),
        ],
        out_shape=[
            jax.ShapeDtypeStruct((n_tok, vw), BF16),
            jax.ShapeDtypeStruct((batch, HEADS, DK, DV), F32),
            jax.ShapeDtypeStruct((batch, HEADS, DK), F32),
            jax.ShapeDtypeStruct((batch, HEADS, LANES), F32),
        ],
        scratch_shapes=[
            pltpu.VMEM((HEADS, DK, DV), F32),
            pltpu.VMEM((8, DK), F32),
            pltpu.VMEM((8, LANES), F32),
        ],
        compiler_params=_cparams(("parallel", "arbitrary")),
        name="mlstm_prompt",
    )(z, z, z, z, zg, b_gates, g_norm)


def _mixer_sample_kernel(zb_ref, zc_ref, zh_ref, q_ref, k_ref, v_ref, o_ref, zg_ref,
                         st_ref, c_ref, n_ref, m_ref, wc_ref, gc_ref, bg_ref, gn_ref,
                         yc_ref, ym_ref, st_out, c_out, n_out, m_out):
    sb = zb_ref.shape[0]
    u = zc_ref[...] * zh_ref[...]
    s0 = st_ref[:, 0:CONV_DIM]
    s1 = st_ref[:, CONV_DIM:2 * CONV_DIM]
    wc = wc_ref[...]
    conv = wc[0:1] * s0 + wc[1:2] * s1 + wc[2:3] * u
    yc_ref[...] = _rms(zb_ref[...] * conv, gc_ref[...]).astype(BF16)
    st_out[:, 0:CONV_DIM] = s1
    st_out[:, CONV_DIM:2 * CONV_DIM] = u

    lane_row = lax.broadcasted_iota(jnp.int32, (1, LANES), 1)
    eye = (lax.broadcasted_iota(jnp.int32, (DK, DK), 0)
           == lax.broadcasted_iota(jnp.int32, (DK, DK), 1)).astype(F32)

    def to_col(row):
        return jnp.sum(eye * row, axis=1, keepdims=True)

    def pick(row, j):
        return jnp.sum(jnp.where(lane_row == j, row, 0.0), axis=1, keepdims=True)

    def per_sample(s, carry):
        zg = zg_ref[s] + bg_ref[...]
        lf_all = _log_sigmoid(zg)
        m_row = m_ref[s]
        m_new_row = jnp.zeros((1, LANES), F32)
        for h in range(HEADS):
            q = q_ref[s, :, h * DK:(h + 1) * DK]
            k = k_ref[s, :, h * DK:(h + 1) * DK] * (DK ** -0.5)
            v = v_ref[s, :, h * DV:(h + 1) * DV]
            o = o_ref[s, :, h * DV:(h + 1) * DV]
            i_pre = pick(zg, h)
            lf = pick(lf_all, HEADS + h)
            m0 = pick(m_row, h)
            c0 = c_ref[s, h]
            n0 = n_ref[s, h:h + 1, :]
            inter = lf + m0
            m_t = jnp.maximum(inter, i_pre)
            w_int = jnp.exp(inter - m_t)
            p = jnp.exp(i_pre - m_t)
            sc = jnp.sum(q * k, axis=1, keepdims=True) * p
            qc = jnp.sum(to_col(q) * c0, axis=0, keepdims=True)
            num = sc * v + w_int * qc
            den = sc + w_int * jnp.sum(q * n0, axis=1, keepdims=True)
            hh = num / jnp.maximum(jnp.abs(den), jnp.exp(-m_t))
            hn = hh * lax.rsqrt(jnp.mean(hh * hh, axis=-1, keepdims=True) + EPS)
            hn = hn * gn_ref[:, h * DV:(h + 1) * DV]
            ym_ref[s, :, h * DV:(h + 1) * DV] = jax.nn.sigmoid(o) * hn
            c_out[s, h] = w_int * c0 + (p * to_col(k)) * v
            n_out[s, h:h + 1, :] = w_int * n0 + p * k
            m_new_row = jnp.where(lane_row == h, m_t, m_new_row)
        m_out[s] = m_new_row
        return carry

    lax.fori_loop(0, sb, per_sample, 0)


def _mixer_sample(z, zs3, zg3, st, c_s, n_s, m3, w_conv, g_conv, b_gates, g_norm, row0, n_dec, sb):
    cb, qw, vw = CONV_DIM, HEADS * DK, HEADS * DV
    r0 = row0 // sb
    row = lambda i: r0 + i
    return pl.pallas_call(
        _mixer_sample_kernel,
        grid=(n_dec // sb,),
        in_specs=[
            pl.BlockSpec((sb, cb), lambda i: (row(i), COL_B // cb)),
            pl.BlockSpec((sb, cb), lambda i: (row(i), COL_C // cb)),
            pl.BlockSpec((sb, cb), lambda i: (row(i), COL_H // cb)),
            pl.BlockSpec((sb, 1, qw), lambda i: (i, 0, COL_Q // qw)),
            pl.BlockSpec((sb, 1, qw), lambda i: (i, 0, COL_K // qw)),
            pl.BlockSpec((sb, 1, vw), lambda i: (i, 0, COL_V // vw)),
            pl.BlockSpec((sb, 1, vw), lambda i: (i, 0, COL_O // vw)),
            pl.BlockSpec((sb, 1, LANES), lambda i: (i, 0, 0)),
            pl.BlockSpec((sb, 2 * cb), lambda i: (i, 0)),
            pl.BlockSpec((sb, HEADS, DK, DV), lambda i: (i, 0, 0, 0)),
            pl.BlockSpec((sb, HEADS, DK), lambda i: (i, 0, 0)),
            pl.BlockSpec((sb, 1, LANES), lambda i: (i, 0, 0)),
            pl.BlockSpec((8, cb), lambda i: (0, 0)),
            pl.BlockSpec((1, cb), lambda i: (0, 0)),
            pl.BlockSpec((1, LANES), lambda i: (0, 0)),
            pl.BlockSpec((1, vw), lambda i: (0, 0)),
        ],
        out_specs=[
            pl.BlockSpec((sb, cb), lambda i: (i, 0)),
            pl.BlockSpec((sb, 1, vw), lambda i: (i, 0, 0)),
            pl.BlockSpec((sb, 2 * cb), lambda i: (i, 0)),
            pl.BlockSpec((sb, HEADS, DK, DV), lambda i: (i, 0, 0, 0)),
            pl.BlockSpec((sb, HEADS, DK), lambda i: (i, 0, 0)),
            pl.BlockSpec((sb, 1, LANES), lambda i: (i, 0, 0)),
        ],
        out_shape=[
            jax.ShapeDtypeStruct((n_dec, cb), BF16),
            jax.ShapeDtypeStruct((n_dec, 1, vw), F32),
            jax.ShapeDtypeStruct((n_dec, 2 * cb), F32),
            jax.ShapeDtypeStruct((n_dec, HEADS, DK, DV), F32),
            jax.ShapeDtypeStruct((n_dec, HEADS, DK), F32),
            jax.ShapeDtypeStruct((n_dec, 1, LANES), F32),
        ],
        compiler_params=_cparams(("arbitrary",)),
        name="mixer_sample",
    )(z, z, z, zs3, zs3, zs3, zs3, zg3, st, c_s, n_s, m3, w_conv, g_conv, b_gates, g_norm)


def _outproj_router_kernel(yc_ref, ym_ref, x_ref, wo1_ref, wo2_ref, g_ref, wr_hi_ref, wr_lo_ref, br_ref,
                           cin_ref, h1_ref, xn_ref, e_ref, gate_ref, rank_ref, cnt_ref, carry):
    tm = x_ref.shape[0]

    @pl.when(pl.program_id(0) == 0)
    def _():
        carry[...] = cin_ref[...]

    mix = (jnp.dot(yc_ref[...], wo1_ref[...], preferred_element_type=F32)
           + jnp.dot(ym_ref[...], wo2_ref[...], preferred_element_type=F32))
    h1 = x_ref[...] + mix
    h1_ref[...] = h1
    xn = _rms(h1, g_ref[...])
    xn_ref[...] = xn

    xh = xn.astype(BF16)
    xl = (xn - xh.astype(F32)).astype(BF16)
    logits = (jnp.dot(xh, wr_hi_ref[...], preferred_element_type=F32)
              + jnp.dot(xl, wr_hi_ref[...], preferred_element_type=F32)
              + jnp.dot(xh, wr_lo_ref[...], preferred_element_type=F32)) + br_ref[...]
    lane = lax.broadcasted_iota(jnp.int32, (tm, LANES), 1)
    lane_f = lane.astype(F32)
    work = jnp.where(lane < N_EXPERTS, logits, NEG_BIG)

    tops, idxs = [], []
    chosen = jnp.zeros((tm, LANES), F32)
    for _ in range(TOP_K):
        mx = jnp.max(work, axis=1, keepdims=True)
        idx = jnp.min(jnp.where(work == mx, lane_f, float(LANES)), axis=1, keepdims=True)
        sel = lane_f == idx
        tops.append(mx)
        idxs.append(idx)
        chosen = jnp.where(sel, 1.0, chosen)
        work = jnp.where(sel, NEG_BIG, work)

    exps = [jnp.exp(t - tops[0]) for t in tops]
    denom = exps[0] + exps[1] + exps[2] + exps[3]

    r_i = lax.broadcasted_iota(jnp.int32, (tm, tm), 0)
    c_i = lax.broadcasted_iota(jnp.int32, (tm, tm), 1)
    before = jnp.dot((c_i < r_i).astype(BF16), chosen.astype(BF16),
                     preferred_element_type=F32) + carry[...]

    e_out = jnp.zeros((tm, LANES), F32)
    g_out = jnp.zeros((tm, LANES), F32)
    r_out = jnp.zeros((tm, LANES), F32)
    for kk in range(TOP_K):
        rank = jnp.sum(jnp.where(lane_f == idxs[kk], before, 0.0), axis=1, keepdims=True)
        e_out = jnp.where(lane == kk, idxs[kk], e_out)
        g_out = jnp.where(lane == kk, exps[kk] / denom, g_out)
        r_out = jnp.where(lane == kk, rank, r_out)
    e_ref[...] = e_out.astype(jnp.int32)
    gate_ref[...] = g_out
    rank_ref[...] = r_out.astype(jnp.int32)

    new_carry = carry[...] + jnp.sum(chosen, axis=0, keepdims=True)
    carry[...] = new_carry
    cnt_ref[...] = new_carry


def _outproj_router(y_conv, y_mlstm, x, wo1, wo2, g_ffn, wr_hi, wr_lo, b_router, cnt_in, tm):
    n_tok = x.shape[0]
    full = lambda shape: pl.BlockSpec(shape, lambda i: (0,) * len(shape))
    tile = lambda w: pl.BlockSpec((tm, w), lambda i: (i, 0))
    return pl.pallas_call(
        _outproj_router_kernel,
        grid=(n_tok // tm,),
        in_specs=[
            tile(CONV_DIM), tile(HEADS * DV), tile(D_MODEL),
            full((CONV_DIM, D_MODEL)), full((HEADS * DV, D_MODEL)), full((1, D_MODEL)),
            full((D_MODEL, LANES)), full((D_MODEL, LANES)), full((1, LANES)), full((1, LANES)),
        ],
        out_specs=[tile(D_MODEL), tile(D_MODEL), tile(LANES), tile(LANES), tile(LANES), full((1, LANES))],
        out_shape=[
            jax.ShapeDtypeStruct((n_tok, D_MODEL), F32),
            jax.ShapeDtypeStruct((n_tok, D_MODEL), F32),
            jax.ShapeDtypeStruct((n_tok, LANES), jnp.int32),
            jax.ShapeDtypeStruct((n_tok, LANES), F32),
            jax.ShapeDtypeStruct((n_tok, LANES), jnp.int32),
            jax.ShapeDtypeStruct((1, LANES), F32),
        ],
        scratch_shapes=[pltpu.VMEM((1, LANES), F32)],
        compiler_params=_cparams(("arbitrary",)),
        name="outproj_router",
    )(y_conv, y_mlstm, x, wo1, wo2, g_ffn, wr_hi, wr_lo, b_router, cnt_in)


def _zero_fill_rows(dst, zero_scr, sem, start, count, act):
    low = count & 7
    for t in range(7):
        @pl.when(t < low)
        def _(t=t):
            act(pltpu.make_async_copy(zero_scr.at[pl.ds(0, 1)], dst.at[pl.ds(start + t, 1)], sem))

    off = start + low
    bit = 8
    while bit < SUB:
        take = count & bit

        @pl.when(take != 0)
        def _(off=off, bit=bit):
            row = pl.multiple_of(off, 8)
            act(pltpu.make_async_copy(zero_scr.at[pl.ds(0, bit)], dst.at[pl.ds(row, bit)], sem))

        off = off + take
        bit *= 2


def _zero_fill_chunks(dst, zero_scr, sem, start, act):
    def body(j, c):
        row = pl.multiple_of(start + j * SUB, SUB)
        act(pltpu.make_async_copy(zero_scr, dst.at[pl.ds(row, SUB)], sem))
        return c

    lax.fori_loop(0, (dst.shape[0] - start) // SUB, body, 0)


def _dispatch_kernel(dest_ref, cnt_ref, pstart_ref, xn_ref, xn_s_ref, xs_out, zero_scr, sem, zsem):
    tm = xn_ref.shape[0]
    step = pl.program_id(0)
    last = pl.num_programs(0) - 1

    def fill(act):
        for e in range(N_EXPERTS):
            c = cnt_ref[e]
            _zero_fill_rows(xs_out, zero_scr, zsem, pstart_ref[e] + c, (SUB - c % SUB) % SUB, act)
        _zero_fill_chunks(xs_out, zero_scr, zsem, pstart_ref[N_EXPERTS], act)

    @pl.when(step == 0)
    def _():
        zero_scr[...] = jnp.zeros_like(zero_scr)
        fill(lambda cp: cp.start())

    def scatter_rows(src, base):
        rows = src.shape[0]

        def issue(r, c):
            for kk in range(TOP_K):
                d = dest_ref[(base + r) * TOP_K + kk]
                pltpu.make_async_copy(src.at[pl.ds(r, 1)], xs_out.at[pl.ds(d, 1)], sem).start()
            return c

        lax.fori_loop(0, rows, issue, 0)
        for _ in range(TOP_K):
            pltpu.make_async_copy(src, xs_out.at[pl.ds(0, rows)], sem).wait()

    scatter_rows(xn_ref, step * tm)

    @pl.when(step == last)
    def _():
        scatter_rows(xn_s_ref, pl.num_programs(0) * tm)

    @pl.when(step == 0)
    def _():
        fill(lambda cp: cp.wait())


def _dispatch(dest_flat, counts, pstart, xn_p, xn_s, n_slots, tm):
    n_prompt = xn_p.shape[0]
    return pl.pallas_call(
        _dispatch_kernel,
        grid_spec=pltpu.PrefetchScalarGridSpec(
            num_scalar_prefetch=3,
            grid=(n_prompt // tm,),
            in_specs=[pl.BlockSpec((tm, D_MODEL), lambda i, d, c, p: (i, 0)),
                      pl.BlockSpec(xn_s.shape, lambda i, d, c, p: (0, 0))],
            out_specs=pl.BlockSpec(memory_space=pl.ANY),
            scratch_shapes=[pltpu.VMEM((SUB, D_MODEL), F32), pltpu.SemaphoreType.DMA(()),
                            pltpu.SemaphoreType.DMA(())],
        ),
        out_shape=jax.ShapeDtypeStruct((n_slots, D_MODEL), F32),
        compiler_params=_cparams(("arbitrary",)),
        name="dispatch",
    )(dest_flat, counts, pstart, xn_p, xn_s)


def _expert_kernel(se_ref, st_ref, sn_ref, end_ref, x_ref, wg_ref, wu_ref, bg_ref, bu_ref, wd_ref, bd_ref,
                   y_ref, xb_scr, acc, zero_scr, sem, zsem):
    i, f = pl.program_id(0), pl.program_id(1)
    nf = pl.num_programs(1)
    tmb = acc.shape[0]
    n = sn_ref[i]
    start = st_ref[i]

    @pl.when((i == 0) & (f == 0))
    def _():
        zero_scr[...] = jnp.zeros_like(zero_scr)
        _zero_fill_chunks(y_ref, zero_scr, zsem, end_ref[0], lambda cp: cp.start())
        _zero_fill_chunks(y_ref, zero_scr, zsem, end_ref[0], lambda cp: cp.wait())

    def y_copy(o, m):
        row = pl.multiple_of(start + o, SUB)
        return pltpu.make_async_copy(acc.at[pl.ds(o, m)], y_ref.at[pl.ds(row, m)], sem)

    def for_each_sub_block(fn):
        kmax = tmb // SUB
        nb = (n + SUB - 1) // SUB
        for k in (kmax, kmax - 1):
            pl.when(nb == k)(functools.partial(fn, 0, k * SUB))
        bits = [b for b in (64, 32, 16, 8, 4, 2, 1) if b <= kmax - 2]
        small = nb <= kmax - 2
        for b in bits:
            higher = [h for h in bits if h > b]
            for mask in range(1 << len(higher)):
                above = sum(h for j, h in enumerate(higher) if (mask >> j) & 1)
                if above + b > kmax - 2:
                    continue
                cond = small & ((nb & b) != 0) & ((nb & sum(higher)) == above)
                pl.when(cond)(functools.partial(fn, above * SUB, b * SUB))

    @pl.when(n > 0)
    def _():
        @pl.when(f == 0)
        def _():
            xb_scr[...] = x_ref[...].astype(BF16)
            acc[...] = jnp.broadcast_to(bd_ref[0], acc.shape)

        wg = wg_ref[0].astype(BF16)
        wu = wu_ref[0].astype(BF16)
        wd = wd_ref[0].astype(BF16)
        bg = bg_ref[0]
        bu = bu_ref[0]

        def sub_block(o, m):
            xb = xb_scr[o:o + m]
            g = jnp.dot(xb, wg, preferred_element_type=F32) + bg
            u = jnp.dot(xb, wu, preferred_element_type=F32) + bu
            g = jnp.minimum(g, SWIGLU_LIMIT)
            u = jnp.clip(u, -SWIGLU_LIMIT, SWIGLU_LIMIT)
            act = g * jax.nn.sigmoid(SWIGLU_ALPHA * g) * (u + 1.0)
            acc[o:o + m] += jnp.dot(act.astype(BF16), wd, preferred_element_type=F32)

            @pl.when(f == nf - 1)
            def _():
                y_copy(o, m).start()

        for_each_sub_block(sub_block)

        @pl.when(f == nf - 1)
        def _():
            for_each_sub_block(lambda o, m: y_copy(o, m).wait())


def _experts(sb_e, sb_start, sb_n, end, xs, w_gate_up, b_gate_up, w_down, b_down, nsb, tmb, tf):
    n_slots = xs.shape[0]
    nf = D_FF // tf

    def fidx(i, f, sn):
        return jnp.where(sn[i] > 0, f, nf - 1)

    return pl.pallas_call(
        _expert_kernel,
        grid_spec=pltpu.PrefetchScalarGridSpec(
            num_scalar_prefetch=4,
            grid=(nsb, nf),
            in_specs=[
                pl.BlockSpec((pl.Element(tmb), pl.Element(D_MODEL)),
                             lambda i, f, se, st, sn, en: (pl.multiple_of(st[i], SUB), 0)),
                pl.BlockSpec((1, D_MODEL, tf), lambda i, f, se, st, sn, en: (se[i], 0, fidx(i, f, sn))),
                pl.BlockSpec((1, D_MODEL, tf), lambda i, f, se, st, sn, en: (se[i], 0, nf + fidx(i, f, sn))),
                pl.BlockSpec((1, 1, tf), lambda i, f, se, st, sn, en: (se[i], 0, fidx(i, f, sn))),
                pl.BlockSpec((1, 1, tf), lambda i, f, se, st, sn, en: (se[i], 0, nf + fidx(i, f, sn))),
                pl.BlockSpec((1, tf, D_MODEL), lambda i, f, se, st, sn, en: (se[i], fidx(i, f, sn), 0)),
                pl.BlockSpec((1, 1, D_MODEL), lambda i, f, se, st, sn, en: (se[i], 0, 0)),
            ],
            out_specs=pl.BlockSpec(memory_space=pl.ANY),
            scratch_shapes=[pltpu.VMEM((tmb, D_MODEL), BF16), pltpu.VMEM((tmb, D_MODEL), F32),
                            pltpu.VMEM((SUB, D_MODEL), F32),
                            pltpu.SemaphoreType.DMA(()), pltpu.SemaphoreType.DMA(())],
        ),
        out_shape=jax.ShapeDtypeStruct((n_slots, D_MODEL), F32),
        compiler_params=_cparams(("arbitrary", "arbitrary")),
        name="experts",
    )(sb_e, sb_start, sb_n, end, xs, w_gate_up, w_gate_up, b_gate_up, b_gate_up, w_down, b_down)


def _combine_kernel(dest_ref, h1_ref, gate_ref, p_ref, yb_ref, wpg_ref, wpp_ref, gp_ref, gf_ref,
                    out_ref, gbuf, sem):
    tm = h1_ref.shape[0]
    step = pl.program_id(0)
    slot = step % 2

    def gather_tile(t, s):
        def issue(r, c):
            for kk in range(TOP_K):
                d = dest_ref[(t * tm + r) * TOP_K + kk]
                pltpu.make_async_copy(yb_ref.at[pl.ds(d, 1)], gbuf.at[s, kk, pl.ds(r, 1)], sem.at[s]).start()
            return c

        lax.fori_loop(0, tm, issue, 0)

    @pl.when(step == 0)
    def _():
        gather_tile(0, 0)

    @pl.when(step + 1 < pl.num_programs(0))
    def _():
        gather_tile(step + 1, 1 - slot)

    for kk in range(TOP_K):
        pltpu.make_async_copy(yb_ref.at[pl.ds(0, tm)], gbuf.at[slot, kk], sem.at[slot]).wait()

    gate = gate_ref[...]
    moe = gate[:, 0:1] * gbuf[slot, 0]
    for kk in range(1, TOP_K):
        moe = moe + gate[:, kk:kk + 1] * gbuf[slot, kk]
    h2 = h1_ref[...] + moe
    a = _rms(h2, gp_ref[...]).astype(BF16)
    pg = jax.nn.sigmoid(jnp.dot(a, wpg_ref[...], preferred_element_type=F32))
    pp = jnp.dot(p_ref[...].astype(BF16), wpp_ref[...], preferred_element_type=F32)
    h3 = h2 + pg * pp
    out_ref[...] = _rms(h3, gf_ref[...])


def _combine(dest_flat, h1, gate, p, yb, wpg, wpp, g_ple, g_final, tm):
    n_tok = h1.shape[0]
    full = lambda shape: pl.BlockSpec(shape, lambda i, d: (0,) * len(shape))
    tile = lambda w: pl.BlockSpec((tm, w), lambda i, d: (i, 0))
    return pl.pallas_call(
        _combine_kernel,
        grid_spec=pltpu.PrefetchScalarGridSpec(
            num_scalar_prefetch=1,
            grid=(n_tok // tm,),
            in_specs=[
                tile(D_MODEL), tile(LANES), tile(PLE_DIM),
                pl.BlockSpec(memory_space=pl.ANY),
                full((D_MODEL, D_MODEL)), full((PLE_DIM, D_MODEL)), full((1, D_MODEL)), full((1, D_MODEL)),
            ],
            out_specs=tile(D_MODEL),
            scratch_shapes=[pltpu.VMEM((2, TOP_K, tm, D_MODEL), F32), pltpu.SemaphoreType.DMA((2,))],
        ),
        out_shape=jax.ShapeDtypeStruct((n_tok, D_MODEL), F32),
        compiler_params=_cparams(("arbitrary",)),
        name="combine",
    )(dest_flat, h1, gate, p, yb, wpg, wpp, g_ple, g_final)


def kernel(x_prompt, x_sample, state_conv, state_mlstm_C, state_mlstm_n, state_mlstm_m, p_prompt, p_sample,
           g_mix, w_in, b_gates, w_conv, g_conv_out, g_mlstm_out, w_out, g_ffn, w_router, b_router,
           w_gate_up, b_gate_up, w_down, b_down, g_ple, w_ple_gate, w_ple_proj, g_final):
    batch, seq, _ = x_prompt.shape
    n_dec = x_sample.shape[0]
    assert w_in.shape[0] == 1 and x_sample.shape[1] == 1
    n_prompt = batch * seq
    n_tok = n_prompt + n_dec

    x_p = x_prompt.reshape(n_prompt, D_MODEL)
    x_s = x_sample.reshape(n_dec, D_MODEL)

    row = lambda a: a.reshape(1, -1)
    pad_lanes = lambda a: jnp.pad(a, ((0, 0), (0, LANES - a.shape[1])))
    w_main = w_in[0, :, :MAIN_COLS].astype(BF16)
    w_gates = pad_lanes(w_in[0, :, MAIN_COLS:]).astype(BF16)
    bg = pad_lanes(row(b_gates[0]))
    wc = jnp.pad(w_conv[0], ((0, 8 - w_conv.shape[1]), (0, 0)))
    g_norm = row(g_mlstm_out[0])

    z_p, zg_p = _inproj(x_p, row(g_mix[0]), w_main, w_gates, tm=1024, tn=1024)
    z_s, zg_s = _inproj(x_s, row(g_mix[0]), w_main, w_gates, tm=n_dec, tn=1024)

    yc_p, conv_p = _conv_prompt(z_p, wc, row(g_conv_out[0]), batch, seq, tt=512)
    ym_p, c_p, n_p, m_p = _mlstm_prompt(z_p, zg_p, bg, g_norm, batch, seq)
    yc_s, ym_s, conv_s, c_s, n_s, m_s = _mixer_sample(
        z_s, z_s.reshape(n_dec, 1, MAIN_COLS), zg_s.reshape(n_dec, 1, LANES),
        state_conv[0].reshape(n_dec, 2 * CONV_DIM), state_mlstm_C[0], state_mlstm_n[0],
        pad_lanes(state_mlstm_m[0]).reshape(n_dec, 1, LANES), wc, row(g_conv_out[0]), bg, g_norm,
        row0=0, n_dec=n_dec, sb=16)
    m_s = m_s[:, 0, :HEADS]
    ym_s = ym_s.reshape(n_dec, HEADS * DV).astype(BF16)

    wo = w_out[0].astype(BF16)
    wr = pad_lanes(w_router[0])
    wr_hi = wr.astype(BF16)
    wr_lo = (wr - wr_hi.astype(F32)).astype(BF16)
    router_args = (wo[:CONV_DIM], wo[CONV_DIM:], row(g_ffn[0]), wr_hi, wr_lo, pad_lanes(row(b_router[0])))
    h1_p, xn_p, e_p, gate_p, rank_p, cnt_p = _outproj_router(
        yc_p, ym_p, x_p, *router_args, jnp.zeros((1, LANES), F32), tm=512)
    h1_s, xn_s, e_s, gate_s, rank_s, cnt = _outproj_router(
        yc_s, ym_s, x_s, *router_args, cnt_p, tm=n_dec)

    n_assign = n_tok * TOP_K
    tmb = MOE_TMB
    nsb = n_assign // tmb + N_EXPERTS
    n_slots = -(-(n_assign + N_EXPERTS * (SUB - 1)) // SUB) * SUB + tmb
    counts = cnt[0, :N_EXPERTS].astype(jnp.int32)
    padded = (counts + SUB - 1) // SUB * SUB
    pend = jnp.cumsum(padded)
    pstart = jnp.concatenate([jnp.zeros((1,), jnp.int32), pend]).astype(jnp.int32)
    sb_per_e = (counts + tmb - 1) // tmb
    sb_cum = jnp.cumsum(sb_per_e)
    n_used = sb_cum[-1]
    sb = jnp.minimum(jnp.arange(nsb, dtype=jnp.int32), n_used - 1)
    sb_e = jnp.minimum(jnp.sum((sb_cum[None, :] <= sb[:, None]).astype(jnp.int32), axis=1), N_EXPERTS - 1)
    sb_j = sb - (sb_cum - sb_per_e)[sb_e]
    sb_start = (pstart[sb_e] + sb_j * tmb).astype(jnp.int32)
    sb_n = jnp.where(jnp.arange(nsb) < n_used, jnp.clip(counts[sb_e] - sb_j * tmb, 0, tmb), 0).astype(jnp.int32)
    slot_of = lambda e, r: (pstart[e[:, :TOP_K]] + r[:, :TOP_K]).reshape(-1).astype(jnp.int32)
    dest_p = slot_of(e_p, rank_p)
    dest_s = slot_of(e_s, rank_s)

    xs = _dispatch(jnp.concatenate([dest_p, dest_s]), counts, pstart, xn_p, xn_s, n_slots, tm=512)
    yb = _experts(sb_e.astype(jnp.int32), sb_start, sb_n, pstart[N_EXPERTS:], xs, w_gate_up[0],
                  b_gate_up[0].reshape(N_EXPERTS, 1, 2 * D_FF), w_down[0],
                  b_down[0].reshape(N_EXPERTS, 1, D_MODEL), nsb, tmb, MOE_TF)

    ple_args = (w_ple_gate[0].astype(BF16), w_ple_proj[0].astype(BF16), row(g_ple[0]), row(g_final))
    out_p = _combine(dest_p, h1_p, gate_p, p_prompt[0].reshape(n_prompt, PLE_DIM), yb, *ple_args, tm=256)
    out_s = _combine(dest_s, h1_s, gate_s, p_sample[0].reshape(n_dec, PLE_DIM), yb, *ple_args, tm=n_dec)

    y_prompt = out_p.reshape(batch, seq, D_MODEL)
    y_sample = out_s.reshape(n_dec, 1, D_MODEL)
    return (y_prompt, y_sample,
            conv_p[None], c_p[None], n_p[None], m_p[None, :, :, 0],
            conv_s.reshape(1, n_dec, 2, CONV_DIM), c_s[None], n_s[None], m_s[None])
```

```python
import functools

import jax
import jax.numpy as jnp
from jax import lax
from jax.experimental import pallas as pl
from jax.experimental.pallas import tpu as pltpu

F32 = jnp.float32
BF16 = jnp.bfloat16

D_MODEL = 2048
CONV_DIM = 1024
HEADS = 4
DK = 128
DV = 256
CHUNK = 128
N_EXPERTS = 32
TOP_K = 4
D_FF = 2048
PLE_DIM = 256
SWIGLU_LIMIT = 7.0
SWIGLU_ALPHA = 1.702
EPS = 1e-6
LANES = 128
NEG_BIG = -1e30
MAIN_COLS = 3 * CONV_DIM + HEADS * (2 * DK + 2 * DV)

COL_B, COL_C, COL_H = 0, CONV_DIM, 2 * CONV_DIM
COL_Q = 3 * CONV_DIM
COL_K = COL_Q + HEADS * DK
COL_V = COL_K + HEADS * DK
COL_O = COL_V + HEADS * DV

VMEM_LIMIT = 56 * 1024 * 1024

SUB = 128
MOE_TMB = 9 * SUB
MOE_TF = 256
MOE_SPLIT = 4
ISSUE_UNROLL = 8


def _cparams(sem, vmem=VMEM_LIMIT):
    return pltpu.CompilerParams(dimension_semantics=sem, vmem_limit_bytes=vmem)


def _rms(x, g):
    return x * lax.rsqrt(jnp.mean(x * x, axis=-1, keepdims=True) + EPS) * g


def _split3(x):
    x1 = x.astype(BF16)
    r1 = x - x1.astype(F32)
    x2 = r1.astype(BF16)
    x3 = (r1 - x2.astype(F32)).astype(BF16)
    return x1, x2, x3


def _log_sigmoid(x):
    return jnp.minimum(x, 0.0) - jnp.log(1.0 + jnp.exp(-jnp.abs(x)))


def _norm_gates_kernel(x_ref, g_ref, wg_ref, a_ref, zg_ref):
    a = _rms(x_ref[...], g_ref[...]).astype(BF16)
    a_ref[...] = a
    zg_ref[...] = jnp.dot(a, wg_ref[...], preferred_element_type=F32)


def _norm_gates(x, g, w_gates, tm):
    n_tok = x.shape[0]
    return pl.pallas_call(
        _norm_gates_kernel,
        grid=(n_tok // tm,),
        in_specs=[
            pl.BlockSpec((tm, D_MODEL), lambda i: (i, 0)),
            pl.BlockSpec((1, D_MODEL), lambda i: (0, 0)),
            pl.BlockSpec((D_MODEL, LANES), lambda i: (0, 0)),
        ],
        out_specs=[
            pl.BlockSpec((tm, D_MODEL), lambda i: (i, 0)),
            pl.BlockSpec((tm, LANES), lambda i: (i, 0)),
        ],
        out_shape=[
            jax.ShapeDtypeStruct((n_tok, D_MODEL), BF16),
            jax.ShapeDtypeStruct((n_tok, LANES), F32),
        ],
        compiler_params=_cparams(("parallel",)),
        name="norm_gates",
    )(x, g, w_gates)


def _inproj_kernel(a_ref, w_ref, z_ref, w_scr):
    @pl.when(pl.program_id(1) == 0)
    def _():
        w_scr[...] = w_ref[0].astype(BF16)

    z_ref[...] = jnp.dot(a_ref[...], w_scr[...], preferred_element_type=F32)


def _inproj(a, w_in, tm, tn):
    n_tok = a.shape[0]
    return pl.pallas_call(
        _inproj_kernel,
        grid=(MAIN_COLS // tn, n_tok // tm),
        in_specs=[
            pl.BlockSpec((tm, D_MODEL), lambda j, i: (i, 0)),
            pl.BlockSpec((1, D_MODEL, tn), lambda j, i: (0, 0, j)),
        ],
        out_specs=pl.BlockSpec((tm, tn), lambda j, i: (i, j)),
        out_shape=jax.ShapeDtypeStruct((n_tok, MAIN_COLS), F32),
        scratch_shapes=[pltpu.VMEM((D_MODEL, tn), BF16)],
        compiler_params=_cparams(("arbitrary", "arbitrary")),
        name="inproj",
    )(a, w_in)


def _conv_prompt_kernel(zb_ref, zc_ref, zh_ref, wc_ref, g_ref, y_ref, st_ref, carry):
    tt = zb_ref.shape[0]

    @pl.when(pl.program_id(1) == 0)
    def _():
        carry[...] = jnp.zeros_like(carry)

    u = zc_ref[...] * zh_ref[...]
    prev = carry[...]
    row = lax.broadcasted_iota(jnp.int32, (tt, 1), 0)
    u1 = jnp.where(row == 0, prev[7:8], pltpu.roll(u, 1, axis=0))
    u2 = jnp.where(row == 0, prev[6:7], jnp.where(row == 1, prev[7:8], pltpu.roll(u, 2, axis=0)))
    wc = wc_ref[...]
    conv = wc[0:1] * u2 + wc[1:2] * u1 + wc[2:3] * u
    y_ref[...] = _rms(zb_ref[...] * conv, g_ref[...]).astype(BF16)
    carry[...] = u[tt - 8:tt]
    st_ref[0] = u[tt - 2:tt]


def _conv_prompt(z, w_conv, g_conv, batch, seq, tt):
    nt = seq // tt
    n_tok = batch * seq
    cb = CONV_DIM
    return pl.pallas_call(
        _conv_prompt_kernel,
        grid=(batch, nt),
        in_specs=[
            pl.BlockSpec((tt, cb), lambda b, t: (b * nt + t, COL_B // cb)),
            pl.BlockSpec((tt, cb), lambda b, t: (b * nt + t, COL_C // cb)),
            pl.BlockSpec((tt, cb), lambda b, t: (b * nt + t, COL_H // cb)),
            pl.BlockSpec((8, cb), lambda b, t: (0, 0)),
            pl.BlockSpec((1, cb), lambda b, t: (0, 0)),
        ],
        out_specs=[
            pl.BlockSpec((tt, cb), lambda b, t: (b * nt + t, 0)),
            pl.BlockSpec((1, 2, cb), lambda b, t: (b, 0, 0)),
        ],
        out_shape=[
            jax.ShapeDtypeStruct((n_tok, cb), BF16),
            jax.ShapeDtypeStruct((batch, 2, cb), F32),
        ],
        scratch_shapes=[pltpu.VMEM((8, cb), F32)],
        compiler_params=_cparams(("parallel", "arbitrary")),
        name="conv_prompt",
    )(z, z, z, w_conv, g_conv)


def _tri_dot(tri_bf16, x, tri_first):
    out = None
    for part in _split3(x):
        d = (jnp.dot(tri_bf16, part, preferred_element_type=F32) if tri_first
             else jnp.dot(part, tri_bf16, preferred_element_type=F32))
        out = d if out is None else out + d
    return out


def _mlstm_prompt_kernel(q_ref, k_ref, v_ref, o_ref, zg_ref, bg_ref, gn_ref,
                         y_ref, c_out, n_out, m_out, c_scr, n_scr, m_scr):
    L = CHUNK

    @pl.when(pl.program_id(1) == 0)
    def _():
        c_scr[...] = jnp.zeros_like(c_scr)
        n_scr[...] = jnp.zeros_like(n_scr)
        m_scr[...] = jnp.zeros_like(m_scr)

    lane = lax.broadcasted_iota(jnp.int32, (L, LANES), 1)
    r_i = lax.broadcasted_iota(jnp.int32, (L, L), 0)
    c_i = lax.broadcasted_iota(jnp.int32, (L, L), 1)
    causal = c_i <= r_i
    tril = causal.astype(BF16)
    triu = (r_i <= c_i).astype(BF16)

    zg = zg_ref[...] + bg_ref[...]
    gates = jnp.where(lane < HEADS, zg, _log_sigmoid(zg))
    gates_t = gates.T
    bcol_all = _tri_dot(tril, gates, True)
    brow_all = _tri_dot(triu, gates_t, False)

    for h in range(HEADS):
        b_col = bcol_all[:, HEADS + h:HEADS + h + 1]
        b_row = brow_all[HEADS + h:HEADS + h + 1, :]
        i_col = gates[:, h:h + 1]
        i_row = gates_t[h:h + 1, :]
        b_end = b_col[L - 1:L, :]
        m0 = m_scr[h:h + 1, 0:1]
        n0 = n_scr[h:h + 1, :]
        c0 = c_scr[h]

        q = q_ref[:, h * DK:(h + 1) * DK]
        k = k_ref[:, h * DK:(h + 1) * DK] * (DK ** -0.5)
        v = v_ref[:, h * DV:(h + 1) * DV]
        qb, vb = q.astype(BF16), v.astype(BF16)

        dmat = jnp.where(causal, b_col - b_row + i_row, NEG_BIG)
        inter = b_col + m0
        m_t = jnp.maximum(inter, jnp.max(dmat, axis=1, keepdims=True))
        w_int = jnp.exp(inter - m_t)
        p = jnp.where(causal, jnp.exp(dmat - m_t), 0.0)
        s = lax.dot_general(qb, k.astype(BF16), (((1,), (1,)), ((), ())),
                            preferred_element_type=F32) * p
        num = (jnp.dot(s.astype(BF16), vb, preferred_element_type=F32)
               + w_int * jnp.dot(qb, c0.astype(BF16), preferred_element_type=F32))
        den = jnp.sum(s, axis=1, keepdims=True) + w_int * jnp.sum(q * n0, axis=1, keepdims=True)
        hh = num / jnp.maximum(jnp.abs(den), jnp.exp(-m_t))
        hn = hh * lax.rsqrt(jnp.mean(hh * hh, axis=-1, keepdims=True) + EPS)
        hn = hn * gn_ref[:, h * DV:(h + 1) * DV]
        y = jax.nn.sigmoid(o_ref[:, h * DV:(h + 1) * DV]) * hn
        y_ref[:, h * DV:(h + 1) * DV] = y.astype(BF16)

        g_col = b_end - b_col + i_col
        g_row = b_end - b_row + i_row
        m_new = jnp.maximum(b_end + m0, jnp.max(g_row, axis=1, keepdims=True))
        decay = jnp.exp(b_end + m0 - m_new)
        kw = k * jnp.exp(g_col - m_new)
        c_new = decay * c0 + lax.dot_general(kw.astype(BF16), vb, (((0,), (0,)), ((), ())),
                                             preferred_element_type=F32)
        n_new = decay * n0 + jnp.sum(kw, axis=0, keepdims=True)
        c_scr[h] = c_new
        n_scr[h:h + 1, :] = n_new
        m_scr[h:h + 1, :] = jnp.broadcast_to(m_new, (1, LANES))
        c_out[0, h] = c_new
        n_out[0, h:h + 1, :] = n_new
        m_out[0, h:h + 1, :] = jnp.broadcast_to(m_new, (1, LANES))


def _mlstm_prompt(z, zg, b_gates, g_norm, batch, seq):
    nc = seq // CHUNK
    n_tok = batch * seq
    qw, vw = HEADS * DK, HEADS * DV
    return pl.pallas_call(
        _mlstm_prompt_kernel,
        grid=(batch, nc),
        in_specs=[
            pl.BlockSpec((CHUNK, qw), lambda b, c: (b * nc + c, COL_Q // qw)),
            pl.BlockSpec((CHUNK, qw), lambda b, c: (b * nc + c, COL_K // qw)),
            pl.BlockSpec((CHUNK, vw), lambda b, c: (b * nc + c, COL_V // vw)),
            pl.BlockSpec((CHUNK, vw), lambda b, c: (b * nc + c, COL_O // vw)),
            pl.BlockSpec((CHUNK, LANES), lambda b, c: (b * nc + c, 0)),
            pl.BlockSpec((1, LANES), lambda b, c: (0, 0)),
            pl.BlockSpec((1, vw), lambda b, c: (0, 0)),
        ],
        out_specs=[
            pl.BlockSpec((CHUNK, vw), lambda b, c: (b * nc + c, 0)),
            pl.BlockSpec((1, HEADS, DK, DV), lambda b, c: (b, 0, 0, 0)),
            pl.BlockSpec((1, HEADS, DK), lambda b, c: (b, 0, 0)),
            pl.BlockSpec((1, HEADS, LANES), lambda b, c: (b, 0, 0)),
        ],
        out_shape=[
            jax.ShapeDtypeStruct((n_tok, vw), BF16),
            jax.ShapeDtypeStruct((batch, HEADS, DK, DV), F32),
            jax.ShapeDtypeStruct((batch, HEADS, DK), F32),
            jax.ShapeDtypeStruct((batch, HEADS, LANES), F32),
        ],
        scratch_shapes=[
            pltpu.VMEM((HEADS, DK, DV), F32),
            pltpu.VMEM((8, DK), F32),
            pltpu.VMEM((8, LANES), F32),
        ],
        compiler_params=_cparams(("parallel", "arbitrary")),
        name="mlstm_prompt",
    )(z, z, z, z, zg, b_gates, g_norm)


def _mixer_sample_kernel(zb_ref, zc_ref, zh_ref, q_ref, k_ref, v_ref, o_ref, zg_ref,
                         st_ref, c_ref, n_ref, m_ref, wc_ref, gc_ref, bg_ref, gn_ref,
                         yc_ref, ym_ref, st_out, c_out, n_out, m_out):
    sb = zb_ref.shape[0]
    u = zc_ref[...] * zh_ref[...]
    s0 = st_ref[:, 0:CONV_DIM]
    s1 = st_ref[:, CONV_DIM:2 * CONV_DIM]
    wc = wc_ref[...]
    conv = wc[0:1] * s0 + wc[1:2] * s1 + wc[2:3] * u
    yc_ref[...] = _rms(zb_ref[...] * conv, gc_ref[...]).astype(BF16)
    st_out[:, 0:CONV_DIM] = s1
    st_out[:, CONV_DIM:2 * CONV_DIM] = u

    lane_row = lax.broadcasted_iota(jnp.int32, (1, LANES), 1)
    eye = (lax.broadcasted_iota(jnp.int32, (DK, DK), 0)
           == lax.broadcasted_iota(jnp.int32, (DK, DK), 1)).astype(F32)

    def to_col(row):
        return jnp.sum(eye * row, axis=1, keepdims=True)

    def pick(row, j):
        return jnp.sum(jnp.where(lane_row == j, row, 0.0), axis=1, keepdims=True)

    def per_sample(s, carry):
        zg = zg_ref[s] + bg_ref[...]
        lf_all = _log_sigmoid(zg)
        m_row = m_ref[s]
        m_new_row = jnp.zeros((1, LANES), F32)
        for h in range(HEADS):
            q = q_ref[s, :, h * DK:(h + 1) * DK]
            k = k_ref[s, :, h * DK:(h + 1) * DK] * (DK ** -0.5)
            v = v_ref[s, :, h * DV:(h + 1) * DV]
            o = o_ref[s, :, h * DV:(h + 1) * DV]
            i_pre = pick(zg, h)
            lf = pick(lf_all, HEADS + h)
            m0 = pick(m_row, h)
            c0 = c_ref[s, h]
            n0 = n_ref[s, h:h + 1, :]
            inter = lf + m0
            m_t = jnp.maximum(inter, i_pre)
            w_int = jnp.exp(inter - m_t)
            p = jnp.exp(i_pre - m_t)
            sc = jnp.sum(q * k, axis=1, keepdims=True) * p
            qc = jnp.sum(to_col(q) * c0, axis=0, keepdims=True)
            num = sc * v + w_int * qc
            den = sc + w_int * jnp.sum(q * n0, axis=1, keepdims=True)
            hh = num / jnp.maximum(jnp.abs(den), jnp.exp(-m_t))
            hn = hh * lax.rsqrt(jnp.mean(hh * hh, axis=-1, keepdims=True) + EPS)
            hn = hn * gn_ref[:, h * DV:(h + 1) * DV]
            ym_ref[s, :, h * DV:(h + 1) * DV] = jax.nn.sigmoid(o) * hn
            c_out[s, h] = w_int * c0 + (p * to_col(k)) * v
            n_out[s, h:h + 1, :] = w_int * n0 + p * k
            m_new_row = jnp.where(lane_row == h, m_t, m_new_row)
        m_out[s] = m_new_row
        return carry

    lax.fori_loop(0, sb, per_sample, 0)


def _mixer_sample(z, zs3, zg3, st, c_s, n_s, m3, w_conv, g_conv, b_gates, g_norm, row0, n_dec, sb):
    cb, qw, vw = CONV_DIM, HEADS * DK, HEADS * DV
    r0 = row0 // sb
    row = lambda i: r0 + i
    return pl.pallas_call(
        _mixer_sample_kernel,
        grid=(n_dec // sb,),
        in_specs=[
            pl.BlockSpec((sb, cb), lambda i: (row(i), COL_B // cb)),
            pl.BlockSpec((sb, cb), lambda i: (row(i), COL_C // cb)),
            pl.BlockSpec((sb, cb), lambda i: (row(i), COL_H // cb)),
            pl.BlockSpec((sb, 1, qw), lambda i: (i, 0, COL_Q // qw)),
            pl.BlockSpec((sb, 1, qw), lambda i: (i, 0, COL_K // qw)),
            pl.BlockSpec((sb, 1, vw), lambda i: (i, 0, COL_V // vw)),
            pl.BlockSpec((sb, 1, vw), lambda i: (i, 0, COL_O // vw)),
            pl.BlockSpec((sb, 1, LANES), lambda i: (i, 0, 0)),
            pl.BlockSpec((sb, 2 * cb), lambda i: (i, 0)),
            pl.BlockSpec((sb, HEADS, DK, DV), lambda i: (i, 0, 0, 0)),
            pl.BlockSpec((sb, HEADS, DK), lambda i: (i, 0, 0)),
            pl.BlockSpec((sb, 1, LANES), lambda i: (i, 0, 0)),
            pl.BlockSpec((8, cb), lambda i: (0, 0)),
            pl.BlockSpec((1, cb), lambda i: (0, 0)),
            pl.BlockSpec((1, LANES), lambda i: (0, 0)),
            pl.BlockSpec((1, vw), lambda i: (0, 0)),
        ],
        out_specs=[
            pl.BlockSpec((sb, cb), lambda i: (i, 0)),
            pl.BlockSpec((sb, 1, vw), lambda i: (i, 0, 0)),
            pl.BlockSpec((sb, 2 * cb), lambda i: (i, 0)),
            pl.BlockSpec((sb, HEADS, DK, DV), lambda i: (i, 0, 0, 0)),
            pl.BlockSpec((sb, HEADS, DK), lambda i: (i, 0, 0)),
            pl.BlockSpec((sb, 1, LANES), lambda i: (i, 0, 0)),
        ],
        out_shape=[
            jax.ShapeDtypeStruct((n_dec, cb), BF16),
            jax.ShapeDtypeStruct((n_dec, 1, vw), F32),
            jax.ShapeDtypeStruct((n_dec, 2 * cb), F32),
            jax.ShapeDtypeStruct((n_dec, HEADS, DK, DV), F32),
            jax.ShapeDtypeStruct((n_dec, HEADS, DK), F32),
            jax.ShapeDtypeStruct((n_dec, 1, LANES), F32),
        ],
        compiler_params=_cparams(("arbitrary",)),
        name="mixer_sample",
    )(z, z, z, zs3, zs3, zs3, zs3, zg3, st, c_s, n_s, m3, w_conv, g_conv, b_gates, g_norm)


def _outproj_router_kernel(yc_ref, ym_ref, x_ref, wo1_ref, wo2_ref, g_ref, wr_hi_ref, wr_lo_ref, br_ref,
                           cin_ref, h1_ref, xn_ref, e_ref, gate_ref, rank_ref, cnt_ref, carry):
    tm = x_ref.shape[0]

    @pl.when(pl.program_id(0) == 0)
    def _():
        carry[...] = cin_ref[...]

    mix = (jnp.dot(yc_ref[...], wo1_ref[...], preferred_element_type=F32)
           + jnp.dot(ym_ref[...], wo2_ref[...], preferred_element_type=F32))
    h1 = x_ref[...] + mix
    h1_ref[...] = h1
    xn = _rms(h1, g_ref[...])
    xn_ref[...] = xn

    xh = xn.astype(BF16)
    xl = (xn - xh.astype(F32)).astype(BF16)
    logits = (jnp.dot(xh, wr_hi_ref[...], preferred_element_type=F32)
              + jnp.dot(xl, wr_hi_ref[...], preferred_element_type=F32)
              + jnp.dot(xh, wr_lo_ref[...], preferred_element_type=F32)) + br_ref[...]
    lane = lax.broadcasted_iota(jnp.int32, (tm, LANES), 1)
    lane_f = lane.astype(F32)
    work = jnp.where(lane < N_EXPERTS, logits, NEG_BIG)

    tops, idxs = [], []
    chosen = jnp.zeros((tm, LANES), F32)
    for _ in range(TOP_K):
        mx = jnp.max(work, axis=1, keepdims=True)
        idx = jnp.min(jnp.where(work == mx, lane_f, float(LANES)), axis=1, keepdims=True)
        sel = lane_f == idx
        tops.append(mx)
        idxs.append(idx)
        chosen = jnp.where(sel, 1.0, chosen)
        work = jnp.where(sel, NEG_BIG, work)

    exps = [jnp.exp(t - tops[0]) for t in tops]
    denom = exps[0] + exps[1] + exps[2] + exps[3]

    r_i = lax.broadcasted_iota(jnp.int32, (tm, tm), 0)
    c_i = lax.broadcasted_iota(jnp.int32, (tm, tm), 1)
    before = jnp.dot((c_i < r_i).astype(BF16), chosen.astype(BF16),
                     preferred_element_type=F32) + carry[...]

    e_out = jnp.zeros((tm, LANES), F32)
    g_out = jnp.zeros((tm, LANES), F32)
    r_out = jnp.zeros((tm, LANES), F32)
    for kk in range(TOP_K):
        rank = jnp.sum(jnp.where(lane_f == idxs[kk], before, 0.0), axis=1, keepdims=True)
        e_out = jnp.where(lane == kk, idxs[kk], e_out)
        g_out = jnp.where(lane == kk, exps[kk] / denom, g_out)
        r_out = jnp.where(lane == kk, rank, r_out)
    gate_ref[...] = g_out
    for c in range(tm // LANES):
        rows = slice(c * LANES, (c + 1) * LANES)
        e_ref[:, rows] = e_out[rows].T[0:8].astype(jnp.int32)
        rank_ref[:, rows] = r_out[rows].T[0:8].astype(jnp.int32)

    new_carry = carry[...] + jnp.sum(chosen, axis=0, keepdims=True)
    carry[...] = new_carry
    cnt_ref[...] = new_carry


def _outproj_router(y_conv, y_mlstm, x, wo1, wo2, g_ffn, wr_hi, wr_lo, b_router, cnt_in, tm):
    n_tok = x.shape[0]
    full = lambda shape: pl.BlockSpec(shape, lambda i: (0,) * len(shape))
    tile = lambda w: pl.BlockSpec((tm, w), lambda i: (i, 0))
    choice_major = pl.BlockSpec((8, tm), lambda i: (0, i))
    return pl.pallas_call(
        _outproj_router_kernel,
        grid=(n_tok // tm,),
        in_specs=[
            tile(CONV_DIM), tile(HEADS * DV), tile(D_MODEL),
            full((CONV_DIM, D_MODEL)), full((HEADS * DV, D_MODEL)), full((1, D_MODEL)),
            full((D_MODEL, LANES)), full((D_MODEL, LANES)), full((1, LANES)), full((1, LANES)),
        ],
        out_specs=[tile(D_MODEL), tile(D_MODEL), choice_major, tile(LANES), choice_major, full((1, LANES))],
        out_shape=[
            jax.ShapeDtypeStruct((n_tok, D_MODEL), F32),
            jax.ShapeDtypeStruct((n_tok, D_MODEL), F32),
            jax.ShapeDtypeStruct((8, n_tok), jnp.int32),
            jax.ShapeDtypeStruct((n_tok, LANES), F32),
            jax.ShapeDtypeStruct((8, n_tok), jnp.int32),
            jax.ShapeDtypeStruct((1, LANES), F32),
        ],
        scratch_shapes=[pltpu.VMEM((1, LANES), F32)],
        compiler_params=_cparams(("arbitrary",)),
        name="outproj_router",
    )(y_conv, y_mlstm, x, wo1, wo2, g_ffn, wr_hi, wr_lo, b_router, cnt_in)


def _zero_fill_rows(dst, zero_scr, sem, start, count, act):
    low = count & 7
    for t in range(7):
        @pl.when(t < low)
        def _(t=t):
            act(pltpu.make_async_copy(zero_scr.at[pl.ds(0, 1)], dst.at[pl.ds(start + t, 1)], sem))

    off = start + low
    bit = 8
    while bit < SUB:
        take = count & bit

        @pl.when(take != 0)
        def _(off=off, bit=bit):
            row = pl.multiple_of(off, 8)
            act(pltpu.make_async_copy(zero_scr.at[pl.ds(0, bit)], dst.at[pl.ds(row, bit)], sem))

        off = off + take
        bit *= 2


def _zero_fill_chunks(dst, zero_scr, sem, start, act):
    def body(j, c):
        row = pl.multiple_of(start + j * SUB, SUB)
        act(pltpu.make_async_copy(zero_scr, dst.at[pl.ds(row, SUB)], sem))
        return c

    lax.fori_loop(0, (dst.shape[0] - start) // SUB, body, 0)


def _dispatch_kernel(dest_ref, cnt_ref, pstart_ref, xn_ref, xn_s_ref, xs_out, zero_scr, sem, zsem):
    tm = xn_ref.shape[0]
    step = pl.program_id(0)
    last = pl.num_programs(0) - 1

    def fill(act):
        for e in range(N_EXPERTS):
            c = cnt_ref[e]
            _zero_fill_rows(xs_out, zero_scr, zsem, pstart_ref[e] + c, (SUB - c % SUB) % SUB, act)
        _zero_fill_chunks(xs_out, zero_scr, zsem, pstart_ref[N_EXPERTS], act)

    @pl.when(step == 0)
    def _():
        zero_scr[...] = jnp.zeros_like(zero_scr)
        fill(lambda cp: cp.start())

    def scatter_rows(src, base):
        rows = src.shape[0]
        n_all = dest_ref.shape[0] // TOP_K

        def issue(r, c):
            for kk in range(TOP_K):
                d = dest_ref[kk * n_all + base + r]
                pltpu.make_async_copy(src.at[pl.ds(r, 1)], xs_out.at[pl.ds(d, 1)], sem).start()
            return c

        lax.fori_loop(0, rows, issue, 0, unroll=ISSUE_UNROLL)
        for _ in range(TOP_K):
            pltpu.make_async_copy(src, xs_out.at[pl.ds(0, rows)], sem).wait()

    scatter_rows(xn_ref, step * tm)

    @pl.when(step == last)
    def _():
        scatter_rows(xn_s_ref, pl.num_programs(0) * tm)

    @pl.when(step == 0)
    def _():
        fill(lambda cp: cp.wait())


def _dispatch(dest_flat, counts, pstart, xn_p, xn_s, n_slots, tm):
    n_prompt = xn_p.shape[0]
    return pl.pallas_call(
        _dispatch_kernel,
        grid_spec=pltpu.PrefetchScalarGridSpec(
            num_scalar_prefetch=3,
            grid=(n_prompt // tm,),
            in_specs=[pl.BlockSpec((tm, D_MODEL), lambda i, d, c, p: (i, 0)),
                      pl.BlockSpec(xn_s.shape, lambda i, d, c, p: (0, 0))],
            out_specs=pl.BlockSpec(memory_space=pl.ANY),
            scratch_shapes=[pltpu.VMEM((SUB, D_MODEL), F32), pltpu.SemaphoreType.DMA(()),
                            pltpu.SemaphoreType.DMA(())],
        ),
        out_shape=jax.ShapeDtypeStruct((n_slots, D_MODEL), F32),
        compiler_params=_cparams(("arbitrary",)),
        name="dispatch",
    )(dest_flat, counts, pstart, xn_p, xn_s)


def _expert_kernel(se_ref, st_ref, sn_ref, end_ref, x_ref, wg_ref, wu_ref, bg_ref, bu_ref, wd_ref, bd_ref,
                   y_ref, xb_scr, acc, zero_scr, sem, zsem):
    i, f = pl.program_id(0), pl.program_id(1)
    nf = pl.num_programs(1)
    tmb = acc.shape[0]
    n = sn_ref[i]
    start = st_ref[i]

    @pl.when((i == 0) & (f == 0))
    def _():
        zero_scr[...] = jnp.zeros_like(zero_scr)
        _zero_fill_chunks(y_ref, zero_scr, zsem, end_ref[0], lambda cp: cp.start())
        _zero_fill_chunks(y_ref, zero_scr, zsem, end_ref[0], lambda cp: cp.wait())

    def y_copy(o, m):
        row = pl.multiple_of(start + o, SUB)
        return pltpu.make_async_copy(acc.at[pl.ds(o, m)], y_ref.at[pl.ds(row, m)], sem)

    def for_each_sub_block(fn):
        kmax = tmb // SUB
        nb = (n + SUB - 1) // SUB
        for k in (kmax, kmax - 1):
            pl.when(nb == k)(functools.partial(fn, 0, k * SUB))
        bits = [b for b in (64, 32, 16, 8, 4, 2, 1) if b <= kmax - 2]
        small = nb <= kmax - 2
        for b in bits:
            higher = [h for h in bits if h > b]
            for mask in range(1 << len(higher)):
                above = sum(h for j, h in enumerate(higher) if (mask >> j) & 1)
                if above + b > kmax - 2:
                    continue
                cond = small & ((nb & b) != 0) & ((nb & sum(higher)) == above)
                pl.when(cond)(functools.partial(fn, above * SUB, b * SUB))

    @pl.when(n > 0)
    def _():
        @pl.when(f == 0)
        def _():
            xb_scr[...] = x_ref[...].astype(BF16)
            acc[...] = jnp.broadcast_to(bd_ref[0], acc.shape)

        kc = D_MODEL // MOE_SPLIT

        def sub_block(o, m):
            g = bg_ref[0]
            u = bu_ref[0]
            for c in range(MOE_SPLIT):
                xb = xb_scr[o:o + m, c * kc:(c + 1) * kc]
                g = g + jnp.dot(xb, wg_ref[0, c * kc:(c + 1) * kc, :].astype(BF16), preferred_element_type=F32)
                u = u + jnp.dot(xb, wu_ref[0, c * kc:(c + 1) * kc, :].astype(BF16), preferred_element_type=F32)
            g = jnp.minimum(g, SWIGLU_LIMIT)
            u = jnp.clip(u, -SWIGLU_LIMIT, SWIGLU_LIMIT)
            act = (g * jax.nn.sigmoid(SWIGLU_ALPHA * g) * (u + 1.0)).astype(BF16)
            for c in range(MOE_SPLIT):
                cols = slice(c * kc, (c + 1) * kc)
                acc[o:o + m, cols] += jnp.dot(act, wd_ref[0, :, cols].astype(BF16), preferred_element_type=F32)

            @pl.when(f == nf - 1)
            def _():
                y_copy(o, m).start()

        for_each_sub_block(sub_block)

        @pl.when(f == nf - 1)
        def _():
            for_each_sub_block(lambda o, m: y_copy(o, m).wait())


def _experts(sb_e, sb_start, sb_n, end, xs, w_gate_up, b_gate_up, w_down, b_down, nsb, tmb, tf):
    n_slots = xs.shape[0]
    nf = D_FF // tf

    def fidx(i, f, sn):
        return jnp.where(sn[i] > 0, f, nf - 1)

    return pl.pallas_call(
        _expert_kernel,
        grid_spec=pltpu.PrefetchScalarGridSpec(
            num_scalar_prefetch=4,
            grid=(nsb, nf),
            in_specs=[
                pl.BlockSpec((pl.Element(tmb), pl.Element(D_MODEL)),
                             lambda i, f, se, st, sn, en: (pl.multiple_of(st[i], SUB), 0)),
                pl.BlockSpec((1, D_MODEL, tf), lambda i, f, se, st, sn, en: (se[i], 0, fidx(i, f, sn))),
                pl.BlockSpec((1, D_MODEL, tf), lambda i, f, se, st, sn, en: (se[i], 0, nf + fidx(i, f, sn))),
                pl.BlockSpec((1, 1, tf), lambda i, f, se, st, sn, en: (se[i], 0, fidx(i, f, sn))),
                pl.BlockSpec((1, 1, tf), lambda i, f, se, st, sn, en: (se[i], 0, nf + fidx(i, f, sn))),
                pl.BlockSpec((1, tf, D_MODEL), lambda i, f, se, st, sn, en: (se[i], fidx(i, f, sn), 0)),
                pl.BlockSpec((1, 1, D_MODEL), lambda i, f, se, st, sn, en: (se[i], 0, 0)),
            ],
            out_specs=pl.BlockSpec(memory_space=pl.ANY),
            scratch_shapes=[pltpu.VMEM((tmb, D_MODEL), BF16), pltpu.VMEM((tmb, D_MODEL), F32),
                            pltpu.VMEM((SUB, D_MODEL), F32),
                            pltpu.SemaphoreType.DMA(()), pltpu.SemaphoreType.DMA(())],
        ),
        out_shape=jax.ShapeDtypeStruct((n_slots, D_MODEL), F32),
        compiler_params=_cparams(("arbitrary", "arbitrary")),
        name="experts",
    )(sb_e, sb_start, sb_n, end, xs, w_gate_up, w_gate_up, b_gate_up, b_gate_up, w_down, b_down)


def _combine_kernel(dest_ref, h1_ref, gate_ref, p_ref, yb_ref, wpg_ref, wpp_ref, gp_ref, gf_ref,
                    out_ref, gbuf, sem):
    tm = h1_ref.shape[0]
    step = pl.program_id(0)
    slot = step % 2

    def gather_tile(t, s):
        n_all = dest_ref.shape[0] // TOP_K

        def issue(r, c):
            for kk in range(TOP_K):
                d = dest_ref[kk * n_all + t * tm + r]
                pltpu.make_async_copy(yb_ref.at[pl.ds(d, 1)], gbuf.at[s, kk, pl.ds(r, 1)], sem.at[s]).start()
            return c

        lax.fori_loop(0, tm, issue, 0, unroll=ISSUE_UNROLL)

    @pl.when(step == 0)
    def _():
        gather_tile(0, 0)

    @pl.when(step + 1 < pl.num_programs(0))
    def _():
        gather_tile(step + 1, 1 - slot)

    for kk in range(TOP_K):
        pltpu.make_async_copy(yb_ref.at[pl.ds(0, tm)], gbuf.at[slot, kk], sem.at[slot]).wait()

    gate = gate_ref[...]
    moe = gate[:, 0:1] * gbuf[slot, 0]
    for kk in range(1, TOP_K):
        moe = moe + gate[:, kk:kk + 1] * gbuf[slot, kk]
    h2 = h1_ref[...] + moe
    a = _rms(h2, gp_ref[...]).astype(BF16)
    pg = jax.nn.sigmoid(jnp.dot(a, wpg_ref[...], preferred_element_type=F32))
    pp = jnp.dot(p_ref[...].astype(BF16), wpp_ref[...], preferred_element_type=F32)
    h3 = h2 + pg * pp
    out_ref[...] = _rms(h3, gf_ref[...])


def _combine(dest_flat, h1, gate, p, yb, wpg, wpp, g_ple, g_final, tm):
    n_tok = h1.shape[0]
    full = lambda shape: pl.BlockSpec(shape, lambda i, d: (0,) * len(shape))
    tile = lambda w: pl.BlockSpec((tm, w), lambda i, d: (i, 0))
    return pl.pallas_call(
        _combine_kernel,
        grid_spec=pltpu.PrefetchScalarGridSpec(
            num_scalar_prefetch=1,
            grid=(n_tok // tm,),
            in_specs=[
                tile(D_MODEL), tile(LANES), tile(PLE_DIM),
                pl.BlockSpec(memory_space=pl.ANY),
                full((D_MODEL, D_MODEL)), full((PLE_DIM, D_MODEL)), full((1, D_MODEL)), full((1, D_MODEL)),
            ],
            out_specs=tile(D_MODEL),
            scratch_shapes=[pltpu.VMEM((2, TOP_K, tm, D_MODEL), F32), pltpu.SemaphoreType.DMA((2,))],
        ),
        out_shape=jax.ShapeDtypeStruct((n_tok, D_MODEL), F32),
        compiler_params=_cparams(("arbitrary",)),
        name="combine",
    )(dest_flat, h1, gate, p, yb, wpg, wpp, g_ple, g_final)


def kernel(x_prompt, x_sample, state_conv, state_mlstm_C, state_mlstm_n, state_mlstm_m, p_prompt, p_sample,
           g_mix, w_in, b_gates, w_conv, g_conv_out, g_mlstm_out, w_out, g_ffn, w_router, b_router,
           w_gate_up, b_gate_up, w_down, b_down, g_ple, w_ple_gate, w_ple_proj, g_final):
    batch, seq, _ = x_prompt.shape
    n_dec = x_sample.shape[0]
    assert w_in.shape[0] == 1 and x_sample.shape[1] == 1
    n_prompt = batch * seq
    n_tok = n_prompt + n_dec

    x_p = x_prompt.reshape(n_prompt, D_MODEL)
    x_s = x_sample.reshape(n_dec, D_MODEL)

    row = lambda a: a.reshape(1, -1)
    pad_lanes = lambda a: jnp.pad(a, ((0, 0), (0, LANES - a.shape[1])))
    w_gates = pad_lanes(w_in[0, :, MAIN_COLS:]).astype(BF16)
    bg = pad_lanes(row(b_gates[0]))
    wc = jnp.pad(w_conv[0], ((0, 8 - w_conv.shape[1]), (0, 0)))
    g_norm = row(g_mlstm_out[0])

    a_p, zg_p = _norm_gates(x_p, row(g_mix[0]), w_gates, tm=1024)
    a_s, zg_s = _norm_gates(x_s, row(g_mix[0]), w_gates, tm=n_dec)
    z_p = _inproj(a_p, w_in, tm=1024, tn=1024)
    z_s = _inproj(a_s, w_in, tm=n_dec, tn=1024)

    yc_p, conv_p = _conv_prompt(z_p, wc, row(g_conv_out[0]), batch, seq, tt=512)
    ym_p, c_p, n_p, m_p = _mlstm_prompt(z_p, zg_p, bg, g_norm, batch, seq)
    yc_s, ym_s, conv_s, c_s, n_s, m_s = _mixer_sample(
        z_s, z_s.reshape(n_dec, 1, MAIN_COLS), zg_s.reshape(n_dec, 1, LANES),
        state_conv[0].reshape(n_dec, 2 * CONV_DIM), state_mlstm_C[0], state_mlstm_n[0],
        pad_lanes(state_mlstm_m[0]).reshape(n_dec, 1, LANES), wc, row(g_conv_out[0]), bg, g_norm,
        row0=0, n_dec=n_dec, sb=16)
    m_s = m_s[:, 0, :HEADS]
    ym_s = ym_s.reshape(n_dec, HEADS * DV).astype(BF16)

    wo = w_out[0].astype(BF16)
    wr = pad_lanes(w_router[0])
    wr_hi = wr.astype(BF16)
    wr_lo = (wr - wr_hi.astype(F32)).astype(BF16)
    router_args = (wo[:CONV_DIM], wo[CONV_DIM:], row(g_ffn[0]), wr_hi, wr_lo, pad_lanes(row(b_router[0])))
    h1_p, xn_p, e_p, gate_p, rank_p, cnt_p = _outproj_router(
        yc_p, ym_p, x_p, *router_args, jnp.zeros((1, LANES), F32), tm=512)
    h1_s, xn_s, e_s, gate_s, rank_s, cnt = _outproj_router(
        yc_s, ym_s, x_s, *router_args, cnt_p, tm=n_dec)

    n_assign = n_tok * TOP_K
    tmb = MOE_TMB
    nsb = n_assign // tmb + N_EXPERTS
    n_slots = -(-(n_assign + N_EXPERTS * (SUB - 1)) // SUB) * SUB + tmb
    counts = cnt[0, :N_EXPERTS].astype(jnp.int32)
    padded = (counts + SUB - 1) // SUB * SUB
    pend = jnp.cumsum(padded)
    pstart = jnp.concatenate([jnp.zeros((1,), jnp.int32), pend]).astype(jnp.int32)
    sb_per_e = (counts + tmb - 1) // tmb
    sb_cum = jnp.cumsum(sb_per_e)
    n_used = sb_cum[-1]
    sb = jnp.minimum(jnp.arange(nsb, dtype=jnp.int32), n_used - 1)
    sb_e = jnp.minimum(jnp.sum((sb_cum[None, :] <= sb[:, None]).astype(jnp.int32), axis=1), N_EXPERTS - 1)
    sb_j = sb - (sb_cum - sb_per_e)[sb_e]
    sb_start = (pstart[sb_e] + sb_j * tmb).astype(jnp.int32)
    sb_n = jnp.where(jnp.arange(nsb) < n_used, jnp.clip(counts[sb_e] - sb_j * tmb, 0, tmb), 0).astype(jnp.int32)
    slot_of = lambda e, r: (pstart[e[:TOP_K]] + r[:TOP_K]).astype(jnp.int32)
    dest_p = slot_of(e_p, rank_p)
    dest_s = slot_of(e_s, rank_s)

    xs = _dispatch(jnp.concatenate([dest_p, dest_s], axis=1).reshape(-1), counts, pstart, xn_p, xn_s,
                   n_slots, tm=512)
    dest_p = dest_p.reshape(-1)
    dest_s = dest_s.reshape(-1)
    yb = _experts(sb_e.astype(jnp.int32), sb_start, sb_n, pstart[N_EXPERTS:], xs, w_gate_up[0],
                  b_gate_up[0].reshape(N_EXPERTS, 1, 2 * D_FF), w_down[0],
                  b_down[0].reshape(N_EXPERTS, 1, D_MODEL), nsb, tmb, MOE_TF)

    ple_args = (w_ple_gate[0].astype(BF16), w_ple_proj[0].astype(BF16), row(g_ple[0]), row(g_final))
    out_p = _combine(dest_p, h1_p, gate_p, p_prompt[0].reshape(n_prompt, PLE_DIM), yb, *ple_args, tm=256)
    out_s = _combine(dest_s, h1_s, gate_s, p_sample[0].reshape(n_dec, PLE_DIM), yb, *ple_args, tm=n_dec)

    y_prompt = out_p.reshape(batch, seq, D_MODEL)
    y_sample = out_s.reshape(n_dec, 1, D_MODEL)
    return (y_prompt, y_sample,
            conv_p[None], c_p[None], n_p[None], m_p[None, :, :, 0],
            conv_s.reshape(1, n_dec, 2, CONV_DIM), c_s[None], n_s[None], m_s[None])
```

```python
import functools

import jax
import jax.numpy as jnp
from jax import lax
from jax.experimental import pallas as pl
from jax.experimental.pallas import tpu as pltpu

F32 = jnp.float32
BF16 = jnp.bfloat16

D_MODEL = 2048
CONV_DIM = 1024
HEADS = 4
DK = 128
DV = 256
CHUNK = 128
N_EXPERTS = 32
TOP_K = 4
D_FF = 2048
PLE_DIM = 256
SWIGLU_LIMIT = 7.0
SWIGLU_ALPHA = 1.702
EPS = 1e-6
LANES = 128
NEG_BIG = -1e30
MAIN_COLS = 3 * CONV_DIM + HEADS * (2 * DK + 2 * DV)

COL_B, COL_C, COL_H = 0, CONV_DIM, 2 * CONV_DIM
COL_Q = 3 * CONV_DIM
COL_K = COL_Q + HEADS * DK
COL_V = COL_K + HEADS * DK
COL_O = COL_V + HEADS * DV

VMEM_LIMIT = 56 * 1024 * 1024

SUB = 128
MOE_TMB = 9 * SUB
MOE_TF = 512
MOE_SPLIT = 4
ISSUE_UNROLL = 8


def _cparams(sem, vmem=VMEM_LIMIT):
    return pltpu.CompilerParams(dimension_semantics=sem, vmem_limit_bytes=vmem)


def _rms(x, g):
    return x * lax.rsqrt(jnp.mean(x * x, axis=-1, keepdims=True) + EPS) * g


def _split3(x):
    x1 = x.astype(BF16)
    r1 = x - x1.astype(F32)
    x2 = r1.astype(BF16)
    x3 = (r1 - x2.astype(F32)).astype(BF16)
    return x1, x2, x3


def _log_sigmoid(x):
    return jnp.minimum(x, 0.0) - jnp.log(1.0 + jnp.exp(-jnp.abs(x)))


def _norm_gates_kernel(x_ref, g_ref, wg_ref, a_ref, zg_ref):
    a = _rms(x_ref[...], g_ref[...]).astype(BF16)
    a_ref[...] = a
    zg_ref[...] = jnp.dot(a, wg_ref[...], preferred_element_type=F32)


def _norm_gates(x, g, w_gates, tm):
    n_tok = x.shape[0]
    return pl.pallas_call(
        _norm_gates_kernel,
        grid=(n_tok // tm,),
        in_specs=[
            pl.BlockSpec((tm, D_MODEL), lambda i: (i, 0)),
            pl.BlockSpec((1, D_MODEL), lambda i: (0, 0)),
            pl.BlockSpec((D_MODEL, LANES), lambda i: (0, 0)),
        ],
        out_specs=[
            pl.BlockSpec((tm, D_MODEL), lambda i: (i, 0)),
            pl.BlockSpec((tm, LANES), lambda i: (i, 0)),
        ],
        out_shape=[
            jax.ShapeDtypeStruct((n_tok, D_MODEL), BF16),
            jax.ShapeDtypeStruct((n_tok, LANES), F32),
        ],
        compiler_params=_cparams(("parallel",)),
        name="norm_gates",
    )(x, g, w_gates)


def _inproj_kernel(a_ref, w_ref, z_ref, w_scr):
    @pl.when(pl.program_id(1) == 0)
    def _():
        w_scr[...] = w_ref[0].astype(BF16)

    z_ref[...] = jnp.dot(a_ref[...], w_scr[...], preferred_element_type=F32)


def _inproj(a, w_in, tm, tn):
    n_tok = a.shape[0]
    return pl.pallas_call(
        _inproj_kernel,
        grid=(MAIN_COLS // tn, n_tok // tm),
        in_specs=[
            pl.BlockSpec((tm, D_MODEL), lambda j, i: (i, 0)),
            pl.BlockSpec((1, D_MODEL, tn), lambda j, i: (0, 0, j)),
        ],
        out_specs=pl.BlockSpec((tm, tn), lambda j, i: (i, j)),
        out_shape=jax.ShapeDtypeStruct((n_tok, MAIN_COLS), F32),
        scratch_shapes=[pltpu.VMEM((D_MODEL, tn), BF16)],
        compiler_params=_cparams(("arbitrary", "arbitrary")),
        name="inproj",
    )(a, w_in)


def _conv_prompt_kernel(zb_ref, zc_ref, zh_ref, wc_ref, g_ref, y_ref, st_ref, carry):
    tt = zb_ref.shape[0]

    @pl.when(pl.program_id(1) == 0)
    def _():
        carry[...] = jnp.zeros_like(carry)

    u = zc_ref[...] * zh_ref[...]
    prev = carry[...]
    row = lax.broadcasted_iota(jnp.int32, (tt, 1), 0)
    u1 = jnp.where(row == 0, prev[7:8], pltpu.roll(u, 1, axis=0))
    u2 = jnp.where(row == 0, prev[6:7], jnp.where(row == 1, prev[7:8], pltpu.roll(u, 2, axis=0)))
    wc = wc_ref[...]
    conv = wc[0:1] * u2 + wc[1:2] * u1 + wc[2:3] * u
    y_ref[...] = _rms(zb_ref[...] * conv, g_ref[...]).astype(BF16)
    carry[...] = u[tt - 8:tt]
    st_ref[0] = u[tt - 2:tt]


def _conv_prompt(z, w_conv, g_conv, batch, seq, tt):
    nt = seq // tt
    n_tok = batch * seq
    cb = CONV_DIM
    return pl.pallas_call(
        _conv_prompt_kernel,
        grid=(batch, nt),
        in_specs=[
            pl.BlockSpec((tt, cb), lambda b, t: (b * nt + t, COL_B // cb)),
            pl.BlockSpec((tt, cb), lambda b, t: (b * nt + t, COL_C // cb)),
            pl.BlockSpec((tt, cb), lambda b, t: (b * nt + t, COL_H // cb)),
            pl.BlockSpec((8, cb), lambda b, t: (0, 0)),
            pl.BlockSpec((1, cb), lambda b, t: (0, 0)),
        ],
        out_specs=[
            pl.BlockSpec((tt, cb), lambda b, t: (b * nt + t, 0)),
            pl.BlockSpec((1, 2, cb), lambda b, t: (b, 0, 0)),
        ],
        out_shape=[
            jax.ShapeDtypeStruct((n_tok, cb), BF16),
            jax.ShapeDtypeStruct((batch, 2, cb), F32),
        ],
        scratch_shapes=[pltpu.VMEM((8, cb), F32)],
        compiler_params=_cparams(("parallel", "arbitrary")),
        name="conv_prompt",
    )(z, z, z, w_conv, g_conv)


def _tri_dot(tri_bf16, x, tri_first):
    out = None
    for part in _split3(x):
        d = (jnp.dot(tri_bf16, part, preferred_element_type=F32) if tri_first
             else jnp.dot(part, tri_bf16, preferred_element_type=F32))
        out = d if out is None else out + d
    return out


def _mlstm_prompt_kernel(q_ref, k_ref, v_ref, o_ref, zg_ref, bg_ref, gn_ref,
                         y_ref, c_out, n_out, m_out, c_scr, n_scr, m_scr):
    L = CHUNK

    @pl.when(pl.program_id(1) == 0)
    def _():
        c_scr[...] = jnp.zeros_like(c_scr)
        n_scr[...] = jnp.zeros_like(n_scr)
        m_scr[...] = jnp.zeros_like(m_scr)

    lane = lax.broadcasted_iota(jnp.int32, (L, LANES), 1)
    r_i = lax.broadcasted_iota(jnp.int32, (L, L), 0)
    c_i = lax.broadcasted_iota(jnp.int32, (L, L), 1)
    causal = c_i <= r_i
    tril = causal.astype(BF16)
    triu = (r_i <= c_i).astype(BF16)

    zg = zg_ref[...] + bg_ref[...]
    gates = jnp.where(lane < HEADS, zg, _log_sigmoid(zg))
    gates_t = gates.T
    bcol_all = _tri_dot(tril, gates, True)
    brow_all = _tri_dot(triu, gates_t, False)

    for h in range(HEADS):
        b_col = bcol_all[:, HEADS + h:HEADS + h + 1]
        b_row = brow_all[HEADS + h:HEADS + h + 1, :]
        i_col = gates[:, h:h + 1]
        i_row = gates_t[h:h + 1, :]
        b_end = b_col[L - 1:L, :]
        m0 = m_scr[h:h + 1, 0:1]
        n0 = n_scr[h:h + 1, :]
        c0 = c_scr[h]

        q = q_ref[:, h * DK:(h + 1) * DK]
        k = k_ref[:, h * DK:(h + 1) * DK] * (DK ** -0.5)
        v = v_ref[:, h * DV:(h + 1) * DV]
        qb, vb = q.astype(BF16), v.astype(BF16)

        dmat = jnp.where(causal, b_col - b_row + i_row, NEG_BIG)
        inter = b_col + m0
        m_t = jnp.maximum(inter, jnp.max(dmat, axis=1, keepdims=True))
        w_int = jnp.exp(inter - m_t)
        p = jnp.where(causal, jnp.exp(dmat - m_t), 0.0)
        s = lax.dot_general(qb, k.astype(BF16), (((1,), (1,)), ((), ())),
                            preferred_element_type=F32) * p
        num = (jnp.dot(s.astype(BF16), vb, preferred_element_type=F32)
               + w_int * jnp.dot(qb, c0.astype(BF16), preferred_element_type=F32))
        den = jnp.sum(s, axis=1, keepdims=True) + w_int * jnp.sum(q * n0, axis=1, keepdims=True)
        hh = num / jnp.maximum(jnp.abs(den), jnp.exp(-m_t))
        hn = hh * lax.rsqrt(jnp.mean(hh * hh, axis=-1, keepdims=True) + EPS)
        hn = hn * gn_ref[:, h * DV:(h + 1) * DV]
        y = jax.nn.sigmoid(o_ref[:, h * DV:(h + 1) * DV]) * hn
        y_ref[:, h * DV:(h + 1) * DV] = y.astype(BF16)

        g_col = b_end - b_col + i_col
        g_row = b_end - b_row + i_row
        m_new = jnp.maximum(b_end + m0, jnp.max(g_row, axis=1, keepdims=True))
        decay = jnp.exp(b_end + m0 - m_new)
        kw = k * jnp.exp(g_col - m_new)
        c_new = decay * c0 + lax.dot_general(kw.astype(BF16), vb, (((0,), (0,)), ((), ())),
                                             preferred_element_type=F32)
        n_new = decay * n0 + jnp.sum(kw, axis=0, keepdims=True)
        c_scr[h] = c_new
        n_scr[h:h + 1, :] = n_new
        m_scr[h:h + 1, :] = jnp.broadcast_to(m_new, (1, LANES))
        c_out[0, h] = c_new
        n_out[0, h:h + 1, :] = n_new
        m_out[0, h:h + 1, :] = jnp.broadcast_to(m_new, (1, LANES))


def _mlstm_prompt(z, zg, b_gates, g_norm, batch, seq):
    nc = seq // CHUNK
    n_tok = batch * seq
    qw, vw = HEADS * DK, HEADS * DV
    return pl.pallas_call(
        _mlstm_prompt_kernel,
        grid=(batch, nc),
        in_specs=[
            pl.BlockSpec((CHUNK, qw), lambda b, c: (b * nc + c, COL_Q // qw)),
            pl.BlockSpec((CHUNK, qw), lambda b, c: (b * nc + c, COL_K // qw)),
            pl.BlockSpec((CHUNK, vw), lambda b, c: (b * nc + c, COL_V // vw)),
            pl.BlockSpec((CHUNK, vw), lambda b, c: (b * nc + c, COL_O // vw)),
            pl.BlockSpec((CHUNK, LANES), lambda b, c: (b * nc + c, 0)),
            pl.BlockSpec((1, LANES), lambda b, c: (0, 0)),
            pl.BlockSpec((1, vw), lambda b, c: (0, 0)),
        ],
        out_specs=[
            pl.BlockSpec((CHUNK, vw), lambda b, c: (b * nc + c, 0)),
            pl.BlockSpec((1, HEADS, DK, DV), lambda b, c: (b, 0, 0, 0)),
            pl.BlockSpec((1, HEADS, DK), lambda b, c: (b, 0, 0)),
            pl.BlockSpec((1, HEADS, LANES), lambda b, c: (b, 0, 0)),
        ],
        out_shape=[
            jax.ShapeDtypeStruct((n_tok, vw), BF16),
            jax.ShapeDtypeStruct((batch, HEADS, DK, DV), F32),
            jax.ShapeDtypeStruct((batch, HEADS, DK), F32),
            jax.ShapeDtypeStruct((batch, HEADS, LANES), F32),
        ],
        scratch_shapes=[
            pltpu.VMEM((HEADS, DK, DV), F32),
            pltpu.VMEM((8, DK), F32),
            pltpu.VMEM((8, LANES), F32),
        ],
        compiler_params=_cparams(("parallel", "arbitrary")),
        name="mlstm_prompt",
    )(z, z, z, z, zg, b_gates, g_norm)


def _mixer_sample_kernel(zb_ref, zc_ref, zh_ref, q_ref, k_ref, v_ref, o_ref, zg_ref,
                         st_ref, c_ref, n_ref, m_ref, wc_ref, gc_ref, bg_ref, gn_ref,
                         yc_ref, ym_ref, st_out, c_out, n_out, m_out):
    sb = zb_ref.shape[0]
    u = zc_ref[...] * zh_ref[...]
    s0 = st_ref[:, 0:CONV_DIM]
    s1 = st_ref[:, CONV_DIM:2 * CONV_DIM]
    wc = wc_ref[...]
    conv = wc[0:1] * s0 + wc[1:2] * s1 + wc[2:3] * u
    yc_ref[...] = _rms(zb_ref[...] * conv, gc_ref[...]).astype(BF16)
    st_out[:, 0:CONV_DIM] = s1
    st_out[:, CONV_DIM:2 * CONV_DIM] = u

    lane_row = lax.broadcasted_iota(jnp.int32, (1, LANES), 1)
    eye = (lax.broadcasted_iota(jnp.int32, (DK, DK), 0)
           == lax.broadcasted_iota(jnp.int32, (DK, DK), 1)).astype(F32)

    def to_col(row):
        return jnp.sum(eye * row, axis=1, keepdims=True)

    def pick(row, j):
        return jnp.sum(jnp.where(lane_row == j, row, 0.0), axis=1, keepdims=True)

    def per_sample(s, carry):
        zg = zg_ref[s] + bg_ref[...]
        lf_all = _log_sigmoid(zg)
        m_row = m_ref[s]
        m_new_row = jnp.zeros((1, LANES), F32)
        for h in range(HEADS):
            q = q_ref[s, :, h * DK:(h + 1) * DK]
            k = k_ref[s, :, h * DK:(h + 1) * DK] * (DK ** -0.5)
            v = v_ref[s, :, h * DV:(h + 1) * DV]
            o = o_ref[s, :, h * DV:(h + 1) * DV]
            i_pre = pick(zg, h)
            lf = pick(lf_all, HEADS + h)
            m0 = pick(m_row, h)
            c0 = c_ref[s, h]
            n0 = n_ref[s, h:h + 1, :]
            inter = lf + m0
            m_t = jnp.maximum(inter, i_pre)
            w_int = jnp.exp(inter - m_t)
            p = jnp.exp(i_pre - m_t)
            sc = jnp.sum(q * k, axis=1, keepdims=True) * p
            qc = jnp.sum(to_col(q) * c0, axis=0, keepdims=True)
            num = sc * v + w_int * qc
            den = sc + w_int * jnp.sum(q * n0, axis=1, keepdims=True)
            hh = num / jnp.maximum(jnp.abs(den), jnp.exp(-m_t))
            hn = hh * lax.rsqrt(jnp.mean(hh * hh, axis=-1, keepdims=True) + EPS)
            hn = hn * gn_ref[:, h * DV:(h + 1) * DV]
            ym_ref[s, :, h * DV:(h + 1) * DV] = jax.nn.sigmoid(o) * hn
            c_out[s, h] = w_int * c0 + (p * to_col(k)) * v
            n_out[s, h:h + 1, :] = w_int * n0 + p * k
            m_new_row = jnp.where(lane_row == h, m_t, m_new_row)
        m_out[s] = m_new_row
        return carry

    lax.fori_loop(0, sb, per_sample, 0)


def _mixer_sample(z, zs3, zg3, st, c_s, n_s, m3, w_conv, g_conv, b_gates, g_norm, row0, n_dec, sb):
    cb, qw, vw = CONV_DIM, HEADS * DK, HEADS * DV
    r0 = row0 // sb
    row = lambda i: r0 + i
    return pl.pallas_call(
        _mixer_sample_kernel,
        grid=(n_dec // sb,),
        in_specs=[
            pl.BlockSpec((sb, cb), lambda i: (row(i), COL_B // cb)),
            pl.BlockSpec((sb, cb), lambda i: (row(i), COL_C // cb)),
            pl.BlockSpec((sb, cb), lambda i: (row(i), COL_H // cb)),
            pl.BlockSpec((sb, 1, qw), lambda i: (i, 0, COL_Q // qw)),
            pl.BlockSpec((sb, 1, qw), lambda i: (i, 0, COL_K // qw)),
            pl.BlockSpec((sb, 1, vw), lambda i: (i, 0, COL_V // vw)),
            pl.BlockSpec((sb, 1, vw), lambda i: (i, 0, COL_O // vw)),
            pl.BlockSpec((sb, 1, LANES), lambda i: (i, 0, 0)),
            pl.BlockSpec((sb, 2 * cb), lambda i: (i, 0)),
            pl.BlockSpec((sb, HEADS, DK, DV), lambda i: (i, 0, 0, 0)),
            pl.BlockSpec((sb, HEADS, DK), lambda i: (i, 0, 0)),
            pl.BlockSpec((sb, 1, LANES), lambda i: (i, 0, 0)),
            pl.BlockSpec((8, cb), lambda i: (0, 0)),
            pl.BlockSpec((1, cb), lambda i: (0, 0)),
            pl.BlockSpec((1, LANES), lambda i: (0, 0)),
            pl.BlockSpec((1, vw), lambda i: (0, 0)),
        ],
        out_specs=[
            pl.BlockSpec((sb, cb), lambda i: (i, 0)),
            pl.BlockSpec((sb, 1, vw), lambda i: (i, 0, 0)),
            pl.BlockSpec((sb, 2 * cb), lambda i: (i, 0)),
            pl.BlockSpec((sb, HEADS, DK, DV), lambda i: (i, 0, 0, 0)),
            pl.BlockSpec((sb, HEADS, DK), lambda i: (i, 0, 0)),
            pl.BlockSpec((sb, 1, LANES), lambda i: (i, 0, 0)),
        ],
        out_shape=[
            jax.ShapeDtypeStruct((n_dec, cb), BF16),
            jax.ShapeDtypeStruct((n_dec, 1, vw), F32),
            jax.ShapeDtypeStruct((n_dec, 2 * cb), F32),
            jax.ShapeDtypeStruct((n_dec, HEADS, DK, DV), F32),
            jax.ShapeDtypeStruct((n_dec, HEADS, DK), F32),
            jax.ShapeDtypeStruct((n_dec, 1, LANES), F32),
        ],
        compiler_params=_cparams(("arbitrary",)),
        name="mixer_sample",
    )(z, z, z, zs3, zs3, zs3, zs3, zg3, st, c_s, n_s, m3, w_conv, g_conv, b_gates, g_norm)


def _outproj_router_kernel(yc_ref, ym_ref, x_ref, wo1_ref, wo2_ref, g_ref, wr_hi_ref, wr_lo_ref, br_ref,
                           cin_ref, h1_ref, xn_ref, e_ref, gate_ref, rank_ref, cnt_ref, carry):
    tm = x_ref.shape[0]

    @pl.when(pl.program_id(0) == 0)
    def _():
        carry[...] = cin_ref[...]

    mix = (jnp.dot(yc_ref[...], wo1_ref[...], preferred_element_type=F32)
           + jnp.dot(ym_ref[...], wo2_ref[...], preferred_element_type=F32))
    h1 = x_ref[...] + mix
    h1_ref[...] = h1
    xn = _rms(h1, g_ref[...])
    xn_ref[...] = xn

    xh = xn.astype(BF16)
    xl = (xn - xh.astype(F32)).astype(BF16)
    logits = (jnp.dot(xh, wr_hi_ref[...], preferred_element_type=F32)
              + jnp.dot(xl, wr_hi_ref[...], preferred_element_type=F32)
              + jnp.dot(xh, wr_lo_ref[...], preferred_element_type=F32)) + br_ref[...]
    lane = lax.broadcasted_iota(jnp.int32, (tm, LANES), 1)
    lane_f = lane.astype(F32)
    work = jnp.where(lane < N_EXPERTS, logits, NEG_BIG)

    tops, idxs = [], []
    chosen = jnp.zeros((tm, LANES), F32)
    for _ in range(TOP_K):
        mx = jnp.max(work, axis=1, keepdims=True)
        idx = jnp.min(jnp.where(work == mx, lane_f, float(LANES)), axis=1, keepdims=True)
        sel = lane_f == idx
        tops.append(mx)
        idxs.append(idx)
        chosen = jnp.where(sel, 1.0, chosen)
        work = jnp.where(sel, NEG_BIG, work)

    exps = [jnp.exp(t - tops[0]) for t in tops]
    denom = exps[0] + exps[1] + exps[2] + exps[3]

    r_i = lax.broadcasted_iota(jnp.int32, (tm, tm), 0)
    c_i = lax.broadcasted_iota(jnp.int32, (tm, tm), 1)
    before = jnp.dot((c_i < r_i).astype(BF16), chosen.astype(BF16),
                     preferred_element_type=F32) + carry[...]

    e_out = jnp.zeros((tm, LANES), F32)
    g_out = jnp.zeros((tm, LANES), F32)
    r_out = jnp.zeros((tm, LANES), F32)
    for kk in range(TOP_K):
        rank = jnp.sum(jnp.where(lane_f == idxs[kk], before, 0.0), axis=1, keepdims=True)
        e_out = jnp.where(lane == kk, idxs[kk], e_out)
        g_out = jnp.where(lane == kk, exps[kk] / denom, g_out)
        r_out = jnp.where(lane == kk, rank, r_out)
    gate_ref[...] = g_out
    for c in range(tm // LANES):
        rows = slice(c * LANES, (c + 1) * LANES)
        e_ref[:, rows] = e_out[rows].T[0:8].astype(jnp.int32)
        rank_ref[:, rows] = r_out[rows].T[0:8].astype(jnp.int32)

    new_carry = carry[...] + jnp.sum(chosen, axis=0, keepdims=True)
    carry[...] = new_carry
    cnt_ref[...] = new_carry


def _outproj_router(y_conv, y_mlstm, x, wo1, wo2, g_ffn, wr_hi, wr_lo, b_router, cnt_in, tm):
    n_tok = x.shape[0]
    full = lambda shape: pl.BlockSpec(shape, lambda i: (0,) * len(shape))
    tile = lambda w: pl.BlockSpec((tm, w), lambda i: (i, 0))
    choice_major = pl.BlockSpec((8, tm), lambda i: (0, i))
    return pl.pallas_call(
        _outproj_router_kernel,
        grid=(n_tok // tm,),
        in_specs=[
            tile(CONV_DIM), tile(HEADS * DV), tile(D_MODEL),
            full((CONV_DIM, D_MODEL)), full((HEADS * DV, D_MODEL)), full((1, D_MODEL)),
            full((D_MODEL, LANES)), full((D_MODEL, LANES)), full((1, LANES)), full((1, LANES)),
        ],
        out_specs=[tile(D_MODEL), tile(D_MODEL), choice_major, tile(LANES), choice_major, full((1, LANES))],
        out_shape=[
            jax.ShapeDtypeStruct((n_tok, D_MODEL), F32),
            jax.ShapeDtypeStruct((n_tok, D_MODEL), F32),
            jax.ShapeDtypeStruct((8, n_tok), jnp.int32),
            jax.ShapeDtypeStruct((n_tok, LANES), F32),
            jax.ShapeDtypeStruct((8, n_tok), jnp.int32),
            jax.ShapeDtypeStruct((1, LANES), F32),
        ],
        scratch_shapes=[pltpu.VMEM((1, LANES), F32)],
        compiler_params=_cparams(("arbitrary",)),
        name="outproj_router",
    )(y_conv, y_mlstm, x, wo1, wo2, g_ffn, wr_hi, wr_lo, b_router, cnt_in)


def _zero_fill_rows(dst, zero_scr, sem, start, count, act):
    low = count & 7
    for t in range(7):
        @pl.when(t < low)
        def _(t=t):
            act(pltpu.make_async_copy(zero_scr.at[pl.ds(0, 1)], dst.at[pl.ds(start + t, 1)], sem))

    off = start + low
    bit = 8
    while bit < SUB:
        take = count & bit

        @pl.when(take != 0)
        def _(off=off, bit=bit):
            row = pl.multiple_of(off, 8)
            act(pltpu.make_async_copy(zero_scr.at[pl.ds(0, bit)], dst.at[pl.ds(row, bit)], sem))

        off = off + take
        bit *= 2


def _zero_fill_chunks(dst, zero_scr, sem, start, act):
    def body(j, c):
        row = pl.multiple_of(start + j * SUB, SUB)
        act(pltpu.make_async_copy(zero_scr, dst.at[pl.ds(row, SUB)], sem))
        return c

    lax.fori_loop(0, (dst.shape[0] - start) // SUB, body, 0)


def _dispatch_kernel(dest_ref, cnt_ref, pstart_ref, xn_ref, xn_s_ref, xs_out, zero_scr, sem, zsem):
    tm = xn_ref.shape[0]
    step = pl.program_id(0)
    last = pl.num_programs(0) - 1

    def fill(act):
        for e in range(N_EXPERTS):
            c = cnt_ref[e]
            _zero_fill_rows(xs_out, zero_scr, zsem, pstart_ref[e] + c, (SUB - c % SUB) % SUB, act)
        _zero_fill_chunks(xs_out, zero_scr, zsem, pstart_ref[N_EXPERTS], act)

    @pl.when(step == 0)
    def _():
        zero_scr[...] = jnp.zeros_like(zero_scr)
        fill(lambda cp: cp.start())

    def scatter_rows(src, base):
        rows = src.shape[0]
        n_all = dest_ref.shape[0] // TOP_K

        def issue(r, c):
            for kk in range(TOP_K):
                d = dest_ref[kk * n_all + base + r]
                pltpu.make_async_copy(src.at[pl.ds(r, 1)], xs_out.at[pl.ds(d, 1)], sem).start()
            return c

        lax.fori_loop(0, rows, issue, 0, unroll=ISSUE_UNROLL)
        for _ in range(TOP_K):
            pltpu.make_async_copy(src, xs_out.at[pl.ds(0, rows)], sem).wait()

    scatter_rows(xn_ref, step * tm)

    @pl.when(step == last)
    def _():
        scatter_rows(xn_s_ref, pl.num_programs(0) * tm)

    @pl.when(step == 0)
    def _():
        fill(lambda cp: cp.wait())


def _dispatch(dest_flat, counts, pstart, xn_p, xn_s, n_slots, tm):
    n_prompt = xn_p.shape[0]
    return pl.pallas_call(
        _dispatch_kernel,
        grid_spec=pltpu.PrefetchScalarGridSpec(
            num_scalar_prefetch=3,
            grid=(n_prompt // tm,),
            in_specs=[pl.BlockSpec((tm, D_MODEL), lambda i, d, c, p: (i, 0)),
                      pl.BlockSpec(xn_s.shape, lambda i, d, c, p: (0, 0))],
            out_specs=pl.BlockSpec(memory_space=pl.ANY),
            scratch_shapes=[pltpu.VMEM((SUB, D_MODEL), F32), pltpu.SemaphoreType.DMA(()),
                            pltpu.SemaphoreType.DMA(())],
        ),
        out_shape=jax.ShapeDtypeStruct((n_slots, D_MODEL), F32),
        compiler_params=_cparams(("arbitrary",)),
        name="dispatch",
    )(dest_flat, counts, pstart, xn_p, xn_s)


def _expert_kernel(se_ref, st_ref, sn_ref, end_ref, xs_ref, wg_ref, wu_ref, bg_ref, bu_ref, wd_ref, bd_ref,
                   y_ref, x_scr, xb_scr, acc, zero_scr, sem, zsem, xsem):
    i, f = pl.program_id(0), pl.program_id(1)
    nf = pl.num_programs(1)
    tmb = acc.shape[0]
    n = sn_ref[i]
    start = st_ref[i]

    def x_copy(j):
        row = pl.multiple_of(st_ref[j], SUB)
        return pltpu.make_async_copy(xs_ref.at[pl.ds(row, tmb)], x_scr, xsem)

    @pl.when((i == 0) & (f == 0))
    def _():
        zero_scr[...] = jnp.zeros_like(zero_scr)
        _zero_fill_chunks(y_ref, zero_scr, zsem, end_ref[0], lambda cp: cp.start())
        _zero_fill_chunks(y_ref, zero_scr, zsem, end_ref[0], lambda cp: cp.wait())

    def y_copy(o, m):
        row = pl.multiple_of(start + o, SUB)
        return pltpu.make_async_copy(acc.at[pl.ds(o, m)], y_ref.at[pl.ds(row, m)], sem)

    def for_each_sub_block(fn):
        kmax = tmb // SUB
        nb = (n + SUB - 1) // SUB
        for k in (kmax, kmax - 1):
            pl.when(nb == k)(functools.partial(fn, 0, k * SUB))
        bits = [b for b in (64, 32, 16, 8, 4, 2, 1) if b <= kmax - 2]
        small = nb <= kmax - 2
        for b in bits:
            higher = [h for h in bits if h > b]
            for mask in range(1 << len(higher)):
                above = sum(h for j, h in enumerate(higher) if (mask >> j) & 1)
                if above + b > kmax - 2:
                    continue
                cond = small & ((nb & b) != 0) & ((nb & sum(higher)) == above)
                pl.when(cond)(functools.partial(fn, above * SUB, b * SUB))

    @pl.when(n > 0)
    def _():
        @pl.when(f == 0)
        def _():
            @pl.when(i == 0)
            def _():
                x_copy(0).start()

            x_copy(i).wait()
            xb_scr[...] = x_scr[...].astype(BF16)
            nxt = jnp.minimum(i + 1, pl.num_programs(0) - 1)

            @pl.when((i + 1 < pl.num_programs(0)) & (sn_ref[nxt] > 0))
            def _():
                x_copy(nxt).start()

            acc[...] = jnp.broadcast_to(bd_ref[0], acc.shape)

        kc = D_MODEL // MOE_SPLIT

        def sub_block(o, m):
            g = bg_ref[0]
            u = bu_ref[0]
            for c in range(MOE_SPLIT):
                xb = xb_scr[o:o + m, c * kc:(c + 1) * kc]
                g = g + jnp.dot(xb, wg_ref[0, c * kc:(c + 1) * kc, :].astype(BF16), preferred_element_type=F32)
                u = u + jnp.dot(xb, wu_ref[0, c * kc:(c + 1) * kc, :].astype(BF16), preferred_element_type=F32)
            g = jnp.minimum(g, SWIGLU_LIMIT)
            u = jnp.clip(u, -SWIGLU_LIMIT, SWIGLU_LIMIT)
            act = (g * jax.nn.sigmoid(SWIGLU_ALPHA * g) * (u + 1.0)).astype(BF16)
            for c in range(MOE_SPLIT):
                cols = slice(c * kc, (c + 1) * kc)
                acc[o:o + m, cols] += jnp.dot(act, wd_ref[0, :, cols].astype(BF16), preferred_element_type=F32)

            @pl.when(f == nf - 1)
            def _():
                y_copy(o, m).start()

        for_each_sub_block(sub_block)

        @pl.when(f == nf - 1)
        def _():
            for_each_sub_block(lambda o, m: y_copy(o, m).wait())


def _experts(sb_e, sb_start, sb_n, end, xs, w_gate_up, b_gate_up, w_down, b_down, nsb, tmb, tf):
    n_slots = xs.shape[0]
    nf = D_FF // tf

    def fidx(i, f, sn):
        return jnp.where(sn[i] > 0, f, nf - 1)

    return pl.pallas_call(
        _expert_kernel,
        grid_spec=pltpu.PrefetchScalarGridSpec(
            num_scalar_prefetch=4,
            grid=(nsb, nf),
            in_specs=[
                pl.BlockSpec(memory_space=pl.ANY),
                pl.BlockSpec((1, D_MODEL, tf), lambda i, f, se, st, sn, en: (se[i], 0, fidx(i, f, sn))),
                pl.BlockSpec((1, D_MODEL, tf), lambda i, f, se, st, sn, en: (se[i], 0, nf + fidx(i, f, sn))),
                pl.BlockSpec((1, 1, tf), lambda i, f, se, st, sn, en: (se[i], 0, fidx(i, f, sn))),
                pl.BlockSpec((1, 1, tf), lambda i, f, se, st, sn, en: (se[i], 0, nf + fidx(i, f, sn))),
                pl.BlockSpec((1, tf, D_MODEL), lambda i, f, se, st, sn, en: (se[i], fidx(i, f, sn), 0)),
                pl.BlockSpec((1, 1, D_MODEL), lambda i, f, se, st, sn, en: (se[i], 0, 0)),
            ],
            out_specs=pl.BlockSpec(memory_space=pl.ANY),
            scratch_shapes=[pltpu.VMEM((tmb, D_MODEL), F32), pltpu.VMEM((tmb, D_MODEL), BF16),
                            pltpu.VMEM((tmb, D_MODEL), F32), pltpu.VMEM((SUB, D_MODEL), F32),
                            pltpu.SemaphoreType.DMA(()), pltpu.SemaphoreType.DMA(()),
                            pltpu.SemaphoreType.DMA(())],
        ),
        out_shape=jax.ShapeDtypeStruct((n_slots, D_MODEL), F32),
        compiler_params=_cparams(("arbitrary", "arbitrary")),
        name="experts",
    )(sb_e, sb_start, sb_n, end, xs, w_gate_up, w_gate_up, b_gate_up, b_gate_up, w_down, b_down)


def _combine_kernel(dest_ref, h1_ref, gate_ref, p_ref, yb_ref, wpg_ref, wpp_ref, gp_ref, gf_ref,
                    out_ref, gbuf, sem):
    tm = h1_ref.shape[0]
    step = pl.program_id(0)
    slot = step % 2

    def gather_tile(t, s):
        n_all = dest_ref.shape[0] // TOP_K

        def issue(r, c):
            for kk in range(TOP_K):
                d = dest_ref[kk * n_all + t * tm + r]
                pltpu.make_async_copy(yb_ref.at[pl.ds(d, 1)], gbuf.at[s, kk, pl.ds(r, 1)], sem.at[s]).start()
            return c

        lax.fori_loop(0, tm, issue, 0, unroll=ISSUE_UNROLL)

    @pl.when(step == 0)
    def _():
        gather_tile(0, 0)

    @pl.when(step + 1 < pl.num_programs(0))
    def _():
        gather_tile(step + 1, 1 - slot)

    for kk in range(TOP_K):
        pltpu.make_async_copy(yb_ref.at[pl.ds(0, tm)], gbuf.at[slot, kk], sem.at[slot]).wait()

    gate = gate_ref[...]
    moe = gate[:, 0:1] * gbuf[slot, 0]
    for kk in range(1, TOP_K):
        moe = moe + gate[:, kk:kk + 1] * gbuf[slot, kk]
    h2 = h1_ref[...] + moe
    a = _rms(h2, gp_ref[...]).astype(BF16)
    pg = jax.nn.sigmoid(jnp.dot(a, wpg_ref[...], preferred_element_type=F32))
    pp = jnp.dot(p_ref[...].astype(BF16), wpp_ref[...], preferred_element_type=F32)
    h3 = h2 + pg * pp
    out_ref[...] = _rms(h3, gf_ref[...])


def _combine(dest_flat, h1, gate, p, yb, wpg, wpp, g_ple, g_final, tm):
    n_tok = h1.shape[0]
    full = lambda shape: pl.BlockSpec(shape, lambda i, d: (0,) * len(shape))
    tile = lambda w: pl.BlockSpec((tm, w), lambda i, d: (i, 0))
    return pl.pallas_call(
        _combine_kernel,
        grid_spec=pltpu.PrefetchScalarGridSpec(
            num_scalar_prefetch=1,
            grid=(n_tok // tm,),
            in_specs=[
                tile(D_MODEL), tile(LANES), tile(PLE_DIM),
                pl.BlockSpec(memory_space=pl.ANY),
                full((D_MODEL, D_MODEL)), full((PLE_DIM, D_MODEL)), full((1, D_MODEL)), full((1, D_MODEL)),
            ],
            out_specs=tile(D_MODEL),
            scratch_shapes=[pltpu.VMEM((2, TOP_K, tm, D_MODEL), F32), pltpu.SemaphoreType.DMA((2,))],
        ),
        out_shape=jax.ShapeDtypeStruct((n_tok, D_MODEL), F32),
        compiler_params=_cparams(("arbitrary",)),
        name="combine",
    )(dest_flat, h1, gate, p, yb, wpg, wpp, g_ple, g_final)


def kernel(x_prompt, x_sample, state_conv, state_mlstm_C, state_mlstm_n, state_mlstm_m, p_prompt, p_sample,
           g_mix, w_in, b_gates, w_conv, g_conv_out, g_mlstm_out, w_out, g_ffn, w_router, b_router,
           w_gate_up, b_gate_up, w_down, b_down, g_ple, w_ple_gate, w_ple_proj, g_final):
    batch, seq, _ = x_prompt.shape
    n_dec = x_sample.shape[0]
    assert w_in.shape[0] == 1 and x_sample.shape[1] == 1
    n_prompt = batch * seq
    n_tok = n_prompt + n_dec

    x_p = x_prompt.reshape(n_prompt, D_MODEL)
    x_s = x_sample.reshape(n_dec, D_MODEL)

    row = lambda a: a.reshape(1, -1)
    pad_lanes = lambda a: jnp.pad(a, ((0, 0), (0, LANES - a.shape[1])))
    w_gates = pad_lanes(w_in[0, :, MAIN_COLS:]).astype(BF16)
    bg = pad_lanes(row(b_gates[0]))
    wc = jnp.pad(w_conv[0], ((0, 8 - w_conv.shape[1]), (0, 0)))
    g_norm = row(g_mlstm_out[0])

    a_p, zg_p = _norm_gates(x_p, row(g_mix[0]), w_gates, tm=1024)
    a_s, zg_s = _norm_gates(x_s, row(g_mix[0]), w_gates, tm=n_dec)
    z_p = _inproj(a_p, w_in, tm=1024, tn=1024)
    z_s = _inproj(a_s, w_in, tm=n_dec, tn=1024)

    yc_p, conv_p = _conv_prompt(z_p, wc, row(g_conv_out[0]), batch, seq, tt=512)
    ym_p, c_p, n_p, m_p = _mlstm_prompt(z_p, zg_p, bg, g_norm, batch, seq)
    yc_s, ym_s, conv_s, c_s, n_s, m_s = _mixer_sample(
        z_s, z_s.reshape(n_dec, 1, MAIN_COLS), zg_s.reshape(n_dec, 1, LANES),
        state_conv[0].reshape(n_dec, 2 * CONV_DIM), state_mlstm_C[0], state_mlstm_n[0],
        pad_lanes(state_mlstm_m[0]).reshape(n_dec, 1, LANES), wc, row(g_conv_out[0]), bg, g_norm,
        row0=0, n_dec=n_dec, sb=16)
    m_s = m_s[:, 0, :HEADS]
    ym_s = ym_s.reshape(n_dec, HEADS * DV).astype(BF16)

    wo = w_out[0].astype(BF16)
    wr = pad_lanes(w_router[0])
    wr_hi = wr.astype(BF16)
    wr_lo = (wr - wr_hi.astype(F32)).astype(BF16)
    router_args = (wo[:CONV_DIM], wo[CONV_DIM:], row(g_ffn[0]), wr_hi, wr_lo, pad_lanes(row(b_router[0])))
    h1_p, xn_p, e_p, gate_p, rank_p, cnt_p = _outproj_router(
        yc_p, ym_p, x_p, *router_args, jnp.zeros((1, LANES), F32), tm=512)
    h1_s, xn_s, e_s, gate_s, rank_s, cnt = _outproj_router(
        yc_s, ym_s, x_s, *router_args, cnt_p, tm=n_dec)

    n_assign = n_tok * TOP_K
    tmb = MOE_TMB
    nsb = n_assign // tmb + N_EXPERTS
    n_slots = -(-(n_assign + N_EXPERTS * (SUB - 1)) // SUB) * SUB + tmb
    counts = cnt[0, :N_EXPERTS].astype(jnp.int32)
    padded = (counts + SUB - 1) // SUB * SUB
    pend = jnp.cumsum(padded)
    pstart = jnp.concatenate([jnp.zeros((1,), jnp.int32), pend]).astype(jnp.int32)
    sb_per_e = (counts + tmb - 1) // tmb
    sb_cum = jnp.cumsum(sb_per_e)
    n_used = sb_cum[-1]
    sb = jnp.minimum(jnp.arange(nsb, dtype=jnp.int32), n_used - 1)
    sb_e = jnp.minimum(jnp.sum((sb_cum[None, :] <= sb[:, None]).astype(jnp.int32), axis=1), N_EXPERTS - 1)
    sb_j = sb - (sb_cum - sb_per_e)[sb_e]
    sb_start = (pstart[sb_e] + sb_j * tmb).astype(jnp.int32)
    sb_n = jnp.where(jnp.arange(nsb) < n_used, jnp.clip(counts[sb_e] - sb_j * tmb, 0, tmb), 0).astype(jnp.int32)
    expert_ids = jnp.arange(N_EXPERTS, dtype=jnp.int32)[:, None, None]

    def slot_of(e, r):
        first = jnp.sum(jnp.where(e[None, :TOP_K] == expert_ids, pstart[:N_EXPERTS, None, None], 0), axis=0)
        return (first + r[:TOP_K]).astype(jnp.int32)
    dest_p = slot_of(e_p, rank_p)
    dest_s = slot_of(e_s, rank_s)

    xs = _dispatch(jnp.concatenate([dest_p, dest_s], axis=1).reshape(-1), counts, pstart, xn_p, xn_s,
                   n_slots, tm=512)
    dest_p = dest_p.reshape(-1)
    dest_s = dest_s.reshape(-1)
    yb = _experts(sb_e.astype(jnp.int32), sb_start, sb_n, pstart[N_EXPERTS:], xs, w_gate_up[0],
                  b_gate_up[0].reshape(N_EXPERTS, 1, 2 * D_FF), w_down[0],
                  b_down[0].reshape(N_EXPERTS, 1, D_MODEL), nsb, tmb, MOE_TF)

    ple_args = (w_ple_gate[0].astype(BF16), w_ple_proj[0].astype(BF16), row(g_ple[0]), row(g_final))
    out_p = _combine(dest_p, h1_p, gate_p, p_prompt[0].reshape(n_prompt, PLE_DIM), yb, *ple_args, tm=256)
    out_s = _combine(dest_s, h1_s, gate_s, p_sample[0].reshape(n_dec, PLE_DIM), yb, *ple_args, tm=n_dec)

    y_prompt = out_p.reshape(batch, seq, D_MODEL)
    y_sample = out_s.reshape(n_dec, 1, D_MODEL)
    return (y_prompt, y_sample,
            conv_p[None], c_p[None], n_p[None], m_p[None, :, :, 0],
            conv_s.reshape(1, n_dec, 2, CONV_DIM), c_s[None], n_s[None], m_s[None])
```

```python
import functools

import jax
import jax.numpy as jnp
from jax import lax
from jax.experimental import pallas as pl
from jax.experimental.pallas import tpu as pltpu

F32 = jnp.float32
BF16 = jnp.bfloat16

D_MODEL = 2048
CONV_DIM = 1024
HEADS = 4
DK = 128
DV = 256
CHUNK = 128
N_EXPERTS = 32
TOP_K = 4
D_FF = 2048
PLE_DIM = 256
SWIGLU_LIMIT = 7.0
SWIGLU_ALPHA = 1.702
EPS = 1e-6
LANES = 128
NEG_BIG = -1e30
MAIN_COLS = 3 * CONV_DIM + HEADS * (2 * DK + 2 * DV)

COL_B, COL_C, COL_H = 0, CONV_DIM, 2 * CONV_DIM
COL_Q = 3 * CONV_DIM
COL_K = COL_Q + HEADS * DK
COL_V = COL_K + HEADS * DK
COL_O = COL_V + HEADS * DV

VMEM_LIMIT = 56 * 1024 * 1024

SUB = 128
MOE_TMB = 9 * SUB
MOE_TF = 512
MOE_SPLIT = 4
ISSUE_UNROLL = 8


def _cparams(sem, vmem=VMEM_LIMIT):
    return pltpu.CompilerParams(dimension_semantics=sem, vmem_limit_bytes=vmem)


def _rms(x, g):
    return x * lax.rsqrt(jnp.mean(x * x, axis=-1, keepdims=True) + EPS) * g


def _split3(x):
    x1 = x.astype(BF16)
    r1 = x - x1.astype(F32)
    x2 = r1.astype(BF16)
    x3 = (r1 - x2.astype(F32)).astype(BF16)
    return x1, x2, x3


def _log_sigmoid(x):
    return jnp.minimum(x, 0.0) - jnp.log(1.0 + jnp.exp(-jnp.abs(x)))


def _norm_gates_kernel(x_ref, g_ref, wg_ref, a_ref, zg_ref):
    a = _rms(x_ref[...], g_ref[...]).astype(BF16)
    a_ref[...] = a
    zg_ref[...] = jnp.dot(a, wg_ref[...], preferred_element_type=F32)


def _norm_gates(x, g, w_gates, tm):
    n_tok = x.shape[0]
    return pl.pallas_call(
        _norm_gates_kernel,
        grid=(n_tok // tm,),
        in_specs=[
            pl.BlockSpec((tm, D_MODEL), lambda i: (i, 0)),
            pl.BlockSpec((1, D_MODEL), lambda i: (0, 0)),
            pl.BlockSpec((D_MODEL, LANES), lambda i: (0, 0)),
        ],
        out_specs=[
            pl.BlockSpec((tm, D_MODEL), lambda i: (i, 0)),
            pl.BlockSpec((tm, LANES), lambda i: (i, 0)),
        ],
        out_shape=[
            jax.ShapeDtypeStruct((n_tok, D_MODEL), BF16),
            jax.ShapeDtypeStruct((n_tok, LANES), F32),
        ],
        compiler_params=_cparams(("parallel",)),
        name="norm_gates",
    )(x, g, w_gates)


def _inproj_kernel(a_ref, w_ref, z_ref, w_scr):
    @pl.when(pl.program_id(1) == 0)
    def _():
        w_scr[...] = w_ref[0].astype(BF16)

    z_ref[...] = jnp.dot(a_ref[...], w_scr[...], preferred_element_type=F32)


def _inproj(a, w_in, tm, tn):
    n_tok = a.shape[0]
    return pl.pallas_call(
        _inproj_kernel,
        grid=(MAIN_COLS // tn, n_tok // tm),
        in_specs=[
            pl.BlockSpec((tm, D_MODEL), lambda j, i: (i, 0)),
            pl.BlockSpec((1, D_MODEL, tn), lambda j, i: (0, 0, j)),
        ],
        out_specs=pl.BlockSpec((tm, tn), lambda j, i: (i, j)),
        out_shape=jax.ShapeDtypeStruct((n_tok, MAIN_COLS), F32),
        scratch_shapes=[pltpu.VMEM((D_MODEL, tn), BF16)],
        compiler_params=_cparams(("arbitrary", "arbitrary")),
        name="inproj",
    )(a, w_in)


def _conv_prompt_kernel(zb_ref, zc_ref, zh_ref, wc_ref, g_ref, y_ref, st_ref, carry):
    tt = zb_ref.shape[0]

    @pl.when(pl.program_id(1) == 0)
    def _():
        carry[...] = jnp.zeros_like(carry)

    u = zc_ref[...] * zh_ref[...]
    prev = carry[...]
    row = lax.broadcasted_iota(jnp.int32, (tt, 1), 0)
    u1 = jnp.where(row == 0, prev[7:8], pltpu.roll(u, 1, axis=0))
    u2 = jnp.where(row == 0, prev[6:7], jnp.where(row == 1, prev[7:8], pltpu.roll(u, 2, axis=0)))
    wc = wc_ref[...]
    conv = wc[0:1] * u2 + wc[1:2] * u1 + wc[2:3] * u
    y_ref[...] = _rms(zb_ref[...] * conv, g_ref[...]).astype(BF16)
    carry[...] = u[tt - 8:tt]
    st_ref[0] = u[tt - 2:tt]


def _conv_prompt(z, w_conv, g_conv, batch, seq, tt):
    nt = seq // tt
    n_tok = batch * seq
    cb = CONV_DIM
    return pl.pallas_call(
        _conv_prompt_kernel,
        grid=(batch, nt),
        in_specs=[
            pl.BlockSpec((tt, cb), lambda b, t: (b * nt + t, COL_B // cb)),
            pl.BlockSpec((tt, cb), lambda b, t: (b * nt + t, COL_C // cb)),
            pl.BlockSpec((tt, cb), lambda b, t: (b * nt + t, COL_H // cb)),
            pl.BlockSpec((8, cb), lambda b, t: (0, 0)),
            pl.BlockSpec((1, cb), lambda b, t: (0, 0)),
        ],
        out_specs=[
            pl.BlockSpec((tt, cb), lambda b, t: (b * nt + t, 0)),
            pl.BlockSpec((1, 2, cb), lambda b, t: (b, 0, 0)),
        ],
        out_shape=[
            jax.ShapeDtypeStruct((n_tok, cb), BF16),
            jax.ShapeDtypeStruct((batch, 2, cb), F32),
        ],
        scratch_shapes=[pltpu.VMEM((8, cb), F32)],
        compiler_params=_cparams(("parallel", "arbitrary")),
        name="conv_prompt",
    )(z, z, z, w_conv, g_conv)


def _tri_dot(tri_bf16, x, tri_first):
    out = None
    for part in _split3(x):
        d = (jnp.dot(tri_bf16, part, preferred_element_type=F32) if tri_first
             else jnp.dot(part, tri_bf16, preferred_element_type=F32))
        out = d if out is None else out + d
    return out


def _mlstm_prompt_kernel(q_ref, k_ref, v_ref, o_ref, zg_ref, bg_ref, gn_ref,
                         y_ref, c_out, n_out, m_out, c_scr, n_scr, m_scr):
    L = CHUNK

    @pl.when(pl.program_id(1) == 0)
    def _():
        c_scr[...] = jnp.zeros_like(c_scr)
        n_scr[...] = jnp.zeros_like(n_scr)
        m_scr[...] = jnp.zeros_like(m_scr)

    lane = lax.broadcasted_iota(jnp.int32, (L, LANES), 1)
    r_i = lax.broadcasted_iota(jnp.int32, (L, L), 0)
    c_i = lax.broadcasted_iota(jnp.int32, (L, L), 1)
    causal = c_i <= r_i
    tril = causal.astype(BF16)
    triu = (r_i <= c_i).astype(BF16)

    zg = zg_ref[...] + bg_ref[...]
    gates = jnp.where(lane < HEADS, zg, _log_sigmoid(zg))
    gates_t = gates.T
    bcol_all = _tri_dot(tril, gates, True)
    brow_all = _tri_dot(triu, gates_t, False)

    for h in range(HEADS):
        b_col = bcol_all[:, HEADS + h:HEADS + h + 1]
        b_row = brow_all[HEADS + h:HEADS + h + 1, :]
        i_col = gates[:, h:h + 1]
        i_row = gates_t[h:h + 1, :]
        b_end = b_col[L - 1:L, :]
        m0 = m_scr[h:h + 1, 0:1]
        n0 = n_scr[h:h + 1, :]
        c0 = c_scr[h]

        q = q_ref[:, h * DK:(h + 1) * DK]
        k = k_ref[:, h * DK:(h + 1) * DK] * (DK ** -0.5)
        v = v_ref[:, h * DV:(h + 1) * DV]
        qb, vb = q.astype(BF16), v.astype(BF16)

        dmat = jnp.where(causal, b_col - b_row + i_row, NEG_BIG)
        inter = b_col + m0
        m_t = jnp.maximum(inter, jnp.max(dmat, axis=1, keepdims=True))
        w_int = jnp.exp(inter - m_t)
        p = jnp.where(causal, jnp.exp(dmat - m_t), 0.0)
        s = lax.dot_general(qb, k.astype(BF16), (((1,), (1,)), ((), ())),
                            preferred_element_type=F32) * p
        num = (jnp.dot(s.astype(BF16), vb, preferred_element_type=F32)
               + w_int * jnp.dot(qb, c0.astype(BF16), preferred_element_type=F32))
        den = jnp.sum(s, axis=1, keepdims=True) + w_int * jnp.sum(q * n0, axis=1, keepdims=True)
        hh = num / jnp.maximum(jnp.abs(den), jnp.exp(-m_t))
        hn = hh * lax.rsqrt(jnp.mean(hh * hh, axis=-1, keepdims=True) + EPS)
        hn = hn * gn_ref[:, h * DV:(h + 1) * DV]
        y = jax.nn.sigmoid(o_ref[:, h * DV:(h + 1) * DV]) * hn
        y_ref[:, h * DV:(h + 1) * DV] = y.astype(BF16)

        g_col = b_end - b_col + i_col
        g_row = b_end - b_row + i_row
        m_new = jnp.maximum(b_end + m0, jnp.max(g_row, axis=1, keepdims=True))
        decay = jnp.exp(b_end + m0 - m_new)
        kw = k * jnp.exp(g_col - m_new)
        c_new = decay * c0 + lax.dot_general(kw.astype(BF16), vb, (((0,), (0,)), ((), ())),
                                             preferred_element_type=F32)
        n_new = decay * n0 + jnp.sum(kw, axis=0, keepdims=True)
        c_scr[h] = c_new
        n_scr[h:h + 1, :] = n_new
        m_scr[h:h + 1, :] = jnp.broadcast_to(m_new, (1, LANES))
        c_out[0, h] = c_new
        n_out[0, h:h + 1, :] = n_new
        m_out[0, h:h + 1, :] = jnp.broadcast_to(m_new, (1, LANES))


def _mlstm_prompt(z, zg, b_gates, g_norm, batch, seq):
    nc = seq // CHUNK
    n_tok = batch * seq
    qw, vw = HEADS * DK, HEADS * DV
    return pl.pallas_call(
        _mlstm_prompt_kernel,
        grid=(batch, nc),
        in_specs=[
            pl.BlockSpec((CHUNK, qw), lambda b, c: (b * nc + c, COL_Q // qw)),
            pl.BlockSpec((CHUNK, qw), lambda b, c: (b * nc + c, COL_K // qw)),
            pl.BlockSpec((CHUNK, vw), lambda b, c: (b * nc + c, COL_V // vw)),
            pl.BlockSpec((CHUNK, vw), lambda b, c: (b * nc + c, COL_O // vw)),
            pl.BlockSpec((CHUNK, LANES), lambda b, c: (b * nc + c, 0)),
            pl.BlockSpec((1, LANES), lambda b, c: (0, 0)),
            pl.BlockSpec((1, vw), lambda b, c: (0, 0)),
        ],
        out_specs=[
            pl.BlockSpec((CHUNK, vw), lambda b, c: (b * nc + c, 0)),
            pl.BlockSpec((1, HEADS, DK, DV), lambda b, c: (b, 0, 0, 0)),
            pl.BlockSpec((1, HEADS, DK), lambda b, c: (b, 0, 0)),
            pl.BlockSpec((1, HEADS, LANES), lambda b, c: (b, 0, 0)),
        ],
        out_shape=[
            jax.ShapeDtypeStruct((n_tok, vw), BF16),
            jax.ShapeDtypeStruct((batch, HEADS, DK, DV), F32),
            jax.ShapeDtypeStruct((batch, HEADS, DK), F32),
            jax.ShapeDtypeStruct((batch, HEADS, LANES), F32),
        ],
        scratch_shapes=[
            pltpu.VMEM((HEADS, DK, DV), F32),
            pltpu.VMEM((8, DK), F32),
            pltpu.VMEM((8, LANES), F32),
        ],
        compiler_params=_cparams(("parallel", "arbitrary")),
        name="mlstm_prompt",
    )(z, z, z, z, zg, b_gates, g_norm)


def _mixer_sample_kernel(zb_ref, zc_ref, zh_ref, q_ref, k_ref, v_ref, o_ref, zg_ref,
                         st_ref, c_ref, n_ref, m_ref, wc_ref, gc_ref, bg_ref, gn_ref,
                         yc_ref, ym_ref, st_out, c_out, n_out, m_out):
    sb = zb_ref.shape[0]
    u = zc_ref[...] * zh_ref[...]
    s0 = st_ref[:, 0:CONV_DIM]
    s1 = st_ref[:, CONV_DIM:2 * CONV_DIM]
    wc = wc_ref[...]
    conv = wc[0:1] * s0 + wc[1:2] * s1 + wc[2:3] * u
    yc_ref[...] = _rms(zb_ref[...] * conv, gc_ref[...]).astype(BF16)
    st_out[:, 0:CONV_DIM] = s1
    st_out[:, CONV_DIM:2 * CONV_DIM] = u

    lane_row = lax.broadcasted_iota(jnp.int32, (1, LANES), 1)
    eye = (lax.broadcasted_iota(jnp.int32, (DK, DK), 0)
           == lax.broadcasted_iota(jnp.int32, (DK, DK), 1)).astype(F32)

    def to_col(row):
        return jnp.sum(eye * row, axis=1, keepdims=True)

    def pick(row, j):
        return jnp.sum(jnp.where(lane_row == j, row, 0.0), axis=1, keepdims=True)

    def per_sample(s, carry):
        zg = zg_ref[s] + bg_ref[...]
        lf_all = _log_sigmoid(zg)
        m_row = m_ref[s]
        m_new_row = jnp.zeros((1, LANES), F32)
        for h in range(HEADS):
            q = q_ref[s, :, h * DK:(h + 1) * DK]
            k = k_ref[s, :, h * DK:(h + 1) * DK] * (DK ** -0.5)
            v = v_ref[s, :, h * DV:(h + 1) * DV]
            o = o_ref[s, :, h * DV:(h + 1) * DV]
            i_pre = pick(zg, h)
            lf = pick(lf_all, HEADS + h)
            m0 = pick(m_row, h)
            c0 = c_ref[s, h]
            n0 = n_ref[s, h:h + 1, :]
            inter = lf + m0
            m_t = jnp.maximum(inter, i_pre)
            w_int = jnp.exp(inter - m_t)
            p = jnp.exp(i_pre - m_t)
            sc = jnp.sum(q * k, axis=1, keepdims=True) * p
            qc = jnp.sum(to_col(q) * c0, axis=0, keepdims=True)
            num = sc * v + w_int * qc
            den = sc + w_int * jnp.sum(q * n0, axis=1, keepdims=True)
            hh = num / jnp.maximum(jnp.abs(den), jnp.exp(-m_t))
            hn = hh * lax.rsqrt(jnp.mean(hh * hh, axis=-1, keepdims=True) + EPS)
            hn = hn * gn_ref[:, h * DV:(h + 1) * DV]
            ym_ref[s, :, h * DV:(h + 1) * DV] = jax.nn.sigmoid(o) * hn
            c_out[s, h] = w_int * c0 + (p * to_col(k)) * v
            n_out[s, h:h + 1, :] = w_int * n0 + p * k
            m_new_row = jnp.where(lane_row == h, m_t, m_new_row)
        m_out[s] = m_new_row
        return carry

    lax.fori_loop(0, sb, per_sample, 0)


def _mixer_sample(z, zs3, zg3, st, c_s, n_s, m3, w_conv, g_conv, b_gates, g_norm, row0, n_dec, sb):
    cb, qw, vw = CONV_DIM, HEADS * DK, HEADS * DV
    r0 = row0 // sb
    row = lambda i: r0 + i
    return pl.pallas_call(
        _mixer_sample_kernel,
        grid=(n_dec // sb,),
        in_specs=[
            pl.BlockSpec((sb, cb), lambda i: (row(i), COL_B // cb)),
            pl.BlockSpec((sb, cb), lambda i: (row(i), COL_C // cb)),
            pl.BlockSpec((sb, cb), lambda i: (row(i), COL_H // cb)),
            pl.BlockSpec((sb, 1, qw), lambda i: (i, 0, COL_Q // qw)),
            pl.BlockSpec((sb, 1, qw), lambda i: (i, 0, COL_K // qw)),
            pl.BlockSpec((sb, 1, vw), lambda i: (i, 0, COL_V // vw)),
            pl.BlockSpec((sb, 1, vw), lambda i: (i, 0, COL_O // vw)),
            pl.BlockSpec((sb, 1, LANES), lambda i: (i, 0, 0)),
            pl.BlockSpec((sb, 2 * cb), lambda i: (i, 0)),
            pl.BlockSpec((sb, HEADS, DK, DV), lambda i: (i, 0, 0, 0)),
            pl.BlockSpec((sb, HEADS, DK), lambda i: (i, 0, 0)),
            pl.BlockSpec((sb, 1, LANES), lambda i: (i, 0, 0)),
            pl.BlockSpec((8, cb), lambda i: (0, 0)),
            pl.BlockSpec((1, cb), lambda i: (0, 0)),
            pl.BlockSpec((1, LANES), lambda i: (0, 0)),
            pl.BlockSpec((1, vw), lambda i: (0, 0)),
        ],
        out_specs=[
            pl.BlockSpec((sb, cb), lambda i: (i, 0)),
            pl.BlockSpec((sb, 1, vw), lambda i: (i, 0, 0)),
            pl.BlockSpec((sb, 2 * cb), lambda i: (i, 0)),
            pl.BlockSpec((sb, HEADS, DK, DV), lambda i: (i, 0, 0, 0)),
            pl.BlockSpec((sb, HEADS, DK), lambda i: (i, 0, 0)),
            pl.BlockSpec((sb, 1, LANES), lambda i: (i, 0, 0)),
        ],
        out_shape=[
            jax.ShapeDtypeStruct((n_dec, cb), BF16),
            jax.ShapeDtypeStruct((n_dec, 1, vw), F32),
            jax.ShapeDtypeStruct((n_dec, 2 * cb), F32),
            jax.ShapeDtypeStruct((n_dec, HEADS, DK, DV), F32),
            jax.ShapeDtypeStruct((n_dec, HEADS, DK), F32),
            jax.ShapeDtypeStruct((n_dec, 1, LANES), F32),
        ],
        compiler_params=_cparams(("arbitrary",)),
        name="mixer_sample",
    )(z, z, z, zs3, zs3, zs3, zs3, zg3, st, c_s, n_s, m3, w_conv, g_conv, b_gates, g_norm)


def _outproj_router_kernel(yc_ref, ym_ref, x_ref, wo1_ref, wo2_ref, g_ref, wr_hi_ref, wr_lo_ref, br_ref,
                           cin_ref, h1_ref, xn_ref, e_ref, gate_ref, rank_ref, cnt_ref, carry):
    tm = x_ref.shape[0]

    @pl.when(pl.program_id(0) == 0)
    def _():
        carry[...] = cin_ref[...]

    mix = (jnp.dot(yc_ref[...], wo1_ref[...], preferred_element_type=F32)
           + jnp.dot(ym_ref[...], wo2_ref[...], preferred_element_type=F32))
    h1 = x_ref[...] + mix
    h1_ref[...] = h1
    xn = _rms(h1, g_ref[...])
    xn_ref[...] = xn

    xh = xn.astype(BF16)
    xl = (xn - xh.astype(F32)).astype(BF16)
    logits = (jnp.dot(xh, wr_hi_ref[...], preferred_element_type=F32)
              + jnp.dot(xl, wr_hi_ref[...], preferred_element_type=F32)
              + jnp.dot(xh, wr_lo_ref[...], preferred_element_type=F32)) + br_ref[...]
    lane = lax.broadcasted_iota(jnp.int32, (tm, LANES), 1)
    lane_f = lane.astype(F32)
    work = jnp.where(lane < N_EXPERTS, logits, NEG_BIG)

    tops, idxs = [], []
    chosen = jnp.zeros((tm, LANES), F32)
    for _ in range(TOP_K):
        mx = jnp.max(work, axis=1, keepdims=True)
        idx = jnp.min(jnp.where(work == mx, lane_f, float(LANES)), axis=1, keepdims=True)
        sel = lane_f == idx
        tops.append(mx)
        idxs.append(idx)
        chosen = jnp.where(sel, 1.0, chosen)
        work = jnp.where(sel, NEG_BIG, work)

    exps = [jnp.exp(t - tops[0]) for t in tops]
    denom = exps[0] + exps[1] + exps[2] + exps[3]

    r_i = lax.broadcasted_iota(jnp.int32, (tm, tm), 0)
    c_i = lax.broadcasted_iota(jnp.int32, (tm, tm), 1)
    before = jnp.dot((c_i < r_i).astype(BF16), chosen.astype(BF16),
                     preferred_element_type=F32) + carry[...]

    e_out = jnp.zeros((tm, LANES), F32)
    g_out = jnp.zeros((tm, LANES), F32)
    r_out = jnp.zeros((tm, LANES), F32)
    for kk in range(TOP_K):
        rank = jnp.sum(jnp.where(lane_f == idxs[kk], before, 0.0), axis=1, keepdims=True)
        e_out = jnp.where(lane == kk, idxs[kk], e_out)
        g_out = jnp.where(lane == kk, exps[kk] / denom, g_out)
        r_out = jnp.where(lane == kk, rank, r_out)
    gate_ref[...] = g_out
    for c in range(tm // LANES):
        rows = slice(c * LANES, (c + 1) * LANES)
        e_ref[:, rows] = e_out[rows].T[0:8].astype(jnp.int32)
        rank_ref[:, rows] = r_out[rows].T[0:8].astype(jnp.int32)

    new_carry = carry[...] + jnp.sum(chosen, axis=0, keepdims=True)
    carry[...] = new_carry
    cnt_ref[...] = new_carry


def _outproj_router(y_conv, y_mlstm, x, wo1, wo2, g_ffn, wr_hi, wr_lo, b_router, cnt_in, tm):
    n_tok = x.shape[0]
    full = lambda shape: pl.BlockSpec(shape, lambda i: (0,) * len(shape))
    tile = lambda w: pl.BlockSpec((tm, w), lambda i: (i, 0))
    choice_major = pl.BlockSpec((8, tm), lambda i: (0, i))
    return pl.pallas_call(
        _outproj_router_kernel,
        grid=(n_tok // tm,),
        in_specs=[
            tile(CONV_DIM), tile(HEADS * DV), tile(D_MODEL),
            full((CONV_DIM, D_MODEL)), full((HEADS * DV, D_MODEL)), full((1, D_MODEL)),
            full((D_MODEL, LANES)), full((D_MODEL, LANES)), full((1, LANES)), full((1, LANES)),
        ],
        out_specs=[tile(D_MODEL), tile(D_MODEL), choice_major, tile(LANES), choice_major, full((1, LANES))],
        out_shape=[
            jax.ShapeDtypeStruct((n_tok, D_MODEL), F32),
            jax.ShapeDtypeStruct((n_tok, D_MODEL), F32),
            jax.ShapeDtypeStruct((8, n_tok), jnp.int32),
            jax.ShapeDtypeStruct((n_tok, LANES), F32),
            jax.ShapeDtypeStruct((8, n_tok), jnp.int32),
            jax.ShapeDtypeStruct((1, LANES), F32),
        ],
        scratch_shapes=[pltpu.VMEM((1, LANES), F32)],
        compiler_params=_cparams(("arbitrary",)),
        name="outproj_router",
    )(y_conv, y_mlstm, x, wo1, wo2, g_ffn, wr_hi, wr_lo, b_router, cnt_in)


def _zero_fill_rows(dst, zero_scr, sem, start, count, act):
    low = count & 7
    for t in range(7):
        @pl.when(t < low)
        def _(t=t):
            act(pltpu.make_async_copy(zero_scr.at[pl.ds(0, 1)], dst.at[pl.ds(start + t, 1)], sem))

    off = start + low
    bit = 8
    while bit < SUB:
        take = count & bit

        @pl.when(take != 0)
        def _(off=off, bit=bit):
            row = pl.multiple_of(off, 8)
            act(pltpu.make_async_copy(zero_scr.at[pl.ds(0, bit)], dst.at[pl.ds(row, bit)], sem))

        off = off + take
        bit *= 2


def _zero_fill_chunks(dst, zero_scr, sem, start, act):
    def body(j, c):
        row = pl.multiple_of(start + j * SUB, SUB)
        act(pltpu.make_async_copy(zero_scr, dst.at[pl.ds(row, SUB)], sem))
        return c

    lax.fori_loop(0, (dst.shape[0] - start) // SUB, body, 0)


def _dispatch_kernel(dest_ref, cnt_ref, pstart_ref, xn_ref, xn_s_ref, xs_out, zero_scr, sem, zsem):
    tm = xn_ref.shape[0]
    step = pl.program_id(0)
    last = pl.num_programs(0) - 1

    def fill(act):
        for e in range(N_EXPERTS):
            c = cnt_ref[e]
            _zero_fill_rows(xs_out, zero_scr, zsem, pstart_ref[e] + c, (SUB - c % SUB) % SUB, act)
        _zero_fill_chunks(xs_out, zero_scr, zsem, pstart_ref[N_EXPERTS], act)

    @pl.when(step == 0)
    def _():
        zero_scr[...] = jnp.zeros_like(zero_scr)
        fill(lambda cp: cp.start())

    def scatter_rows(src, base):
        rows = src.shape[0]
        n_all = dest_ref.shape[0] // TOP_K

        def issue(r, c):
            for kk in range(TOP_K):
                d = dest_ref[kk * n_all + base + r]
                pltpu.make_async_copy(src.at[pl.ds(r, 1)], xs_out.at[pl.ds(d, 1)], sem).start()
            return c

        lax.fori_loop(0, rows, issue, 0, unroll=ISSUE_UNROLL)
        for _ in range(TOP_K):
            pltpu.make_async_copy(src, xs_out.at[pl.ds(0, rows)], sem).wait()

    scatter_rows(xn_ref, step * tm)

    @pl.when(step == last)
    def _():
        scatter_rows(xn_s_ref, pl.num_programs(0) * tm)

    @pl.when(step == 0)
    def _():
        fill(lambda cp: cp.wait())


def _dispatch(dest_flat, counts, pstart, xn_p, xn_s, n_slots, tm):
    n_prompt = xn_p.shape[0]
    return pl.pallas_call(
        _dispatch_kernel,
        grid_spec=pltpu.PrefetchScalarGridSpec(
            num_scalar_prefetch=3,
            grid=(n_prompt // tm,),
            in_specs=[pl.BlockSpec((tm, D_MODEL), lambda i, d, c, p: (i, 0)),
                      pl.BlockSpec(xn_s.shape, lambda i, d, c, p: (0, 0))],
            out_specs=pl.BlockSpec(memory_space=pl.ANY),
            scratch_shapes=[pltpu.VMEM((SUB, D_MODEL), F32), pltpu.SemaphoreType.DMA(()),
                            pltpu.SemaphoreType.DMA(())],
        ),
        out_shape=jax.ShapeDtypeStruct((n_slots, D_MODEL), F32),
        compiler_params=_cparams(("arbitrary",)),
        name="dispatch",
    )(dest_flat, counts, pstart, xn_p, xn_s)


def _expert_kernel(se_ref, st_ref, sn_ref, end_ref, xs_ref, wg_ref, wu_ref, bg_ref, bu_ref, wd_ref, bd_ref,
                   y_ref, x_scr, xb_scr, acc, zero_scr, sem, zsem, xsem):
    i, f = pl.program_id(0), pl.program_id(1)
    nf = pl.num_programs(1)
    tmb = acc.shape[0]
    n = sn_ref[i]
    start = st_ref[i]

    def x_copy(j):
        row = pl.multiple_of(st_ref[j], SUB)
        return pltpu.make_async_copy(xs_ref.at[pl.ds(row, tmb)], x_scr, xsem)

    @pl.when((i == 0) & (f == 0))
    def _():
        zero_scr[...] = jnp.zeros_like(zero_scr)
        _zero_fill_chunks(y_ref, zero_scr, zsem, end_ref[0], lambda cp: cp.start())
        _zero_fill_chunks(y_ref, zero_scr, zsem, end_ref[0], lambda cp: cp.wait())

    kc = D_MODEL // MOE_SPLIT

    def y_copy(o, mc, c):
        row = pl.multiple_of(start + o, SUB)
        cols = pl.ds(c * kc, kc)
        return pltpu.make_async_copy(acc.at[pl.ds(o, mc), cols], y_ref.at[pl.ds(row, mc), cols], sem)

    def for_each_sub_block(fn):
        kmax = tmb // SUB
        half = SUB // 2
        nh = (n + half - 1) // half
        for kh in range(2 * kmax, 2 * kmax - 4, -1):
            pl.when(nh == kh)(functools.partial(fn, 0, kh * half, -(-kh // 2) * SUB))
        nb = (n + SUB - 1) // SUB
        bits = [b for b in (64, 32, 16, 8, 4, 2, 1) if b <= kmax - 2]
        small = nh <= 2 * kmax - 4
        for b in bits:
            higher = [h for h in bits if h > b]
            for mask in range(1 << len(higher)):
                above = sum(h for j, h in enumerate(higher) if (mask >> j) & 1)
                if above + b > kmax - 2:
                    continue
                cond = small & ((nb & b) != 0) & ((nb & sum(higher)) == above)
                pl.when(cond)(functools.partial(fn, above * SUB, b * SUB, b * SUB))

    @pl.when(n > 0)
    def _():
        @pl.when(f == 0)
        def _():
            @pl.when(i == 0)
            def _():
                x_copy(0).start()

            x_copy(i).wait()
            xb_scr[...] = x_scr[...].astype(BF16)
            nxt = jnp.minimum(i + 1, pl.num_programs(0) - 1)

            @pl.when((i + 1 < pl.num_programs(0)) & (sn_ref[nxt] > 0))
            def _():
                x_copy(nxt).start()

            acc[...] = jnp.broadcast_to(bd_ref[0], acc.shape)

        def sub_block(o, m, mc):
            g = bg_ref[0]
            u = bu_ref[0]
            for c in range(MOE_SPLIT):
                xb = xb_scr[o:o + m, c * kc:(c + 1) * kc]
                g = g + jnp.dot(xb, wg_ref[0, c * kc:(c + 1) * kc, :].astype(BF16), preferred_element_type=F32)
                u = u + jnp.dot(xb, wu_ref[0, c * kc:(c + 1) * kc, :].astype(BF16), preferred_element_type=F32)
            g = jnp.minimum(g, SWIGLU_LIMIT)
            u = jnp.clip(u, -SWIGLU_LIMIT, SWIGLU_LIMIT)
            act = (g * jax.nn.sigmoid(SWIGLU_ALPHA * g) * (u + 1.0)).astype(BF16)
            for c in range(MOE_SPLIT):
                cols = slice(c * kc, (c + 1) * kc)
                acc[o:o + m, cols] += jnp.dot(act, wd_ref[0, :, cols].astype(BF16), preferred_element_type=F32)

                @pl.when(f == nf - 1)
                def _(c=c):
                    y_copy(o, mc, c).start()

        for_each_sub_block(sub_block)

        @pl.when(f == nf - 1)
        def _():
            def drain(o, m, mc):
                for c in range(MOE_SPLIT):
                    y_copy(o, mc, c).wait()

            for_each_sub_block(drain)


def _experts(sb_e, sb_start, sb_n, end, xs, w_gate_up, b_gate_up, w_down, b_down, nsb, tmb, tf):
    n_slots = xs.shape[0]
    nf = D_FF // tf

    def fidx(i, f, sn):
        return jnp.where(sn[i] > 0, f, nf - 1)

    return pl.pallas_call(
        _expert_kernel,
        grid_spec=pltpu.PrefetchScalarGridSpec(
            num_scalar_prefetch=4,
            grid=(nsb, nf),
            in_specs=[
                pl.BlockSpec(memory_space=pl.ANY),
                pl.BlockSpec((1, D_MODEL, tf), lambda i, f, se, st, sn, en: (se[i], 0, fidx(i, f, sn))),
                pl.BlockSpec((1, D_MODEL, tf), lambda i, f, se, st, sn, en: (se[i], 0, nf + fidx(i, f, sn))),
                pl.BlockSpec((1, 1, tf), lambda i, f, se, st, sn, en: (se[i], 0, fidx(i, f, sn))),
                pl.BlockSpec((1, 1, tf), lambda i, f, se, st, sn, en: (se[i], 0, nf + fidx(i, f, sn))),
                pl.BlockSpec((1, tf, D_MODEL), lambda i, f, se, st, sn, en: (se[i], fidx(i, f, sn), 0)),
                pl.BlockSpec((1, 1, D_MODEL), lambda i, f, se, st, sn, en: (se[i], 0, 0)),
            ],
            out_specs=pl.BlockSpec(memory_space=pl.ANY),
            scratch_shapes=[pltpu.VMEM((tmb, D_MODEL), F32), pltpu.VMEM((tmb, D_MODEL), BF16),
                            pltpu.VMEM((tmb, D_MODEL), F32), pltpu.VMEM((SUB, D_MODEL), F32),
                            pltpu.SemaphoreType.DMA(()), pltpu.SemaphoreType.DMA(()),
                            pltpu.SemaphoreType.DMA(())],
        ),
        out_shape=jax.ShapeDtypeStruct((n_slots, D_MODEL), F32),
        compiler_params=_cparams(("arbitrary", "arbitrary")),
        name="experts",
    )(sb_e, sb_start, sb_n, end, xs, w_gate_up, w_gate_up, b_gate_up, b_gate_up, w_down, b_down)


def _combine_kernel(dest_ref, h1_ref, gate_ref, p_ref, yb_ref, wpg_ref, wpp_ref, gp_ref, gf_ref,
                    out_ref, gbuf, sem):
    tm = h1_ref.shape[0]
    step = pl.program_id(0)
    slot = step % 2

    def gather_tile(t, s):
        n_all = dest_ref.shape[0] // TOP_K

        def issue(r, c):
            for kk in range(TOP_K):
                d = dest_ref[kk * n_all + t * tm + r]
                pltpu.make_async_copy(yb_ref.at[pl.ds(d, 1)], gbuf.at[s, kk, pl.ds(r, 1)], sem.at[s]).start()
            return c

        lax.fori_loop(0, tm, issue, 0, unroll=ISSUE_UNROLL)

    @pl.when(step == 0)
    def _():
        gather_tile(0, 0)

    @pl.when(step + 1 < pl.num_programs(0))
    def _():
        gather_tile(step + 1, 1 - slot)

    for kk in range(TOP_K):
        pltpu.make_async_copy(yb_ref.at[pl.ds(0, tm)], gbuf.at[slot, kk], sem.at[slot]).wait()

    gate = gate_ref[...]
    moe = gate[:, 0:1] * gbuf[slot, 0]
    for kk in range(1, TOP_K):
        moe = moe + gate[:, kk:kk + 1] * gbuf[slot, kk]
    h2 = h1_ref[...] + moe
    a = _rms(h2, gp_ref[...]).astype(BF16)
    pg = jax.nn.sigmoid(jnp.dot(a, wpg_ref[...], preferred_element_type=F32))
    pp = jnp.dot(p_ref[...].astype(BF16), wpp_ref[...], preferred_element_type=F32)
    h3 = h2 + pg * pp
    out_ref[...] = _rms(h3, gf_ref[...])


def _combine(dest_flat, h1, gate, p, yb, wpg, wpp, g_ple, g_final, tm):
    n_tok = h1.shape[0]
    full = lambda shape: pl.BlockSpec(shape, lambda i, d: (0,) * len(shape))
    tile = lambda w: pl.BlockSpec((tm, w), lambda i, d: (i, 0))
    return pl.pallas_call(
        _combine_kernel,
        grid_spec=pltpu.PrefetchScalarGridSpec(
            num_scalar_prefetch=1,
            grid=(n_tok // tm,),
            in_specs=[
                tile(D_MODEL), tile(LANES), tile(PLE_DIM),
                pl.BlockSpec(memory_space=pl.ANY),
                full((D_MODEL, D_MODEL)), full((PLE_DIM, D_MODEL)), full((1, D_MODEL)), full((1, D_MODEL)),
            ],
            out_specs=tile(D_MODEL),
            scratch_shapes=[pltpu.VMEM((2, TOP_K, tm, D_MODEL), F32), pltpu.SemaphoreType.DMA((2,))],
        ),
        out_shape=jax.ShapeDtypeStruct((n_tok, D_MODEL), F32),
        compiler_params=_cparams(("arbitrary",)),
        name="combine",
    )(dest_flat, h1, gate, p, yb, wpg, wpp, g_ple, g_final)


def kernel(x_prompt, x_sample, state_conv, state_mlstm_C, state_mlstm_n, state_mlstm_m, p_prompt, p_sample,
           g_mix, w_in, b_gates, w_conv, g_conv_out, g_mlstm_out, w_out, g_ffn, w_router, b_router,
           w_gate_up, b_gate_up, w_down, b_down, g_ple, w_ple_gate, w_ple_proj, g_final):
    batch, seq, _ = x_prompt.shape
    n_dec = x_sample.shape[0]
    assert w_in.shape[0] == 1 and x_sample.shape[1] == 1
    n_prompt = batch * seq
    n_tok = n_prompt + n_dec

    x_p = x_prompt.reshape(n_prompt, D_MODEL)
    x_s = x_sample.reshape(n_dec, D_MODEL)

    row = lambda a: a.reshape(1, -1)
    pad_lanes = lambda a: jnp.pad(a, ((0, 0), (0, LANES - a.shape[1])))
    w_gates = pad_lanes(w_in[0, :, MAIN_COLS:]).astype(BF16)
    bg = pad_lanes(row(b_gates[0]))
    wc = jnp.pad(w_conv[0], ((0, 8 - w_conv.shape[1]), (0, 0)))
    g_norm = row(g_mlstm_out[0])

    a_p, zg_p = _norm_gates(x_p, row(g_mix[0]), w_gates, tm=1024)
    a_s, zg_s = _norm_gates(x_s, row(g_mix[0]), w_gates, tm=n_dec)
    z_p = _inproj(a_p, w_in, tm=1024, tn=1024)
    z_s = _inproj(a_s, w_in, tm=n_dec, tn=1024)

    yc_p, conv_p = _conv_prompt(z_p, wc, row(g_conv_out[0]), batch, seq, tt=512)
    ym_p, c_p, n_p, m_p = _mlstm_prompt(z_p, zg_p, bg, g_norm, batch, seq)
    yc_s, ym_s, conv_s, c_s, n_s, m_s = _mixer_sample(
        z_s, z_s.reshape(n_dec, 1, MAIN_COLS), zg_s.reshape(n_dec, 1, LANES),
        state_conv[0].reshape(n_dec, 2 * CONV_DIM), state_mlstm_C[0], state_mlstm_n[0],
        pad_lanes(state_mlstm_m[0]).reshape(n_dec, 1, LANES), wc, row(g_conv_out[0]), bg, g_norm,
        row0=0, n_dec=n_dec, sb=16)
    m_s = m_s[:, 0, :HEADS]
    ym_s = ym_s.reshape(n_dec, HEADS * DV).astype(BF16)

    wo = w_out[0].astype(BF16)
    wr = pad_lanes(w_router[0])
    wr_hi = wr.astype(BF16)
    wr_lo = (wr - wr_hi.astype(F32)).astype(BF16)
    router_args = (wo[:CONV_DIM], wo[CONV_DIM:], row(g_ffn[0]), wr_hi, wr_lo, pad_lanes(row(b_router[0])))
    h1_p, xn_p, e_p, gate_p, rank_p, cnt_p = _outproj_router(
        yc_p, ym_p, x_p, *router_args, jnp.zeros((1, LANES), F32), tm=512)
    h1_s, xn_s, e_s, gate_s, rank_s, cnt = _outproj_router(
        yc_s, ym_s, x_s, *router_args, cnt_p, tm=n_dec)

    n_assign = n_tok * TOP_K
    tmb = MOE_TMB
    nsb = n_assign // tmb + N_EXPERTS
    n_slots = -(-(n_assign + N_EXPERTS * (SUB - 1)) // SUB) * SUB + tmb
    counts = cnt[0, :N_EXPERTS].astype(jnp.int32)
    padded = (counts + SUB - 1) // SUB * SUB
    pend = jnp.cumsum(padded)
    pstart = jnp.concatenate([jnp.zeros((1,), jnp.int32), pend]).astype(jnp.int32)
    sb_per_e = (counts + tmb - 1) // tmb
    sb_cum = jnp.cumsum(sb_per_e)
    n_used = sb_cum[-1]
    sb = jnp.minimum(jnp.arange(nsb, dtype=jnp.int32), n_used - 1)
    sb_e = jnp.minimum(jnp.sum((sb_cum[None, :] <= sb[:, None]).astype(jnp.int32), axis=1), N_EXPERTS - 1)
    sb_j = sb - (sb_cum - sb_per_e)[sb_e]
    sb_start = (pstart[sb_e] + sb_j * tmb).astype(jnp.int32)
    sb_n = jnp.where(jnp.arange(nsb) < n_used, jnp.clip(counts[sb_e] - sb_j * tmb, 0, tmb), 0).astype(jnp.int32)
    expert_ids = jnp.arange(N_EXPERTS, dtype=jnp.int32)[:, None, None]

    def slot_of(e, r):
        first = jnp.sum(jnp.where(e[None, :TOP_K] == expert_ids, pstart[:N_EXPERTS, None, None], 0), axis=0)
        return (first + r[:TOP_K]).astype(jnp.int32)
    dest_p = slot_of(e_p, rank_p)
    dest_s = slot_of(e_s, rank_s)

    xs = _dispatch(jnp.concatenate([dest_p, dest_s], axis=1).reshape(-1), counts, pstart, xn_p, xn_s,
                   n_slots, tm=512)
    dest_p = dest_p.reshape(-1)
    dest_s = dest_s.reshape(-1)
    yb = _experts(sb_e.astype(jnp.int32), sb_start, sb_n, pstart[N_EXPERTS:], xs, w_gate_up[0],
                  b_gate_up[0].reshape(N_EXPERTS, 1, 2 * D_FF), w_down[0],
                  b_down[0].reshape(N_EXPERTS, 1, D_MODEL), nsb, tmb, MOE_TF)

    ple_args = (w_ple_gate[0].astype(BF16), w_ple_proj[0].astype(BF16), row(g_ple[0]), row(g_final))
    out_p = _combine(dest_p, h1_p, gate_p, p_prompt[0].reshape(n_prompt, PLE_DIM), yb, *ple_args, tm=256)
    out_s = _combine(dest_s, h1_s, gate_s, p_sample[0].reshape(n_dec, PLE_DIM), yb, *ple_args, tm=n_dec)

    y_prompt = out_p.reshape(batch, seq, D_MODEL)
    y_sample = out_s.reshape(n_dec, 1, D_MODEL)
    return (y_prompt, y_sample,
            conv_p[None], c_p[None], n_p[None], m_p[None, :, :, 0],
            conv_s.reshape(1, n_dec, 2, CONV_DIM), c_s[None], n_s[None], m_s[None])
```

```python
import functools

import jax
import jax.numpy as jnp
from jax import lax
from jax.experimental import pallas as pl
from jax.experimental.pallas import tpu as pltpu

F32 = jnp.float32
BF16 = jnp.bfloat16

D_MODEL = 2048
CONV_DIM = 1024
HEADS = 4
DK = 128
DV = 256
CHUNK = 128
N_EXPERTS = 32
TOP_K = 4
D_FF = 2048
PLE_DIM = 256
SWIGLU_LIMIT = 7.0
SWIGLU_ALPHA = 1.702
EPS = 1e-6
LANES = 128
NEG_BIG = -1e30
MAIN_COLS = 3 * CONV_DIM + HEADS * (2 * DK + 2 * DV)

COL_B, COL_C, COL_H = 0, CONV_DIM, 2 * CONV_DIM
COL_Q = 3 * CONV_DIM
COL_K = COL_Q + HEADS * DK
COL_V = COL_K + HEADS * DK
COL_O = COL_V + HEADS * DV

VMEM_LIMIT = 56 * 1024 * 1024

SUB = 128
MOE_TMB = 9 * SUB
MOE_TF = 512
MOE_SPLIT = 4
ISSUE_UNROLL = 8


def _cparams(sem, vmem=VMEM_LIMIT):
    return pltpu.CompilerParams(dimension_semantics=sem, vmem_limit_bytes=vmem)


def _rms(x, g):
    return x * lax.rsqrt(jnp.mean(x * x, axis=-1, keepdims=True) + EPS) * g


def _split3(x):
    x1 = x.astype(BF16)
    r1 = x - x1.astype(F32)
    x2 = r1.astype(BF16)
    x3 = (r1 - x2.astype(F32)).astype(BF16)
    return x1, x2, x3


def _log_sigmoid(x):
    return jnp.minimum(x, 0.0) - jnp.log(1.0 + jnp.exp(-jnp.abs(x)))


def _norm_gates_kernel(x_ref, g_ref, wg_ref, a_ref, zg_ref):
    a = _rms(x_ref[...], g_ref[...]).astype(BF16)
    a_ref[...] = a
    zg_ref[...] = jnp.dot(a, wg_ref[...], preferred_element_type=F32)


def _norm_gates(x, g, w_gates, tm):
    n_tok = x.shape[0]
    return pl.pallas_call(
        _norm_gates_kernel,
        grid=(n_tok // tm,),
        in_specs=[
            pl.BlockSpec((tm, D_MODEL), lambda i: (i, 0)),
            pl.BlockSpec((1, D_MODEL), lambda i: (0, 0)),
            pl.BlockSpec((D_MODEL, LANES), lambda i: (0, 0)),
        ],
        out_specs=[
            pl.BlockSpec((tm, D_MODEL), lambda i: (i, 0)),
            pl.BlockSpec((tm, LANES), lambda i: (i, 0)),
        ],
        out_shape=[
            jax.ShapeDtypeStruct((n_tok, D_MODEL), BF16),
            jax.ShapeDtypeStruct((n_tok, LANES), F32),
        ],
        compiler_params=_cparams(("parallel",)),
        name="norm_gates",
    )(x, g, w_gates)


def _inproj_kernel(a_ref, w_ref, z_ref, w_scr):
    @pl.when(pl.program_id(1) == 0)
    def _():
        w_scr[...] = w_ref[0].astype(BF16)

    z_ref[...] = jnp.dot(a_ref[...], w_scr[...], preferred_element_type=F32)


def _inproj(a, w_in, tm, tn):
    n_tok = a.shape[0]
    return pl.pallas_call(
        _inproj_kernel,
        grid=(MAIN_COLS // tn, n_tok // tm),
        in_specs=[
            pl.BlockSpec((tm, D_MODEL), lambda j, i: (i, 0)),
            pl.BlockSpec((1, D_MODEL, tn), lambda j, i: (0, 0, j)),
        ],
        out_specs=pl.BlockSpec((tm, tn), lambda j, i: (i, j)),
        out_shape=jax.ShapeDtypeStruct((n_tok, MAIN_COLS), F32),
        scratch_shapes=[pltpu.VMEM((D_MODEL, tn), BF16)],
        compiler_params=_cparams(("arbitrary", "arbitrary")),
        name="inproj",
    )(a, w_in)


def _conv_chunk(zb, zc, zh, prev, wc, g):
    tt = zb.shape[0]
    u = zc * zh
    row = lax.broadcasted_iota(jnp.int32, (tt, 1), 0)
    u1 = jnp.where(row == 0, prev[7:8], pltpu.roll(u, 1, axis=0))
    u2 = jnp.where(row == 0, prev[6:7], jnp.where(row == 1, prev[7:8], pltpu.roll(u, 2, axis=0)))
    conv = wc[0:1] * u2 + wc[1:2] * u1 + wc[2:3] * u
    return _rms(zb * conv, g).astype(BF16), u


def _tri_dot(tri_bf16, x, tri_first):
    out = None
    for part in _split3(x):
        d = (jnp.dot(tri_bf16, part, preferred_element_type=F32) if tri_first
             else jnp.dot(part, tri_bf16, preferred_element_type=F32))
        out = d if out is None else out + d
    return out


def _mixer_prompt_kernel(zb_ref, zc_ref, zh_ref, q_ref, k_ref, v_ref, o_ref, zg_ref,
                         wc_ref, gc_ref, bg_ref, gn_ref,
                         yc_ref, y_ref, st_out, c_out, n_out, m_out, carry, c_scr, n_scr, m_scr):
    L = CHUNK

    @pl.when(pl.program_id(1) == 0)
    def _():
        carry[...] = jnp.zeros_like(carry)
        c_scr[...] = jnp.zeros_like(c_scr)
        n_scr[...] = jnp.zeros_like(n_scr)
        m_scr[...] = jnp.zeros_like(m_scr)

    lane = lax.broadcasted_iota(jnp.int32, (L, LANES), 1)
    r_i = lax.broadcasted_iota(jnp.int32, (L, L), 0)
    c_i = lax.broadcasted_iota(jnp.int32, (L, L), 1)
    causal = c_i <= r_i
    tril = causal.astype(BF16)
    triu = (r_i <= c_i).astype(BF16)

    for bi in range(q_ref.shape[0]):
        _mixer_prompt_chunk(bi, zb_ref, zc_ref, zh_ref, q_ref, k_ref, v_ref, o_ref, zg_ref,
                            wc_ref, gc_ref, bg_ref, gn_ref, yc_ref, y_ref, st_out, c_out, n_out, m_out,
                            carry, c_scr, n_scr, m_scr, lane, causal, tril, triu)


def _mixer_prompt_chunk(bi, zb_ref, zc_ref, zh_ref, q_ref, k_ref, v_ref, o_ref, zg_ref,
                        wc_ref, gc_ref, bg_ref, gn_ref, yc_ref, y_ref, st_out, c_out, n_out, m_out,
                        carry, c_scr, n_scr, m_scr, lane, causal, tril, triu):
    L = CHUNK
    yc, u = _conv_chunk(zb_ref[bi], zc_ref[bi], zh_ref[bi], carry[bi], wc_ref[...], gc_ref[...])
    yc_ref[bi] = yc
    carry[bi] = u[L - 8:L]
    st_out[bi] = u[L - 2:L]

    zg = zg_ref[bi] + bg_ref[...]
    gates = jnp.where(lane < HEADS, zg, _log_sigmoid(zg))
    gates_t = gates.T
    bcol_all = _tri_dot(tril, gates, True)
    brow_all = _tri_dot(triu, gates_t, False)

    for h in range(HEADS):
        b_col = bcol_all[:, HEADS + h:HEADS + h + 1]
        b_row = brow_all[HEADS + h:HEADS + h + 1, :]
        i_col = gates[:, h:h + 1]
        i_row = gates_t[h:h + 1, :]
        b_end = b_col[L - 1:L, :]
        m0 = m_scr[bi, h:h + 1, 0:1]
        n0 = n_scr[bi, h:h + 1, :]
        c0 = c_scr[bi, h]

        q = q_ref[bi, :, h * DK:(h + 1) * DK]
        k = k_ref[bi, :, h * DK:(h + 1) * DK] * (DK ** -0.5)
        v = v_ref[bi, :, h * DV:(h + 1) * DV]
        qb, vb = q.astype(BF16), v.astype(BF16)

        dmat = jnp.where(causal, b_col - b_row + i_row, NEG_BIG)
        inter = b_col + m0
        m_t = jnp.maximum(inter, jnp.max(dmat, axis=1, keepdims=True))
        w_int = jnp.exp(inter - m_t)
        p = jnp.where(causal, jnp.exp(dmat - m_t), 0.0)
        s = lax.dot_general(qb, k.astype(BF16), (((1,), (1,)), ((), ())),
                            preferred_element_type=F32) * p
        num = (jnp.dot(s.astype(BF16), vb, preferred_element_type=F32)
               + w_int * jnp.dot(qb, c0.astype(BF16), preferred_element_type=F32))
        den = jnp.sum(s, axis=1, keepdims=True) + w_int * jnp.sum(q * n0, axis=1, keepdims=True)
        hh = num / jnp.maximum(jnp.abs(den), jnp.exp(-m_t))
        hn = hh * lax.rsqrt(jnp.mean(hh * hh, axis=-1, keepdims=True) + EPS)
        hn = hn * gn_ref[:, h * DV:(h + 1) * DV]
        y = jax.nn.sigmoid(o_ref[bi, :, h * DV:(h + 1) * DV]) * hn
        y_ref[bi, :, h * DV:(h + 1) * DV] = y.astype(BF16)

        g_col = b_end - b_col + i_col
        g_row = b_end - b_row + i_row
        m_new = jnp.maximum(b_end + m0, jnp.max(g_row, axis=1, keepdims=True))
        decay = jnp.exp(b_end + m0 - m_new)
        kw = k * jnp.exp(g_col - m_new)
        c_new = decay * c0 + lax.dot_general(kw.astype(BF16), vb, (((0,), (0,)), ((), ())),
                                             preferred_element_type=F32)
        n_new = decay * n0 + jnp.sum(kw, axis=0, keepdims=True)
        c_scr[bi, h] = c_new
        n_scr[bi, h:h + 1, :] = n_new
        m_scr[bi, h:h + 1, :] = jnp.broadcast_to(m_new, (1, LANES))
        c_out[bi, h] = c_new
        n_out[bi, h:h + 1, :] = n_new
        m_out[bi, h:h + 1, :] = jnp.broadcast_to(m_new, (1, LANES))


def _mixer_prompt(z3, zg3, w_conv, g_conv, b_gates, g_norm, nbt):
    batch, seq, _ = z3.shape
    nc = seq // CHUNK
    cb, qw, vw = CONV_DIM, HEADS * DK, HEADS * DV
    zcol = lambda w, col: pl.BlockSpec((nbt, CHUNK, w), lambda b, c: (b, c, col // w))
    const = lambda shape: pl.BlockSpec(shape, lambda b, c: (0,) * len(shape))
    per_seq = lambda *tail: pl.BlockSpec((nbt,) + tail, lambda b, c: (b,) + (0,) * len(tail))
    return pl.pallas_call(
        _mixer_prompt_kernel,
        grid=(batch // nbt, nc),
        in_specs=[
            zcol(cb, COL_B), zcol(cb, COL_C), zcol(cb, COL_H),
            zcol(qw, COL_Q), zcol(qw, COL_K), zcol(vw, COL_V), zcol(vw, COL_O),
            pl.BlockSpec((nbt, CHUNK, LANES), lambda b, c: (b, c, 0)),
            const((8, cb)), const((1, cb)), const((1, LANES)), const((1, vw)),
        ],
        out_specs=[
            pl.BlockSpec((nbt, CHUNK, cb), lambda b, c: (b, c, 0)),
            pl.BlockSpec((nbt, CHUNK, vw), lambda b, c: (b, c, 0)),
            per_seq(2, cb), per_seq(HEADS, DK, DV), per_seq(HEADS, DK), per_seq(HEADS, LANES),
        ],
        out_shape=[
            jax.ShapeDtypeStruct((batch, seq, cb), BF16),
            jax.ShapeDtypeStruct((batch, seq, vw), BF16),
            jax.ShapeDtypeStruct((batch, 2, cb), F32),
            jax.ShapeDtypeStruct((batch, HEADS, DK, DV), F32),
            jax.ShapeDtypeStruct((batch, HEADS, DK), F32),
            jax.ShapeDtypeStruct((batch, HEADS, LANES), F32),
        ],
        scratch_shapes=[
            pltpu.VMEM((nbt, 8, cb), F32),
            pltpu.VMEM((nbt, HEADS, DK, DV), F32),
            pltpu.VMEM((nbt, 8, DK), F32),
            pltpu.VMEM((nbt, 8, LANES), F32),
        ],
        compiler_params=_cparams(("parallel", "arbitrary")),
        name="mixer_prompt",
    )(z3, z3, z3, z3, z3, z3, z3, zg3, w_conv, g_conv, b_gates, g_norm)


def _mixer_sample_kernel(zb_ref, zc_ref, zh_ref, q_ref, k_ref, v_ref, o_ref, zg_ref,
                         st_ref, c_ref, n_ref, m_ref, wc_ref, gc_ref, bg_ref, gn_ref,
                         yc_ref, ym_ref, st_out, c_out, n_out, m_out):
    sb = zb_ref.shape[0]
    u = zc_ref[...] * zh_ref[...]
    s0 = st_ref[:, 0:CONV_DIM]
    s1 = st_ref[:, CONV_DIM:2 * CONV_DIM]
    wc = wc_ref[...]
    conv = wc[0:1] * s0 + wc[1:2] * s1 + wc[2:3] * u
    yc_ref[...] = _rms(zb_ref[...] * conv, gc_ref[...]).astype(BF16)
    st_out[:, 0:CONV_DIM] = s1
    st_out[:, CONV_DIM:2 * CONV_DIM] = u

    lane_row = lax.broadcasted_iota(jnp.int32, (1, LANES), 1)
    eye = (lax.broadcasted_iota(jnp.int32, (DK, DK), 0)
           == lax.broadcasted_iota(jnp.int32, (DK, DK), 1)).astype(F32)

    def to_col(row):
        return jnp.sum(eye * row, axis=1, keepdims=True)

    def pick(row, j):
        return jnp.sum(jnp.where(lane_row == j, row, 0.0), axis=1, keepdims=True)

    def per_sample(s, carry):
        zg = zg_ref[s] + bg_ref[...]
        lf_all = _log_sigmoid(zg)
        m_row = m_ref[s]
        m_new_row = jnp.zeros((1, LANES), F32)
        for h in range(HEADS):
            q = q_ref[s, :, h * DK:(h + 1) * DK]
            k = k_ref[s, :, h * DK:(h + 1) * DK] * (DK ** -0.5)
            v = v_ref[s, :, h * DV:(h + 1) * DV]
            o = o_ref[s, :, h * DV:(h + 1) * DV]
            i_pre = pick(zg, h)
            lf = pick(lf_all, HEADS + h)
            m0 = pick(m_row, h)
            c0 = c_ref[s, h]
            n0 = n_ref[s, h:h + 1, :]
            inter = lf + m0
            m_t = jnp.maximum(inter, i_pre)
            w_int = jnp.exp(inter - m_t)
            p = jnp.exp(i_pre - m_t)
            sc = jnp.sum(q * k, axis=1, keepdims=True) * p
            qc = jnp.sum(to_col(q) * c0, axis=0, keepdims=True)
            num = sc * v + w_int * qc
            den = sc + w_int * jnp.sum(q * n0, axis=1, keepdims=True)
            hh = num / jnp.maximum(jnp.abs(den), jnp.exp(-m_t))
            hn = hh * lax.rsqrt(jnp.mean(hh * hh, axis=-1, keepdims=True) + EPS)
            hn = hn * gn_ref[:, h * DV:(h + 1) * DV]
            ym_ref[s, :, h * DV:(h + 1) * DV] = jax.nn.sigmoid(o) * hn
            c_out[s, h] = w_int * c0 + (p * to_col(k)) * v
            n_out[s, h:h + 1, :] = w_int * n0 + p * k
            m_new_row = jnp.where(lane_row == h, m_t, m_new_row)
        m_out[s] = m_new_row
        return carry

    lax.fori_loop(0, sb, per_sample, 0)


def _mixer_sample(z, zs3, zg3, st, c_s, n_s, m3, w_conv, g_conv, b_gates, g_norm, row0, n_dec, sb):
    cb, qw, vw = CONV_DIM, HEADS * DK, HEADS * DV
    r0 = row0 // sb
    row = lambda i: r0 + i
    return pl.pallas_call(
        _mixer_sample_kernel,
        grid=(n_dec // sb,),
        in_specs=[
            pl.BlockSpec((sb, cb), lambda i: (row(i), COL_B // cb)),
            pl.BlockSpec((sb, cb), lambda i: (row(i), COL_C // cb)),
            pl.BlockSpec((sb, cb), lambda i: (row(i), COL_H // cb)),
            pl.BlockSpec((sb, 1, qw), lambda i: (i, 0, COL_Q // qw)),
            pl.BlockSpec((sb, 1, qw), lambda i: (i, 0, COL_K // qw)),
            pl.BlockSpec((sb, 1, vw), lambda i: (i, 0, COL_V // vw)),
            pl.BlockSpec((sb, 1, vw), lambda i: (i, 0, COL_O // vw)),
            pl.BlockSpec((sb, 1, LANES), lambda i: (i, 0, 0)),
            pl.BlockSpec((sb, 2 * cb), lambda i: (i, 0)),
            pl.BlockSpec((sb, HEADS, DK, DV), lambda i: (i, 0, 0, 0)),
            pl.BlockSpec((sb, HEADS, DK), lambda i: (i, 0, 0)),
            pl.BlockSpec((sb, 1, LANES), lambda i: (i, 0, 0)),
            pl.BlockSpec((8, cb), lambda i: (0, 0)),
            pl.BlockSpec((1, cb), lambda i: (0, 0)),
            pl.BlockSpec((1, LANES), lambda i: (0, 0)),
            pl.BlockSpec((1, vw), lambda i: (0, 0)),
        ],
        out_specs=[
            pl.BlockSpec((sb, cb), lambda i: (i, 0)),
            pl.BlockSpec((sb, 1, vw), lambda i: (i, 0, 0)),
            pl.BlockSpec((sb, 2 * cb), lambda i: (i, 0)),
            pl.BlockSpec((sb, HEADS, DK, DV), lambda i: (i, 0, 0, 0)),
            pl.BlockSpec((sb, HEADS, DK), lambda i: (i, 0, 0)),
            pl.BlockSpec((sb, 1, LANES), lambda i: (i, 0, 0)),
        ],
        out_shape=[
            jax.ShapeDtypeStruct((n_dec, cb), BF16),
            jax.ShapeDtypeStruct((n_dec, 1, vw), F32),
            jax.ShapeDtypeStruct((n_dec, 2 * cb), F32),
            jax.ShapeDtypeStruct((n_dec, HEADS, DK, DV), F32),
            jax.ShapeDtypeStruct((n_dec, HEADS, DK), F32),
            jax.ShapeDtypeStruct((n_dec, 1, LANES), F32),
        ],
        compiler_params=_cparams(("arbitrary",)),
        name="mixer_sample",
    )(z, z, z, zs3, zs3, zs3, zs3, zg3, st, c_s, n_s, m3, w_conv, g_conv, b_gates, g_norm)


def _outproj_router_kernel(yc_ref, ym_ref, x_ref, wo1_ref, wo2_ref, g_ref, wr_hi_ref, wr_lo_ref, br_ref,
                           cin_ref, h1_ref, xn_ref, e_ref, gate_ref, rank_ref, cnt_ref, carry):
    tm = x_ref.shape[0]

    @pl.when(pl.program_id(0) == 0)
    def _():
        carry[...] = cin_ref[...]

    mix = (jnp.dot(yc_ref[...], wo1_ref[...], preferred_element_type=F32)
           + jnp.dot(ym_ref[...], wo2_ref[...], preferred_element_type=F32))
    h1 = x_ref[...] + mix
    h1_ref[...] = h1
    xn = _rms(h1, g_ref[...])
    xn_ref[...] = xn

    xh = xn.astype(BF16)
    xl = (xn - xh.astype(F32)).astype(BF16)
    logits = (jnp.dot(xh, wr_hi_ref[...], preferred_element_type=F32)
              + jnp.dot(xl, wr_hi_ref[...], preferred_element_type=F32)
              + jnp.dot(xh, wr_lo_ref[...], preferred_element_type=F32)) + br_ref[...]
    lane = lax.broadcasted_iota(jnp.int32, (tm, LANES), 1)
    lane_f = lane.astype(F32)
    work = jnp.where(lane < N_EXPERTS, logits, NEG_BIG)

    tops, idxs = [], []
    chosen = jnp.zeros((tm, LANES), F32)
    for _ in range(TOP_K):
        mx = jnp.max(work, axis=1, keepdims=True)
        idx = jnp.min(jnp.where(work == mx, lane_f, float(LANES)), axis=1, keepdims=True)
        sel = lane_f == idx
        tops.append(mx)
        idxs.append(idx)
        chosen = jnp.where(sel, 1.0, chosen)
        work = jnp.where(sel, NEG_BIG, work)

    exps = [jnp.exp(t - tops[0]) for t in tops]
    denom = exps[0] + exps[1] + exps[2] + exps[3]

    r_i = lax.broadcasted_iota(jnp.int32, (tm, tm), 0)
    c_i = lax.broadcasted_iota(jnp.int32, (tm, tm), 1)
    before = jnp.dot((c_i < r_i).astype(BF16), chosen.astype(BF16),
                     preferred_element_type=F32) + carry[...]

    e_out = jnp.zeros((tm, LANES), F32)
    g_out = jnp.zeros((tm, LANES), F32)
    r_out = jnp.zeros((tm, LANES), F32)
    for kk in range(TOP_K):
        rank = jnp.sum(jnp.where(lane_f == idxs[kk], before, 0.0), axis=1, keepdims=True)
        e_out = jnp.where(lane == kk, idxs[kk], e_out)
        g_out = jnp.where(lane == kk, exps[kk] / denom, g_out)
        r_out = jnp.where(lane == kk, rank, r_out)
    gate_ref[...] = g_out
    for c in range(tm // LANES):
        rows = slice(c * LANES, (c + 1) * LANES)
        e_ref[:, rows] = e_out[rows].T[0:8].astype(jnp.int32)
        rank_ref[:, rows] = r_out[rows].T[0:8].astype(jnp.int32)

    new_carry = carry[...] + jnp.sum(chosen, axis=0, keepdims=True)
    carry[...] = new_carry
    cnt_ref[...] = new_carry


def _outproj_router(y_conv, y_mlstm, x, wo1, wo2, g_ffn, wr_hi, wr_lo, b_router, cnt_in, tm):
    n_tok = x.shape[0]
    full = lambda shape: pl.BlockSpec(shape, lambda i: (0,) * len(shape))
    tile = lambda w: pl.BlockSpec((tm, w), lambda i: (i, 0))
    choice_major = pl.BlockSpec((8, tm), lambda i: (0, i))
    return pl.pallas_call(
        _outproj_router_kernel,
        grid=(n_tok // tm,),
        in_specs=[
            tile(CONV_DIM), tile(HEADS * DV), tile(D_MODEL),
            full((CONV_DIM, D_MODEL)), full((HEADS * DV, D_MODEL)), full((1, D_MODEL)),
            full((D_MODEL, LANES)), full((D_MODEL, LANES)), full((1, LANES)), full((1, LANES)),
        ],
        out_specs=[tile(D_MODEL), tile(D_MODEL), choice_major, tile(LANES), choice_major, full((1, LANES))],
        out_shape=[
            jax.ShapeDtypeStruct((n_tok, D_MODEL), F32),
            jax.ShapeDtypeStruct((n_tok, D_MODEL), F32),
            jax.ShapeDtypeStruct((8, n_tok), jnp.int32),
            jax.ShapeDtypeStruct((n_tok, LANES), F32),
            jax.ShapeDtypeStruct((8, n_tok), jnp.int32),
            jax.ShapeDtypeStruct((1, LANES), F32),
        ],
        scratch_shapes=[pltpu.VMEM((1, LANES), F32)],
        compiler_params=_cparams(("arbitrary",)),
        name="outproj_router",
    )(y_conv, y_mlstm, x, wo1, wo2, g_ffn, wr_hi, wr_lo, b_router, cnt_in)


def _zero_fill_rows(dst, zero_scr, sem, start, count, act):
    low = count & 7
    for t in range(7):
        @pl.when(t < low)
        def _(t=t):
            act(pltpu.make_async_copy(zero_scr.at[pl.ds(0, 1)], dst.at[pl.ds(start + t, 1)], sem))

    off = start + low
    bit = 8
    while bit < SUB:
        take = count & bit

        @pl.when(take != 0)
        def _(off=off, bit=bit):
            row = pl.multiple_of(off, 8)
            act(pltpu.make_async_copy(zero_scr.at[pl.ds(0, bit)], dst.at[pl.ds(row, bit)], sem))

        off = off + take
        bit *= 2


def _zero_fill_chunks(dst, zero_scr, sem, start, act):
    def body(j, c):
        row = pl.multiple_of(start + j * SUB, SUB)
        act(pltpu.make_async_copy(zero_scr, dst.at[pl.ds(row, SUB)], sem))
        return c

    lax.fori_loop(0, (dst.shape[0] - start) // SUB, body, 0)


def _dispatch_kernel(d0_ref, d1_ref, d2_ref, d3_ref, cnt_ref, pstart_ref, xn_ref, xn_s_ref, xs_out,
                     zero_scr, sem, zsem):
    dest_refs = (d0_ref, d1_ref, d2_ref, d3_ref)
    tm = xn_ref.shape[0]
    step = pl.program_id(0)
    last = pl.num_programs(0) - 1

    def fill(act):
        for e in range(N_EXPERTS):
            c = cnt_ref[e]
            _zero_fill_rows(xs_out, zero_scr, zsem, pstart_ref[e] + c, (SUB - c % SUB) % SUB, act)
        _zero_fill_chunks(xs_out, zero_scr, zsem, pstart_ref[N_EXPERTS], act)

    @pl.when(step == 0)
    def _():
        zero_scr[...] = jnp.zeros_like(zero_scr)
        fill(lambda cp: cp.start())

    def scatter_rows(src, base):
        rows = src.shape[0]

        def issue(r, c):
            for kk in range(TOP_K):
                d = dest_refs[kk][base + r]
                pltpu.make_async_copy(src.at[pl.ds(r, 1)], xs_out.at[pl.ds(d, 1)], sem).start()
            return c

        lax.fori_loop(0, rows, issue, 0, unroll=ISSUE_UNROLL)
        for _ in range(TOP_K):
            pltpu.make_async_copy(src, xs_out.at[pl.ds(0, rows)], sem).wait()

    scatter_rows(xn_ref, step * tm)

    @pl.when(step == last)
    def _():
        scatter_rows(xn_s_ref, pl.num_programs(0) * tm)

    @pl.when(step == 0)
    def _():
        fill(lambda cp: cp.wait())


def _dispatch(dest, counts, pstart, xn_p, xn_s, n_slots, tm):
    n_prompt = xn_p.shape[0]
    return pl.pallas_call(
        _dispatch_kernel,
        grid_spec=pltpu.PrefetchScalarGridSpec(
            num_scalar_prefetch=TOP_K + 2,
            grid=(n_prompt // tm,),
            in_specs=[pl.BlockSpec((tm, D_MODEL), lambda i, *_: (i, 0)),
                      pl.BlockSpec(xn_s.shape, lambda i, *_: (0, 0))],
            out_specs=pl.BlockSpec(memory_space=pl.ANY),
            scratch_shapes=[pltpu.VMEM((SUB, D_MODEL), F32), pltpu.SemaphoreType.DMA(()),
                            pltpu.SemaphoreType.DMA(())],
        ),
        out_shape=jax.ShapeDtypeStruct((n_slots, D_MODEL), F32),
        compiler_params=_cparams(("arbitrary",)),
        name="dispatch",
    )(*[dest[k] for k in range(TOP_K)], counts, pstart, xn_p, xn_s)


def _expert_kernel(se_ref, st_ref, sn_ref, end_ref, xs_ref, wg_ref, wu_ref, bg_ref, bu_ref, wd_ref, bd_ref,
                   y_ref, x_scr, xb_scr, acc, zero_scr, sem, zsem, xsem):
    i, f = pl.program_id(0), pl.program_id(1)
    nf = pl.num_programs(1)
    tmb = acc.shape[0]
    n = sn_ref[i]
    start = st_ref[i]

    def x_copy(j):
        row = pl.multiple_of(st_ref[j], SUB)
        return pltpu.make_async_copy(xs_ref.at[pl.ds(row, tmb)], x_scr, xsem)

    @pl.when((i == 0) & (f == 0))
    def _():
        zero_scr[...] = jnp.zeros_like(zero_scr)
        _zero_fill_chunks(y_ref, zero_scr, zsem, end_ref[0], lambda cp: cp.start())
        _zero_fill_chunks(y_ref, zero_scr, zsem, end_ref[0], lambda cp: cp.wait())

    kc = D_MODEL // MOE_SPLIT

    def y_copy(o, mc, c):
        row = pl.multiple_of(start + o, SUB)
        cols = pl.ds(c * kc, kc)
        return pltpu.make_async_copy(acc.at[pl.ds(o, mc), cols], y_ref.at[pl.ds(row, mc), cols], sem)

    def for_each_sub_block(fn):
        kmax = tmb // SUB
        half = SUB // 2
        nh = (n + half - 1) // half
        for kh in range(2 * kmax, 2 * kmax - 4, -1):
            pl.when(nh == kh)(functools.partial(fn, 0, kh * half, -(-kh // 2) * SUB))
        nb = (n + SUB - 1) // SUB
        bits = [b for b in (64, 32, 16, 8, 4, 2, 1) if b <= kmax - 2]
        small = nh <= 2 * kmax - 4
        for b in bits:
            higher = [h for h in bits if h > b]
            for mask in range(1 << len(higher)):
                above = sum(h for j, h in enumerate(higher) if (mask >> j) & 1)
                if above + b > kmax - 2:
                    continue
                cond = small & ((nb & b) != 0) & ((nb & sum(higher)) == above)
                pl.when(cond)(functools.partial(fn, above * SUB, b * SUB, b * SUB))

    @pl.when(n > 0)
    def _():
        @pl.when(f == 0)
        def _():
            @pl.when(i == 0)
            def _():
                x_copy(0).start()

            x_copy(i).wait()
            xb_scr[...] = x_scr[...].astype(BF16)
            nxt = jnp.minimum(i + 1, pl.num_programs(0) - 1)

            @pl.when((i + 1 < pl.num_programs(0)) & (sn_ref[nxt] > 0))
            def _():
                x_copy(nxt).start()

            acc[...] = jnp.broadcast_to(bd_ref[0], acc.shape)

        def sub_block(o, m, mc):
            g = bg_ref[0]
            u = bu_ref[0]
            for c in range(MOE_SPLIT):
                xb = xb_scr[o:o + m, c * kc:(c + 1) * kc]
                g = g + jnp.dot(xb, wg_ref[0, c * kc:(c + 1) * kc, :].astype(BF16), preferred_element_type=F32)
                u = u + jnp.dot(xb, wu_ref[0, c * kc:(c + 1) * kc, :].astype(BF16), preferred_element_type=F32)
            g = jnp.minimum(g, SWIGLU_LIMIT)
            u = jnp.clip(u, -SWIGLU_LIMIT, SWIGLU_LIMIT)
            act = (g * jax.nn.sigmoid(SWIGLU_ALPHA * g) * (u + 1.0)).astype(BF16)
            for c in range(MOE_SPLIT):
                cols = slice(c * kc, (c + 1) * kc)
                acc[o:o + m, cols] += jnp.dot(act, wd_ref[0, :, cols].astype(BF16), preferred_element_type=F32)

                @pl.when(f == nf - 1)
                def _(c=c):
                    y_copy(o, mc, c).start()

        for_each_sub_block(sub_block)

        @pl.when(f == nf - 1)
        def _():
            def drain(o, m, mc):
                for c in range(MOE_SPLIT):
                    y_copy(o, mc, c).wait()

            for_each_sub_block(drain)


def _experts(sb_e, sb_start, sb_n, end, xs, w_gate_up, b_gate_up, w_down, b_down, nsb, tmb, tf):
    n_slots = xs.shape[0]
    nf = D_FF // tf

    def fidx(i, f, sn):
        return jnp.where(sn[i] > 0, f, nf - 1)

    return pl.pallas_call(
        _expert_kernel,
        grid_spec=pltpu.PrefetchScalarGridSpec(
            num_scalar_prefetch=4,
            grid=(nsb, nf),
            in_specs=[
                pl.BlockSpec(memory_space=pl.ANY),
                pl.BlockSpec((1, D_MODEL, tf), lambda i, f, se, st, sn, en: (se[i], 0, fidx(i, f, sn))),
                pl.BlockSpec((1, D_MODEL, tf), lambda i, f, se, st, sn, en: (se[i], 0, nf + fidx(i, f, sn))),
                pl.BlockSpec((1, 1, tf), lambda i, f, se, st, sn, en: (se[i], 0, fidx(i, f, sn))),
                pl.BlockSpec((1, 1, tf), lambda i, f, se, st, sn, en: (se[i], 0, nf + fidx(i, f, sn))),
                pl.BlockSpec((1, tf, D_MODEL), lambda i, f, se, st, sn, en: (se[i], fidx(i, f, sn), 0)),
                pl.BlockSpec((1, 1, D_MODEL), lambda i, f, se, st, sn, en: (se[i], 0, 0)),
            ],
            out_specs=pl.BlockSpec(memory_space=pl.ANY),
            scratch_shapes=[pltpu.VMEM((tmb, D_MODEL), F32), pltpu.VMEM((tmb, D_MODEL), BF16),
                            pltpu.VMEM((tmb, D_MODEL), F32), pltpu.VMEM((SUB, D_MODEL), F32),
                            pltpu.SemaphoreType.DMA(()), pltpu.SemaphoreType.DMA(()),
                            pltpu.SemaphoreType.DMA(())],
        ),
        out_shape=jax.ShapeDtypeStruct((n_slots, D_MODEL), F32),
        compiler_params=_cparams(("arbitrary", "arbitrary")),
        name="experts",
    )(sb_e, sb_start, sb_n, end, xs, w_gate_up, w_gate_up, b_gate_up, b_gate_up, w_down, b_down)


def _combine_kernel(d0_ref, d1_ref, d2_ref, d3_ref, h1_ref, gate_ref, p_ref, yb_ref, wpg_ref, wpp_ref,
                    gp_ref, gf_ref, out_ref, gbuf, sem):
    dest_refs = (d0_ref, d1_ref, d2_ref, d3_ref)
    tm = h1_ref.shape[0]
    step = pl.program_id(0)
    slot = step % 2

    def gather_tile(t, s):
        def issue(r, c):
            for kk in range(TOP_K):
                d = dest_refs[kk][t * tm + r]
                pltpu.make_async_copy(yb_ref.at[pl.ds(d, 1)], gbuf.at[s, kk, pl.ds(r, 1)], sem.at[s]).start()
            return c

        lax.fori_loop(0, tm, issue, 0, unroll=ISSUE_UNROLL)

    @pl.when(step == 0)
    def _():
        gather_tile(0, 0)

    @pl.when(step + 1 < pl.num_programs(0))
    def _():
        gather_tile(step + 1, 1 - slot)

    for kk in range(TOP_K):
        pltpu.make_async_copy(yb_ref.at[pl.ds(0, tm)], gbuf.at[slot, kk], sem.at[slot]).wait()

    gate = gate_ref[...]
    moe = gate[:, 0:1] * gbuf[slot, 0]
    for kk in range(1, TOP_K):
        moe = moe + gate[:, kk:kk + 1] * gbuf[slot, kk]
    h2 = h1_ref[...] + moe
    a = _rms(h2, gp_ref[...]).astype(BF16)
    pg = jax.nn.sigmoid(jnp.dot(a, wpg_ref[...], preferred_element_type=F32))
    pp = jnp.dot(p_ref[...].astype(BF16), wpp_ref[...], preferred_element_type=F32)
    h3 = h2 + pg * pp
    out_ref[...] = _rms(h3, gf_ref[...])


def _combine(dest, h1, gate, p, yb, wpg, wpp, g_ple, g_final, tm):
    n_tok = h1.shape[0]
    full = lambda shape: pl.BlockSpec(shape, lambda i, *_: (0,) * len(shape))
    tile = lambda w: pl.BlockSpec((tm, w), lambda i, *_: (i, 0))
    return pl.pallas_call(
        _combine_kernel,
        grid_spec=pltpu.PrefetchScalarGridSpec(
            num_scalar_prefetch=TOP_K,
            grid=(n_tok // tm,),
            in_specs=[
                tile(D_MODEL), tile(LANES), tile(PLE_DIM),
                pl.BlockSpec(memory_space=pl.ANY),
                full((D_MODEL, D_MODEL)), full((PLE_DIM, D_MODEL)), full((1, D_MODEL)), full((1, D_MODEL)),
            ],
            out_specs=tile(D_MODEL),
            scratch_shapes=[pltpu.VMEM((2, TOP_K, tm, D_MODEL), F32), pltpu.SemaphoreType.DMA((2,))],
        ),
        out_shape=jax.ShapeDtypeStruct((n_tok, D_MODEL), F32),
        compiler_params=_cparams(("arbitrary",)),
        name="combine",
    )(*[dest[k] for k in range(TOP_K)], h1, gate, p, yb, wpg, wpp, g_ple, g_final)


def kernel(x_prompt, x_sample, state_conv, state_mlstm_C, state_mlstm_n, state_mlstm_m, p_prompt, p_sample,
           g_mix, w_in, b_gates, w_conv, g_conv_out, g_mlstm_out, w_out, g_ffn, w_router, b_router,
           w_gate_up, b_gate_up, w_down, b_down, g_ple, w_ple_gate, w_ple_proj, g_final):
    batch, seq, _ = x_prompt.shape
    n_dec = x_sample.shape[0]
    assert w_in.shape[0] == 1 and x_sample.shape[1] == 1
    n_prompt = batch * seq
    n_tok = n_prompt + n_dec

    x_p = x_prompt.reshape(n_prompt, D_MODEL)
    x_s = x_sample.reshape(n_dec, D_MODEL)

    row = lambda a: a.reshape(1, -1)
    pad_lanes = lambda a: jnp.pad(a, ((0, 0), (0, LANES - a.shape[1])))
    w_gates = pad_lanes(w_in[0, :, MAIN_COLS:]).astype(BF16)
    bg = pad_lanes(row(b_gates[0]))
    wc = jnp.pad(w_conv[0], ((0, 8 - w_conv.shape[1]), (0, 0)))
    g_norm = row(g_mlstm_out[0])

    a_p, zg_p = _norm_gates(x_p, row(g_mix[0]), w_gates, tm=1024)
    a_s, zg_s = _norm_gates(x_s, row(g_mix[0]), w_gates, tm=n_dec)
    z_p = _inproj(a_p, w_in, tm=1024, tn=1024)
    z_s = _inproj(a_s, w_in, tm=n_dec, tn=1024)

    yc_p, ym_p, conv_p, c_p, n_p, m_p = _mixer_prompt(
        z_p.reshape(batch, seq, MAIN_COLS), zg_p.reshape(batch, seq, LANES), wc, row(g_conv_out[0]), bg,
        g_norm, nbt=2)
    yc_p = yc_p.reshape(n_prompt, CONV_DIM)
    ym_p = ym_p.reshape(n_prompt, HEADS * DV)
    yc_s, ym_s, conv_s, c_s, n_s, m_s = _mixer_sample(
        z_s, z_s.reshape(n_dec, 1, MAIN_COLS), zg_s.reshape(n_dec, 1, LANES),
        state_conv[0].reshape(n_dec, 2 * CONV_DIM), state_mlstm_C[0], state_mlstm_n[0],
        pad_lanes(state_mlstm_m[0]).reshape(n_dec, 1, LANES), wc, row(g_conv_out[0]), bg, g_norm,
        row0=0, n_dec=n_dec, sb=16)
    m_s = m_s[:, 0, :HEADS]
    ym_s = ym_s.reshape(n_dec, HEADS * DV).astype(BF16)

    wo = w_out[0].astype(BF16)
    wr = pad_lanes(w_router[0])
    wr_hi = wr.astype(BF16)
    wr_lo = (wr - wr_hi.astype(F32)).astype(BF16)
    router_args = (wo[:CONV_DIM], wo[CONV_DIM:], row(g_ffn[0]), wr_hi, wr_lo, pad_lanes(row(b_router[0])))
    h1_p, xn_p, e_p, gate_p, rank_p, cnt_p = _outproj_router(
        yc_p, ym_p, x_p, *router_args, jnp.zeros((1, LANES), F32), tm=512)
    h1_s, xn_s, e_s, gate_s, rank_s, cnt = _outproj_router(
        yc_s, ym_s, x_s, *router_args, cnt_p, tm=n_dec)

    n_assign = n_tok * TOP_K
    tmb = MOE_TMB
    nsb = n_assign // tmb + N_EXPERTS
    n_slots = -(-(n_assign + N_EXPERTS * (SUB - 1)) // SUB) * SUB + tmb
    counts = cnt[0, :N_EXPERTS].astype(jnp.int32)
    padded = (counts + SUB - 1) // SUB * SUB
    pend = jnp.cumsum(padded)
    pstart = jnp.concatenate([jnp.zeros((1,), jnp.int32), pend]).astype(jnp.int32)
    sb_per_e = (counts + tmb - 1) // tmb
    sb_cum = jnp.cumsum(sb_per_e)
    n_used = sb_cum[-1]
    sb = jnp.minimum(jnp.arange(nsb, dtype=jnp.int32), n_used - 1)
    sb_e = jnp.minimum(jnp.sum((sb_cum[None, :] <= sb[:, None]).astype(jnp.int32), axis=1), N_EXPERTS - 1)
    sb_j = sb - (sb_cum - sb_per_e)[sb_e]
    sb_start = (pstart[sb_e] + sb_j * tmb).astype(jnp.int32)
    sb_n = jnp.where(jnp.arange(nsb) < n_used, jnp.clip(counts[sb_e] - sb_j * tmb, 0, tmb), 0).astype(jnp.int32)
    expert_ids = jnp.arange(N_EXPERTS, dtype=jnp.int32)[:, None, None]

    def slot_of(e, r):
        first = jnp.sum(jnp.where(e[None, :TOP_K] == expert_ids, pstart[:N_EXPERTS, None, None], 0), axis=0)
        return (first + r[:TOP_K]).astype(jnp.int32)
    dest_p = slot_of(e_p, rank_p)
    dest_s = slot_of(e_s, rank_s)

    xs = _dispatch(jnp.concatenate([dest_p, dest_s], axis=1), counts, pstart, xn_p, xn_s, n_slots, tm=512)
    yb = _experts(sb_e.astype(jnp.int32), sb_start, sb_n, pstart[N_EXPERTS:], xs, w_gate_up[0],
                  b_gate_up[0].reshape(N_EXPERTS, 1, 2 * D_FF), w_down[0],
                  b_down[0].reshape(N_EXPERTS, 1, D_MODEL), nsb, tmb, MOE_TF)

    ple_args = (w_ple_gate[0].astype(BF16), w_ple_proj[0].astype(BF16), row(g_ple[0]), row(g_final))
    out_p = _combine(dest_p, h1_p, gate_p, p_prompt[0].reshape(n_prompt, PLE_DIM), yb, *ple_args, tm=256)
    out_s = _combine(dest_s, h1_s, gate_s, p_sample[0].reshape(n_dec, PLE_DIM), yb, *ple_args, tm=n_dec)

    y_prompt = out_p.reshape(batch, seq, D_MODEL)
    y_sample = out_s.reshape(n_dec, 1, D_MODEL)
    return (y_prompt, y_sample,
            conv_p[None], c_p[None], n_p[None], m_p[None, :, :, 0],
            conv_s.reshape(1, n_dec, 2, CONV_DIM), c_s[None], n_s[None], m_s[None])
```

```python
import functools

import jax
import jax.numpy as jnp
from jax import lax
from jax.experimental import pallas as pl
from jax.experimental.pallas import tpu as pltpu

F32 = jnp.float32
BF16 = jnp.bfloat16

D_MODEL = 2048
CONV_DIM = 1024
HEADS = 4
DK = 128
DV = 256
CHUNK = 128
N_EXPERTS = 32
TOP_K = 4
D_FF = 2048
PLE_DIM = 256
SWIGLU_LIMIT = 7.0
SWIGLU_ALPHA = 1.702
EPS = 1e-6
LANES = 128
NEG_BIG = -1e30
MAIN_COLS = 3 * CONV_DIM + HEADS * (2 * DK + 2 * DV)

COL_B, COL_C, COL_H = 0, CONV_DIM, 2 * CONV_DIM
COL_Q = 3 * CONV_DIM
COL_K = COL_Q + HEADS * DK
COL_V = COL_K + HEADS * DK
COL_O = COL_V + HEADS * DV

VMEM_LIMIT = 56 * 1024 * 1024

SUB = 128
MOE_TMB = 9 * SUB
MOE_TF = 512
MOE_SPLIT = 4
ISSUE_UNROLL = 8


def _cparams(sem, vmem=VMEM_LIMIT):
    return pltpu.CompilerParams(dimension_semantics=sem, vmem_limit_bytes=vmem)


def _rms(x, g):
    return x * lax.rsqrt(jnp.mean(x * x, axis=-1, keepdims=True) + EPS) * g


def _split3(x):
    x1 = x.astype(BF16)
    r1 = x - x1.astype(F32)
    x2 = r1.astype(BF16)
    x3 = (r1 - x2.astype(F32)).astype(BF16)
    return x1, x2, x3


def _log_sigmoid(x):
    return jnp.minimum(x, 0.0) - jnp.log(1.0 + jnp.exp(-jnp.abs(x)))


def _norm_gates_kernel(x_ref, g_ref, wg_ref, a_ref, zg_ref):
    a = _rms(x_ref[...], g_ref[...]).astype(BF16)
    a_ref[...] = a
    zg_ref[...] = jnp.dot(a, wg_ref[...], preferred_element_type=F32)


def _norm_gates(x, g, w_gates, tm):
    n_tok = x.shape[0]
    return pl.pallas_call(
        _norm_gates_kernel,
        grid=(n_tok // tm,),
        in_specs=[
            pl.BlockSpec((tm, D_MODEL), lambda i: (i, 0)),
            pl.BlockSpec((1, D_MODEL), lambda i: (0, 0)),
            pl.BlockSpec((D_MODEL, LANES), lambda i: (0, 0)),
        ],
        out_specs=[
            pl.BlockSpec((tm, D_MODEL), lambda i: (i, 0)),
            pl.BlockSpec((tm, LANES), lambda i: (i, 0)),
        ],
        out_shape=[
            jax.ShapeDtypeStruct((n_tok, D_MODEL), BF16),
            jax.ShapeDtypeStruct((n_tok, LANES), F32),
        ],
        compiler_params=_cparams(("parallel",)),
        name="norm_gates",
    )(x, g, w_gates)


def _inproj_kernel(a_ref, w_ref, z_ref, w_scr):
    @pl.when(pl.program_id(1) == 0)
    def _():
        w_scr[...] = w_ref[0].astype(BF16)

    z_ref[...] = jnp.dot(a_ref[...], w_scr[...], preferred_element_type=F32)


def _inproj(a, w_in, tm, tn):
    n_tok = a.shape[0]
    return pl.pallas_call(
        _inproj_kernel,
        grid=(MAIN_COLS // tn, n_tok // tm),
        in_specs=[
            pl.BlockSpec((tm, D_MODEL), lambda j, i: (i, 0)),
            pl.BlockSpec((1, D_MODEL, tn), lambda j, i: (0, 0, j)),
        ],
        out_specs=pl.BlockSpec((tm, tn), lambda j, i: (i, j)),
        out_shape=jax.ShapeDtypeStruct((n_tok, MAIN_COLS), F32),
        scratch_shapes=[pltpu.VMEM((D_MODEL, tn), BF16)],
        compiler_params=_cparams(("arbitrary", "arbitrary")),
        name="inproj",
    )(a, w_in)


def _conv_chunk(zb, zc, zh, prev, wc, g):
    tt = zb.shape[0]
    u = zc * zh
    row = lax.broadcasted_iota(jnp.int32, (tt, 1), 0)
    u1 = jnp.where(row == 0, prev[7:8], pltpu.roll(u, 1, axis=0))
    u2 = jnp.where(row == 0, prev[6:7], jnp.where(row == 1, prev[7:8], pltpu.roll(u, 2, axis=0)))
    conv = wc[0:1] * u2 + wc[1:2] * u1 + wc[2:3] * u
    return _rms(zb * conv, g).astype(BF16), u


def _tri_dot(tri_bf16, x, tri_first):
    out = None
    for part in _split3(x):
        d = (jnp.dot(tri_bf16, part, preferred_element_type=F32) if tri_first
             else jnp.dot(part, tri_bf16, preferred_element_type=F32))
        out = d if out is None else out + d
    return out


def _mixer_prompt_kernel(zb_ref, zc_ref, zh_ref, q_ref, k_ref, v_ref, o_ref, zg_ref,
                         wc_ref, gc_ref, bg_ref, gn_ref,
                         yc_ref, y_ref, st_out, c_out, n_out, m_out, carry, c_scr, n_scr, m_scr):
    L = CHUNK

    @pl.when(pl.program_id(1) == 0)
    def _():
        carry[...] = jnp.zeros_like(carry)
        c_scr[...] = jnp.zeros_like(c_scr)
        n_scr[...] = jnp.zeros_like(n_scr)
        m_scr[...] = jnp.zeros_like(m_scr)

    lane = lax.broadcasted_iota(jnp.int32, (L, LANES), 1)
    r_i = lax.broadcasted_iota(jnp.int32, (L, L), 0)
    c_i = lax.broadcasted_iota(jnp.int32, (L, L), 1)
    causal = c_i <= r_i
    tril = causal.astype(BF16)
    triu = (r_i <= c_i).astype(BF16)

    nbt = q_ref.shape[0]
    for bi in range(nbt):
        yc, u = _conv_chunk(zb_ref[bi], zc_ref[bi], zh_ref[bi], carry[bi], wc_ref[...], gc_ref[...])
        yc_ref[bi] = yc
        carry[bi] = u[L - 8:L]
        st_out[bi] = u[L - 2:L]

    prs = [(bi, h) for bi in range(nbt) for h in range(HEADS)]
    each = lambda fn: {pr: fn(*pr) for pr in prs}
    nt_dims = (((1,), (1,)), ((), ()))
    tn_dims = (((0,), (0,)), ((), ()))

    gates, gates_t, bcol_all, brow_all = {}, {}, {}, {}
    for bi in range(nbt):
        zg = zg_ref[bi] + bg_ref[...]
        gates[bi] = jnp.where(lane < HEADS, zg, _log_sigmoid(zg))
        gates_t[bi] = gates[bi].T
        bcol_all[bi] = _tri_dot(tril, gates[bi], True)
        brow_all[bi] = _tri_dot(triu, gates_t[bi], False)

    b_col = each(lambda bi, h: bcol_all[bi][:, HEADS + h:HEADS + h + 1])
    b_row = each(lambda bi, h: brow_all[bi][HEADS + h:HEADS + h + 1, :])
    i_col = each(lambda bi, h: gates[bi][:, h:h + 1])
    i_row = each(lambda bi, h: gates_t[bi][h:h + 1, :])
    b_end = each(lambda bi, h: b_col[bi, h][L - 1:L, :])
    m0 = each(lambda bi, h: m_scr[bi, h:h + 1, 0:1])
    n0 = each(lambda bi, h: n_scr[bi, h:h + 1, :])
    c0 = each(lambda bi, h: c_scr[bi, h])
    q = each(lambda bi, h: q_ref[bi, :, h * DK:(h + 1) * DK])
    k = each(lambda bi, h: k_ref[bi, :, h * DK:(h + 1) * DK] * (DK ** -0.5))
    qb = each(lambda bi, h: q[bi, h].astype(BF16))
    vb = each(lambda bi, h: v_ref[bi, :, h * DV:(h + 1) * DV].astype(BF16))

    dmat = each(lambda bi, h: jnp.where(causal, b_col[bi, h] - b_row[bi, h] + i_row[bi, h], NEG_BIG))
    inter = each(lambda bi, h: b_col[bi, h] + m0[bi, h])
    m_t = each(lambda bi, h: jnp.maximum(inter[bi, h], jnp.max(dmat[bi, h], axis=1, keepdims=True)))
    w_int = each(lambda bi, h: jnp.exp(inter[bi, h] - m_t[bi, h]))
    p = each(lambda bi, h: jnp.where(causal, jnp.exp(dmat[bi, h] - m_t[bi, h]), 0.0))
    s = each(lambda bi, h: lax.dot_general(qb[bi, h], k[bi, h].astype(BF16), nt_dims,
                                           preferred_element_type=F32) * p[bi, h])
    num = each(lambda bi, h: jnp.dot(s[bi, h].astype(BF16), vb[bi, h], preferred_element_type=F32)
               + w_int[bi, h] * jnp.dot(qb[bi, h], c0[bi, h].astype(BF16), preferred_element_type=F32))
    den = each(lambda bi, h: jnp.sum(s[bi, h], axis=1, keepdims=True)
               + w_int[bi, h] * jnp.sum(q[bi, h] * n0[bi, h], axis=1, keepdims=True))
    hh = each(lambda bi, h: num[bi, h] / jnp.maximum(jnp.abs(den[bi, h]), jnp.exp(-m_t[bi, h])))
    hn = each(lambda bi, h: hh[bi, h] * lax.rsqrt(jnp.mean(hh[bi, h] * hh[bi, h], axis=-1, keepdims=True)
                                                  + EPS) * gn_ref[:, h * DV:(h + 1) * DV])
    for bi, h in prs:
        y = jax.nn.sigmoid(o_ref[bi, :, h * DV:(h + 1) * DV]) * hn[bi, h]
        y_ref[bi, :, h * DV:(h + 1) * DV] = y.astype(BF16)

    g_col = each(lambda bi, h: b_end[bi, h] - b_col[bi, h] + i_col[bi, h])
    g_row = each(lambda bi, h: b_end[bi, h] - b_row[bi, h] + i_row[bi, h])
    m_new = each(lambda bi, h: jnp.maximum(b_end[bi, h] + m0[bi, h],
                                           jnp.max(g_row[bi, h], axis=1, keepdims=True)))
    decay = each(lambda bi, h: jnp.exp(b_end[bi, h] + m0[bi, h] - m_new[bi, h]))
    kw = each(lambda bi, h: k[bi, h] * jnp.exp(g_col[bi, h] - m_new[bi, h]))
    c_new = each(lambda bi, h: decay[bi, h] * c0[bi, h]
                 + lax.dot_general(kw[bi, h].astype(BF16), vb[bi, h], tn_dims, preferred_element_type=F32))
    n_new = each(lambda bi, h: decay[bi, h] * n0[bi, h] + jnp.sum(kw[bi, h], axis=0, keepdims=True))
    for bi, h in prs:
        c_scr[bi, h] = c_new[bi, h]
        n_scr[bi, h:h + 1, :] = n_new[bi, h]
        m_scr[bi, h:h + 1, :] = jnp.broadcast_to(m_new[bi, h], (1, LANES))
        c_out[bi, h] = c_new[bi, h]
        n_out[bi, h:h + 1, :] = n_new[bi, h]
        m_out[bi, h:h + 1, :] = jnp.broadcast_to(m_new[bi, h], (1, LANES))


def _mixer_prompt(z3, zg3, w_conv, g_conv, b_gates, g_norm, nbt):
    batch, seq, _ = z3.shape
    nc = seq // CHUNK
    cb, qw, vw = CONV_DIM, HEADS * DK, HEADS * DV
    zcol = lambda w, col: pl.BlockSpec((nbt, CHUNK, w), lambda b, c: (b, c, col // w))
    const = lambda shape: pl.BlockSpec(shape, lambda b, c: (0,) * len(shape))
    per_seq = lambda *tail: pl.BlockSpec((nbt,) + tail, lambda b, c: (b,) + (0,) * len(tail))
    return pl.pallas_call(
        _mixer_prompt_kernel,
        grid=(batch // nbt, nc),
        in_specs=[
            zcol(cb, COL_B), zcol(cb, COL_C), zcol(cb, COL_H),
            zcol(qw, COL_Q), zcol(qw, COL_K), zcol(vw, COL_V), zcol(vw, COL_O),
            pl.BlockSpec((nbt, CHUNK, LANES), lambda b, c: (b, c, 0)),
            const((8, cb)), const((1, cb)), const((1, LANES)), const((1, vw)),
        ],
        out_specs=[
            pl.BlockSpec((nbt, CHUNK, cb), lambda b, c: (b, c, 0)),
            pl.BlockSpec((nbt, CHUNK, vw), lambda b, c: (b, c, 0)),
            per_seq(2, cb), per_seq(HEADS, DK, DV), per_seq(HEADS, DK), per_seq(HEADS, LANES),
        ],
        out_shape=[
            jax.ShapeDtypeStruct((batch, seq, cb), BF16),
            jax.ShapeDtypeStruct((batch, seq, vw), BF16),
            jax.ShapeDtypeStruct((batch, 2, cb), F32),
            jax.ShapeDtypeStruct((batch, HEADS, DK, DV), F32),
            jax.ShapeDtypeStruct((batch, HEADS, DK), F32),
            jax.ShapeDtypeStruct((batch, HEADS, LANES), F32),
        ],
        scratch_shapes=[
            pltpu.VMEM((nbt, 8, cb), F32),
            pltpu.VMEM((nbt, HEADS, DK, DV), F32),
            pltpu.VMEM((nbt, 8, DK), F32),
            pltpu.VMEM((nbt, 8, LANES), F32),
        ],
        compiler_params=_cparams(("parallel", "arbitrary")),
        name="mixer_prompt",
    )(z3, z3, z3, z3, z3, z3, z3, zg3, w_conv, g_conv, b_gates, g_norm)


def _mixer_sample_kernel(zb_ref, zc_ref, zh_ref, q_ref, k_ref, v_ref, o_ref, zg_ref,
                         st_ref, c_ref, n_ref, m_ref, wc_ref, gc_ref, bg_ref, gn_ref,
                         yc_ref, ym_ref, st_out, c_out, n_out, m_out):
    sb = zb_ref.shape[0]
    u = zc_ref[...] * zh_ref[...]
    s0 = st_ref[:, 0:CONV_DIM]
    s1 = st_ref[:, CONV_DIM:2 * CONV_DIM]
    wc = wc_ref[...]
    conv = wc[0:1] * s0 + wc[1:2] * s1 + wc[2:3] * u
    yc_ref[...] = _rms(zb_ref[...] * conv, gc_ref[...]).astype(BF16)
    st_out[:, 0:CONV_DIM] = s1
    st_out[:, CONV_DIM:2 * CONV_DIM] = u

    lane_row = lax.broadcasted_iota(jnp.int32, (1, LANES), 1)
    eye = (lax.broadcasted_iota(jnp.int32, (DK, DK), 0)
           == lax.broadcasted_iota(jnp.int32, (DK, DK), 1)).astype(F32)

    def to_col(row):
        return jnp.sum(eye * row, axis=1, keepdims=True)

    def pick(row, j):
        return jnp.sum(jnp.where(lane_row == j, row, 0.0), axis=1, keepdims=True)

    def per_sample(s, carry):
        zg = zg_ref[s] + bg_ref[...]
        lf_all = _log_sigmoid(zg)
        m_row = m_ref[s]
        m_new_row = jnp.zeros((1, LANES), F32)
        for h in range(HEADS):
            q = q_ref[s, :, h * DK:(h + 1) * DK]
            k = k_ref[s, :, h * DK:(h + 1) * DK] * (DK ** -0.5)
            v = v_ref[s, :, h * DV:(h + 1) * DV]
            o = o_ref[s, :, h * DV:(h + 1) * DV]
            i_pre = pick(zg, h)
            lf = pick(lf_all, HEADS + h)
            m0 = pick(m_row, h)
            c0 = c_ref[s, h]
            n0 = n_ref[s, h:h + 1, :]
            inter = lf + m0
            m_t = jnp.maximum(inter, i_pre)
            w_int = jnp.exp(inter - m_t)
            p = jnp.exp(i_pre - m_t)
            sc = jnp.sum(q * k, axis=1, keepdims=True) * p
            qc = jnp.sum(to_col(q) * c0, axis=0, keepdims=True)
            num = sc * v + w_int * qc
            den = sc + w_int * jnp.sum(q * n0, axis=1, keepdims=True)
            hh = num / jnp.maximum(jnp.abs(den), jnp.exp(-m_t))
            hn = hh * lax.rsqrt(jnp.mean(hh * hh, axis=-1, keepdims=True) + EPS)
            hn = hn * gn_ref[:, h * DV:(h + 1) * DV]
            ym_ref[s, :, h * DV:(h + 1) * DV] = jax.nn.sigmoid(o) * hn
            c_out[s, h] = w_int * c0 + (p * to_col(k)) * v
            n_out[s, h:h + 1, :] = w_int * n0 + p * k
            m_new_row = jnp.where(lane_row == h, m_t, m_new_row)
        m_out[s] = m_new_row
        return carry

    lax.fori_loop(0, sb, per_sample, 0)


def _mixer_sample(z, zs3, zg3, st, c_s, n_s, m3, w_conv, g_conv, b_gates, g_norm, row0, n_dec, sb):
    cb, qw, vw = CONV_DIM, HEADS * DK, HEADS * DV
    r0 = row0 // sb
    row = lambda i: r0 + i
    return pl.pallas_call(
        _mixer_sample_kernel,
        grid=(n_dec // sb,),
        in_specs=[
            pl.BlockSpec((sb, cb), lambda i: (row(i), COL_B // cb)),
            pl.BlockSpec((sb, cb), lambda i: (row(i), COL_C // cb)),
            pl.BlockSpec((sb, cb), lambda i: (row(i), COL_H // cb)),
            pl.BlockSpec((sb, 1, qw), lambda i: (i, 0, COL_Q // qw)),
            pl.BlockSpec((sb, 1, qw), lambda i: (i, 0, COL_K // qw)),
            pl.BlockSpec((sb, 1, vw), lambda i: (i, 0, COL_V // vw)),
            pl.BlockSpec((sb, 1, vw), lambda i: (i, 0, COL_O // vw)),
            pl.BlockSpec((sb, 1, LANES), lambda i: (i, 0, 0)),
            pl.BlockSpec((sb, 2 * cb), lambda i: (i, 0)),
            pl.BlockSpec((sb, HEADS, DK, DV), lambda i: (i, 0, 0, 0)),
            pl.BlockSpec((sb, HEADS, DK), lambda i: (i, 0, 0)),
            pl.BlockSpec((sb, 1, LANES), lambda i: (i, 0, 0)),
            pl.BlockSpec((8, cb), lambda i: (0, 0)),
            pl.BlockSpec((1, cb), lambda i: (0, 0)),
            pl.BlockSpec((1, LANES), lambda i: (0, 0)),
            pl.BlockSpec((1, vw), lambda i: (0, 0)),
        ],
        out_specs=[
            pl.BlockSpec((sb, cb), lambda i: (i, 0)),
            pl.BlockSpec((sb, 1, vw), lambda i: (i, 0, 0)),
            pl.BlockSpec((sb, 2 * cb), lambda i: (i, 0)),
            pl.BlockSpec((sb, HEADS, DK, DV), lambda i: (i, 0, 0, 0)),
            pl.BlockSpec((sb, HEADS, DK), lambda i: (i, 0, 0)),
            pl.BlockSpec((sb, 1, LANES), lambda i: (i, 0, 0)),
        ],
        out_shape=[
            jax.ShapeDtypeStruct((n_dec, cb), BF16),
            jax.ShapeDtypeStruct((n_dec, 1, vw), F32),
            jax.ShapeDtypeStruct((n_dec, 2 * cb), F32),
            jax.ShapeDtypeStruct((n_dec, HEADS, DK, DV), F32),
            jax.ShapeDtypeStruct((n_dec, HEADS, DK), F32),
            jax.ShapeDtypeStruct((n_dec, 1, LANES), F32),
        ],
        compiler_params=_cparams(("arbitrary",)),
        name="mixer_sample",
    )(z, z, z, zs3, zs3, zs3, zs3, zg3, st, c_s, n_s, m3, w_conv, g_conv, b_gates, g_norm)


def _outproj_router_kernel(yc_ref, ym_ref, x_ref, wo1_ref, wo2_ref, g_ref, wr_hi_ref, wr_lo_ref, br_ref,
                           cin_ref, h1_ref, xn_ref, e_ref, gate_ref, rank_ref, cnt_ref, carry):
    tm = x_ref.shape[0]

    @pl.when(pl.program_id(0) == 0)
    def _():
        carry[...] = cin_ref[...]

    mix = (jnp.dot(yc_ref[...], wo1_ref[...], preferred_element_type=F32)
           + jnp.dot(ym_ref[...], wo2_ref[...], preferred_element_type=F32))
    h1 = x_ref[...] + mix
    h1_ref[...] = h1
    xn = _rms(h1, g_ref[...])
    xn_ref[...] = xn

    xh = xn.astype(BF16)
    xl = (xn - xh.astype(F32)).astype(BF16)
    logits = (jnp.dot(xh, wr_hi_ref[...], preferred_element_type=F32)
              + jnp.dot(xl, wr_hi_ref[...], preferred_element_type=F32)
              + jnp.dot(xh, wr_lo_ref[...], preferred_element_type=F32)) + br_ref[...]
    lane = lax.broadcasted_iota(jnp.int32, (tm, LANES), 1)
    lane_f = lane.astype(F32)
    work = jnp.where(lane < N_EXPERTS, logits, NEG_BIG)

    tops, idxs = [], []
    chosen = jnp.zeros((tm, LANES), F32)
    for _ in range(TOP_K):
        mx = jnp.max(work, axis=1, keepdims=True)
        idx = jnp.min(jnp.where(work == mx, lane_f, float(LANES)), axis=1, keepdims=True)
        sel = lane_f == idx
        tops.append(mx)
        idxs.append(idx)
        chosen = jnp.where(sel, 1.0, chosen)
        work = jnp.where(sel, NEG_BIG, work)

    exps = [jnp.exp(t - tops[0]) for t in tops]
    denom = exps[0] + exps[1] + exps[2] + exps[3]

    r_i = lax.broadcasted_iota(jnp.int32, (tm, tm), 0)
    c_i = lax.broadcasted_iota(jnp.int32, (tm, tm), 1)
    before = jnp.dot((c_i < r_i).astype(BF16), chosen.astype(BF16),
                     preferred_element_type=F32) + carry[...]

    e_out = jnp.zeros((tm, LANES), F32)
    g_out = jnp.zeros((tm, LANES), F32)
    r_out = jnp.zeros((tm, LANES), F32)
    for kk in range(TOP_K):
        rank = jnp.sum(jnp.where(lane_f == idxs[kk], before, 0.0), axis=1, keepdims=True)
        e_out = jnp.where(lane == kk, idxs[kk], e_out)
        g_out = jnp.where(lane == kk, exps[kk] / denom, g_out)
        r_out = jnp.where(lane == kk, rank, r_out)
    gate_ref[...] = g_out
    for c in range(tm // LANES):
        rows = slice(c * LANES, (c + 1) * LANES)
        e_ref[:, rows] = e_out[rows].T[0:8].astype(jnp.int32)
        rank_ref[:, rows] = r_out[rows].T[0:8].astype(jnp.int32)

    new_carry = carry[...] + jnp.sum(chosen, axis=0, keepdims=True)
    carry[...] = new_carry
    cnt_ref[...] = new_carry


def _outproj_router(y_conv, y_mlstm, x, wo1, wo2, g_ffn, wr_hi, wr_lo, b_router, cnt_in, tm):
    n_tok = x.shape[0]
    full = lambda shape: pl.BlockSpec(shape, lambda i: (0,) * len(shape))
    tile = lambda w: pl.BlockSpec((tm, w), lambda i: (i, 0))
    choice_major = pl.BlockSpec((8, tm), lambda i: (0, i))
    return pl.pallas_call(
        _outproj_router_kernel,
        grid=(n_tok // tm,),
        in_specs=[
            tile(CONV_DIM), tile(HEADS * DV), tile(D_MODEL),
            full((CONV_DIM, D_MODEL)), full((HEADS * DV, D_MODEL)), full((1, D_MODEL)),
            full((D_MODEL, LANES)), full((D_MODEL, LANES)), full((1, LANES)), full((1, LANES)),
        ],
        out_specs=[tile(D_MODEL), tile(D_MODEL), choice_major, tile(LANES), choice_major, full((1, LANES))],
        out_shape=[
            jax.ShapeDtypeStruct((n_tok, D_MODEL), F32),
            jax.ShapeDtypeStruct((n_tok, D_MODEL), F32),
            jax.ShapeDtypeStruct((8, n_tok), jnp.int32),
            jax.ShapeDtypeStruct((n_tok, LANES), F32),
            jax.ShapeDtypeStruct((8, n_tok), jnp.int32),
            jax.ShapeDtypeStruct((1, LANES), F32),
        ],
        scratch_shapes=[pltpu.VMEM((1, LANES), F32)],
        compiler_params=_cparams(("arbitrary",)),
        name="outproj_router",
    )(y_conv, y_mlstm, x, wo1, wo2, g_ffn, wr_hi, wr_lo, b_router, cnt_in)


def _zero_fill_rows(dst, zero_scr, sem, start, count, act):
    low = count & 7
    for t in range(7):
        @pl.when(t < low)
        def _(t=t):
            act(pltpu.make_async_copy(zero_scr.at[pl.ds(0, 1)], dst.at[pl.ds(start + t, 1)], sem))

    off = start + low
    bit = 8
    while bit < SUB:
        take = count & bit

        @pl.when(take != 0)
        def _(off=off, bit=bit):
            row = pl.multiple_of(off, 8)
            act(pltpu.make_async_copy(zero_scr.at[pl.ds(0, bit)], dst.at[pl.ds(row, bit)], sem))

        off = off + take
        bit *= 2


def _zero_fill_chunks(dst, zero_scr, sem, start, act):
    def body(j, c):
        row = pl.multiple_of(start + j * SUB, SUB)
        act(pltpu.make_async_copy(zero_scr, dst.at[pl.ds(row, SUB)], sem))
        return c

    lax.fori_loop(0, (dst.shape[0] - start) // SUB, body, 0)


def _dispatch_kernel(d0_ref, d1_ref, d2_ref, d3_ref, cnt_ref, pstart_ref, xn_ref, xn_s_ref, xs_out,
                     zero_scr, sem, zsem):
    dest_refs = (d0_ref, d1_ref, d2_ref, d3_ref)
    tm = xn_ref.shape[0]
    step = pl.program_id(0)
    last = pl.num_programs(0) - 1

    def fill(act):
        for e in range(N_EXPERTS):
            c = cnt_ref[e]
            _zero_fill_rows(xs_out, zero_scr, zsem, pstart_ref[e] + c, (SUB - c % SUB) % SUB, act)
        _zero_fill_chunks(xs_out, zero_scr, zsem, pstart_ref[N_EXPERTS], act)

    @pl.when(step == 0)
    def _():
        zero_scr[...] = jnp.zeros_like(zero_scr)
        fill(lambda cp: cp.start())

    def scatter_rows(src, base):
        rows = src.shape[0]

        def issue(r, c):
            for kk in range(TOP_K):
                d = dest_refs[kk][base + r]
                pltpu.make_async_copy(src.at[pl.ds(r, 1)], xs_out.at[pl.ds(d, 1)], sem).start()
            return c

        lax.fori_loop(0, rows, issue, 0, unroll=ISSUE_UNROLL)
        for _ in range(TOP_K):
            pltpu.make_async_copy(src, xs_out.at[pl.ds(0, rows)], sem).wait()

    scatter_rows(xn_ref, step * tm)

    @pl.when(step == last)
    def _():
        scatter_rows(xn_s_ref, pl.num_programs(0) * tm)

    @pl.when(step == 0)
    def _():
        fill(lambda cp: cp.wait())


def _dispatch(dest, counts, pstart, xn_p, xn_s, n_slots, tm):
    n_prompt = xn_p.shape[0]
    return pl.pallas_call(
        _dispatch_kernel,
        grid_spec=pltpu.PrefetchScalarGridSpec(
            num_scalar_prefetch=TOP_K + 2,
            grid=(n_prompt // tm,),
            in_specs=[pl.BlockSpec((tm, D_MODEL), lambda i, *_: (i, 0)),
                      pl.BlockSpec(xn_s.shape, lambda i, *_: (0, 0))],
            out_specs=pl.BlockSpec(memory_space=pl.ANY),
            scratch_shapes=[pltpu.VMEM((SUB, D_MODEL), F32), pltpu.SemaphoreType.DMA(()),
                            pltpu.SemaphoreType.DMA(())],
        ),
        out_shape=jax.ShapeDtypeStruct((n_slots, D_MODEL), F32),
        compiler_params=_cparams(("arbitrary",)),
        name="dispatch",
    )(*[dest[k] for k in range(TOP_K)], counts, pstart, xn_p, xn_s)


def _expert_kernel(se_ref, st_ref, sn_ref, end_ref, xs_ref, wg_ref, wu_ref, bg_ref, bu_ref, wd_ref, bd_ref,
                   y_ref, x_scr, xb_scr, acc, zero_scr, sem, zsem, xsem):
    i, f = pl.program_id(0), pl.program_id(1)
    nf = pl.num_programs(1)
    tmb = acc.shape[0]
    n = sn_ref[i]
    start = st_ref[i]

    def x_copy(j):
        row = pl.multiple_of(st_ref[j], SUB)
        return pltpu.make_async_copy(xs_ref.at[pl.ds(row, tmb)], x_scr, xsem)

    @pl.when((i == 0) & (f == 0))
    def _():
        zero_scr[...] = jnp.zeros_like(zero_scr)
        _zero_fill_chunks(y_ref, zero_scr, zsem, end_ref[0], lambda cp: cp.start())
        _zero_fill_chunks(y_ref, zero_scr, zsem, end_ref[0], lambda cp: cp.wait())

    kc = D_MODEL // MOE_SPLIT

    def y_copy(o, mc, c):
        row = pl.multiple_of(start + o, SUB)
        cols = pl.ds(c * kc, kc)
        return pltpu.make_async_copy(acc.at[pl.ds(o, mc), cols], y_ref.at[pl.ds(row, mc), cols], sem)

    def for_each_sub_block(fn):
        kmax = tmb // SUB
        half = SUB // 2
        nh = (n + half - 1) // half
        for kh in range(2 * kmax, 2 * kmax - 4, -1):
            pl.when(nh == kh)(functools.partial(fn, 0, kh * half, -(-kh // 2) * SUB))
        nb = (n + SUB - 1) // SUB
        bits = [b for b in (64, 32, 16, 8, 4, 2, 1) if b <= kmax - 2]
        small = nh <= 2 * kmax - 4
        for b in bits:
            higher = [h for h in bits if h > b]
            for mask in range(1 << len(higher)):
                above = sum(h for j, h in enumerate(higher) if (mask >> j) & 1)
                if above + b > kmax - 2:
                    continue
                cond = small & ((nb & b) != 0) & ((nb & sum(higher)) == above)
                pl.when(cond)(functools.partial(fn, above * SUB, b * SUB, b * SUB))

    @pl.when(n > 0)
    def _():
        @pl.when(f == 0)
        def _():
            @pl.when(i == 0)
            def _():
                x_copy(0).start()

            x_copy(i).wait()
            xb_scr[...] = x_scr[...].astype(BF16)
            nxt = jnp.minimum(i + 1, pl.num_programs(0) - 1)

            @pl.when((i + 1 < pl.num_programs(0)) & (sn_ref[nxt] > 0))
            def _():
                x_copy(nxt).start()

            acc[...] = jnp.broadcast_to(bd_ref[0], acc.shape)

        def sub_block(o, m, mc):
            g = bg_ref[0]
            u = bu_ref[0]
            for c in range(MOE_SPLIT):
                xb = xb_scr[o:o + m, c * kc:(c + 1) * kc]
                g = g + jnp.dot(xb, wg_ref[0, c * kc:(c + 1) * kc, :].astype(BF16), preferred_element_type=F32)
                u = u + jnp.dot(xb, wu_ref[0, c * kc:(c + 1) * kc, :].astype(BF16), preferred_element_type=F32)
            g = jnp.minimum(g, SWIGLU_LIMIT)
            u = jnp.clip(u, -SWIGLU_LIMIT, SWIGLU_LIMIT)
            act = (g * jax.nn.sigmoid(SWIGLU_ALPHA * g) * (u + 1.0)).astype(BF16)
            for c in range(MOE_SPLIT):
                cols = slice(c * kc, (c + 1) * kc)
                acc[o:o + m, cols] += jnp.dot(act, wd_ref[0, :, cols].astype(BF16), preferred_element_type=F32)

                @pl.when(f == nf - 1)
                def _(c=c):
                    y_copy(o, mc, c).start()

        for_each_sub_block(sub_block)

        @pl.when(f == nf - 1)
        def _():
            def drain(o, m, mc):
                for c in range(MOE_SPLIT):
                    y_copy(o, mc, c).wait()

            for_each_sub_block(drain)


def _experts(sb_e, sb_start, sb_n, end, xs, w_gate_up, b_gate_up, w_down, b_down, nsb, tmb, tf):
    n_slots = xs.shape[0]
    nf = D_FF // tf

    def fidx(i, f, sn):
        return jnp.where(sn[i] > 0, f, nf - 1)

    return pl.pallas_call(
        _expert_kernel,
        grid_spec=pltpu.PrefetchScalarGridSpec(
            num_scalar_prefetch=4,
            grid=(nsb, nf),
            in_specs=[
                pl.BlockSpec(memory_space=pl.ANY),
                pl.BlockSpec((1, D_MODEL, tf), lambda i, f, se, st, sn, en: (se[i], 0, fidx(i, f, sn))),
                pl.BlockSpec((1, D_MODEL, tf), lambda i, f, se, st, sn, en: (se[i], 0, nf + fidx(i, f, sn))),
                pl.BlockSpec((1, 1, tf), lambda i, f, se, st, sn, en: (se[i], 0, fidx(i, f, sn))),
                pl.BlockSpec((1, 1, tf), lambda i, f, se, st, sn, en: (se[i], 0, nf + fidx(i, f, sn))),
                pl.BlockSpec((1, tf, D_MODEL), lambda i, f, se, st, sn, en: (se[i], fidx(i, f, sn), 0)),
                pl.BlockSpec((1, 1, D_MODEL), lambda i, f, se, st, sn, en: (se[i], 0, 0)),
            ],
            out_specs=pl.BlockSpec(memory_space=pl.ANY),
            scratch_shapes=[pltpu.VMEM((tmb, D_MODEL), F32), pltpu.VMEM((tmb, D_MODEL), BF16),
                            pltpu.VMEM((tmb, D_MODEL), F32), pltpu.VMEM((SUB, D_MODEL), F32),
                            pltpu.SemaphoreType.DMA(()), pltpu.SemaphoreType.DMA(()),
                            pltpu.SemaphoreType.DMA(())],
        ),
        out_shape=jax.ShapeDtypeStruct((n_slots, D_MODEL), F32),
        compiler_params=_cparams(("arbitrary", "arbitrary")),
        name="experts",
    )(sb_e, sb_start, sb_n, end, xs, w_gate_up, w_gate_up, b_gate_up, b_gate_up, w_down, b_down)


def _combine_kernel(d0_ref, d1_ref, d2_ref, d3_ref, h1_ref, gate_ref, p_ref, yb_ref, wpg_ref, wpp_ref,
                    gp_ref, gf_ref, out_ref, gbuf, sem):
    dest_refs = (d0_ref, d1_ref, d2_ref, d3_ref)
    tm = h1_ref.shape[0]
    step = pl.program_id(0)
    slot = step % 2

    def gather_tile(t, s):
        def issue(r, c):
            for kk in range(TOP_K):
                d = dest_refs[kk][t * tm + r]
                pltpu.make_async_copy(yb_ref.at[pl.ds(d, 1)], gbuf.at[s, kk, pl.ds(r, 1)], sem.at[s]).start()
            return c

        lax.fori_loop(0, tm, issue, 0, unroll=ISSUE_UNROLL)

    @pl.when(step == 0)
    def _():
        gather_tile(0, 0)

    @pl.when(step + 1 < pl.num_programs(0))
    def _():
        gather_tile(step + 1, 1 - slot)

    for kk in range(TOP_K):
        pltpu.make_async_copy(yb_ref.at[pl.ds(0, tm)], gbuf.at[slot, kk], sem.at[slot]).wait()

    gate = gate_ref[...]
    moe = gate[:, 0:1] * gbuf[slot, 0]
    for kk in range(1, TOP_K):
        moe = moe + gate[:, kk:kk + 1] * gbuf[slot, kk]
    h2 = h1_ref[...] + moe
    a = _rms(h2, gp_ref[...]).astype(BF16)
    pg = jax.nn.sigmoid(jnp.dot(a, wpg_ref[...], preferred_element_type=F32))
    pp = jnp.dot(p_ref[...].astype(BF16), wpp_ref[...], preferred_element_type=F32)
    h3 = h2 + pg * pp
    out_ref[...] = _rms(h3, gf_ref[...])


def _combine(dest, h1, gate, p, yb, wpg, wpp, g_ple, g_final, tm):
    n_tok = h1.shape[0]
    full = lambda shape: pl.BlockSpec(shape, lambda i, *_: (0,) * len(shape))
    tile = lambda w: pl.BlockSpec((tm, w), lambda i, *_: (i, 0))
    return pl.pallas_call(
        _combine_kernel,
        grid_spec=pltpu.PrefetchScalarGridSpec(
            num_scalar_prefetch=TOP_K,
            grid=(n_tok // tm,),
            in_specs=[
                tile(D_MODEL), tile(LANES), tile(PLE_DIM),
                pl.BlockSpec(memory_space=pl.ANY),
                full((D_MODEL, D_MODEL)), full((PLE_DIM, D_MODEL)), full((1, D_MODEL)), full((1, D_MODEL)),
            ],
            out_specs=tile(D_MODEL),
            scratch_shapes=[pltpu.VMEM((2, TOP_K, tm, D_MODEL), F32), pltpu.SemaphoreType.DMA((2,))],
        ),
        out_shape=jax.ShapeDtypeStruct((n_tok, D_MODEL), F32),
        compiler_params=_cparams(("arbitrary",)),
        name="combine",
    )(*[dest[k] for k in range(TOP_K)], h1, gate, p, yb, wpg, wpp, g_ple, g_final)


def kernel(x_prompt, x_sample, state_conv, state_mlstm_C, state_mlstm_n, state_mlstm_m, p_prompt, p_sample,
           g_mix, w_in, b_gates, w_conv, g_conv_out, g_mlstm_out, w_out, g_ffn, w_router, b_router,
           w_gate_up, b_gate_up, w_down, b_down, g_ple, w_ple_gate, w_ple_proj, g_final):
    batch, seq, _ = x_prompt.shape
    n_dec = x_sample.shape[0]
    assert w_in.shape[0] == 1 and x_sample.shape[1] == 1
    n_prompt = batch * seq
    n_tok = n_prompt + n_dec

    x_p = x_prompt.reshape(n_prompt, D_MODEL)
    x_s = x_sample.reshape(n_dec, D_MODEL)

    row = lambda a: a.reshape(1, -1)
    pad_lanes = lambda a: jnp.pad(a, ((0, 0), (0, LANES - a.shape[1])))
    w_gates = pad_lanes(w_in[0, :, MAIN_COLS:]).astype(BF16)
    bg = pad_lanes(row(b_gates[0]))
    wc = jnp.pad(w_conv[0], ((0, 8 - w_conv.shape[1]), (0, 0)))
    g_norm = row(g_mlstm_out[0])

    a_p, zg_p = _norm_gates(x_p, row(g_mix[0]), w_gates, tm=1024)
    a_s, zg_s = _norm_gates(x_s, row(g_mix[0]), w_gates, tm=n_dec)
    z_p = _inproj(a_p, w_in, tm=1024, tn=1024)
    z_s = _inproj(a_s, w_in, tm=n_dec, tn=1024)

    yc_p, ym_p, conv_p, c_p, n_p, m_p = _mixer_prompt(
        z_p.reshape(batch, seq, MAIN_COLS), zg_p.reshape(batch, seq, LANES), wc, row(g_conv_out[0]), bg,
        g_norm, nbt=4)
    yc_p = yc_p.reshape(n_prompt, CONV_DIM)
    ym_p = ym_p.reshape(n_prompt, HEADS * DV)
    yc_s, ym_s, conv_s, c_s, n_s, m_s = _mixer_sample(
        z_s, z_s.reshape(n_dec, 1, MAIN_COLS), zg_s.reshape(n_dec, 1, LANES),
        state_conv[0].reshape(n_dec, 2 * CONV_DIM), state_mlstm_C[0], state_mlstm_n[0],
        pad_lanes(state_mlstm_m[0]).reshape(n_dec, 1, LANES), wc, row(g_conv_out[0]), bg, g_norm,
        row0=0, n_dec=n_dec, sb=16)
    m_s = m_s[:, 0, :HEADS]
    ym_s = ym_s.reshape(n_dec, HEADS * DV).astype(BF16)

    wo = w_out[0].astype(BF16)
    wr = pad_lanes(w_router[0])
    wr_hi = wr.astype(BF16)
    wr_lo = (wr - wr_hi.astype(F32)).astype(BF16)
    router_args = (wo[:CONV_DIM], wo[CONV_DIM:], row(g_ffn[0]), wr_hi, wr_lo, pad_lanes(row(b_router[0])))
    h1_p, xn_p, e_p, gate_p, rank_p, cnt_p = _outproj_router(
        yc_p, ym_p, x_p, *router_args, jnp.zeros((1, LANES), F32), tm=512)
    h1_s, xn_s, e_s, gate_s, rank_s, cnt = _outproj_router(
        yc_s, ym_s, x_s, *router_args, cnt_p, tm=n_dec)

    n_assign = n_tok * TOP_K
    tmb = MOE_TMB
    nsb = n_assign // tmb + N_EXPERTS
    n_slots = -(-(n_assign + N_EXPERTS * (SUB - 1)) // SUB) * SUB + tmb
    counts = cnt[0, :N_EXPERTS].astype(jnp.int32)
    padded = (counts + SUB - 1) // SUB * SUB
    pend = jnp.cumsum(padded)
    pstart = jnp.concatenate([jnp.zeros((1,), jnp.int32), pend]).astype(jnp.int32)
    sb_per_e = (counts + tmb - 1) // tmb
    sb_cum = jnp.cumsum(sb_per_e)
    n_used = sb_cum[-1]
    sb = jnp.minimum(jnp.arange(nsb, dtype=jnp.int32), n_used - 1)
    sb_e = jnp.minimum(jnp.sum((sb_cum[None, :] <= sb[:, None]).astype(jnp.int32), axis=1), N_EXPERTS - 1)
    sb_j = sb - (sb_cum - sb_per_e)[sb_e]
    sb_start = (pstart[sb_e] + sb_j * tmb).astype(jnp.int32)
    sb_n = jnp.where(jnp.arange(nsb) < n_used, jnp.clip(counts[sb_e] - sb_j * tmb, 0, tmb), 0).astype(jnp.int32)
    expert_ids = jnp.arange(N_EXPERTS, dtype=jnp.int32)[:, None, None]

    def slot_of(e, r):
        first = jnp.sum(jnp.where(e[None, :TOP_K] == expert_ids, pstart[:N_EXPERTS, None, None], 0), axis=0)
        return (first + r[:TOP_K]).astype(jnp.int32)
    dest_p = slot_of(e_p, rank_p)
    dest_s = slot_of(e_s, rank_s)

    xs = _dispatch(jnp.concatenate([dest_p, dest_s], axis=1), counts, pstart, xn_p, xn_s, n_slots, tm=512)
    yb = _experts(sb_e.astype(jnp.int32), sb_start, sb_n, pstart[N_EXPERTS:], xs, w_gate_up[0],
                  b_gate_up[0].reshape(N_EXPERTS, 1, 2 * D_FF), w_down[0],
                  b_down[0].reshape(N_EXPERTS, 1, D_MODEL), nsb, tmb, MOE_TF)

    ple_args = (w_ple_gate[0].astype(BF16), w_ple_proj[0].astype(BF16), row(g_ple[0]), row(g_final))
    out_p = _combine(dest_p, h1_p, gate_p, p_prompt[0].reshape(n_prompt, PLE_DIM), yb, *ple_args, tm=256)
    out_s = _combine(dest_s, h1_s, gate_s, p_sample[0].reshape(n_dec, PLE_DIM), yb, *ple_args, tm=n_dec)

    y_prompt = out_p.reshape(batch, seq, D_MODEL)
    y_sample = out_s.reshape(n_dec, 1, D_MODEL)
    return (y_prompt, y_sample,
            conv_p[None], c_p[None], n_p[None], m_p[None, :, :, 0],
            conv_s.reshape(1, n_dec, 2, CONV_DIM), c_s[None], n_s[None], m_s[None])
```

```python
import functools

import jax
import jax.numpy as jnp
from jax import lax
from jax.experimental import pallas as pl
from jax.experimental.pallas import tpu as pltpu

F32 = jnp.float32
BF16 = jnp.bfloat16

D_MODEL = 2048
CONV_DIM = 1024
HEADS = 4
DK = 128
DV = 256
CHUNK = 128
N_EXPERTS = 32
TOP_K = 4
D_FF = 2048
PLE_DIM = 256
SWIGLU_LIMIT = 7.0
SWIGLU_ALPHA = 1.702
EPS = 1e-6
LANES = 128
NEG_BIG = -1e30
MAIN_COLS = 3 * CONV_DIM + HEADS * (2 * DK + 2 * DV)

COL_B, COL_C, COL_H = 0, CONV_DIM, 2 * CONV_DIM
COL_Q = 3 * CONV_DIM
COL_K = COL_Q + HEADS * DK
COL_V = COL_K + HEADS * DK
COL_O = COL_V + HEADS * DV

VMEM_LIMIT = 56 * 1024 * 1024

SUB = 128
MOE_TMB = 9 * SUB
MOE_TF = 512
MOE_SPLIT = 4
ISSUE_UNROLL = 8


def _cparams(sem, vmem=VMEM_LIMIT):
    return pltpu.CompilerParams(dimension_semantics=sem, vmem_limit_bytes=vmem)


def _rms(x, g):
    return x * lax.rsqrt(jnp.mean(x * x, axis=-1, keepdims=True) + EPS) * g


def _split3(x):
    x1 = x.astype(BF16)
    r1 = x - x1.astype(F32)
    x2 = r1.astype(BF16)
    x3 = (r1 - x2.astype(F32)).astype(BF16)
    return x1, x2, x3


def _log_sigmoid(x):
    return jnp.minimum(x, 0.0) - jnp.log(1.0 + jnp.exp(-jnp.abs(x)))


def _norm_gates_kernel(x_ref, g_ref, wg_ref, a_ref, zg_ref):
    a = _rms(x_ref[...], g_ref[...]).astype(BF16)
    a_ref[...] = a
    zg_ref[...] = jnp.dot(a, wg_ref[...], preferred_element_type=F32)


def _norm_gates(x, g, w_gates, tm):
    n_tok = x.shape[0]
    return pl.pallas_call(
        _norm_gates_kernel,
        grid=(n_tok // tm,),
        in_specs=[
            pl.BlockSpec((tm, D_MODEL), lambda i: (i, 0)),
            pl.BlockSpec((1, D_MODEL), lambda i: (0, 0)),
            pl.BlockSpec((D_MODEL, LANES), lambda i: (0, 0)),
        ],
        out_specs=[
            pl.BlockSpec((tm, D_MODEL), lambda i: (i, 0)),
            pl.BlockSpec((tm, LANES), lambda i: (i, 0)),
        ],
        out_shape=[
            jax.ShapeDtypeStruct((n_tok, D_MODEL), BF16),
            jax.ShapeDtypeStruct((n_tok, LANES), F32),
        ],
        compiler_params=_cparams(("parallel",)),
        name="norm_gates",
    )(x, g, w_gates)


def _inproj_kernel(a_ref, w_ref, z_ref, w_scr):
    @pl.when(pl.program_id(1) == 0)
    def _():
        w_scr[...] = w_ref[0].astype(BF16)

    z_ref[...] = jnp.dot(a_ref[...], w_scr[...], preferred_element_type=F32)


def _inproj(a, w_in, tm, tn):
    n_tok = a.shape[0]
    return pl.pallas_call(
        _inproj_kernel,
        grid=(MAIN_COLS // tn, n_tok // tm),
        in_specs=[
            pl.BlockSpec((tm, D_MODEL), lambda j, i: (i, 0)),
            pl.BlockSpec((1, D_MODEL, tn), lambda j, i: (0, 0, j)),
        ],
        out_specs=pl.BlockSpec((tm, tn), lambda j, i: (i, j)),
        out_shape=jax.ShapeDtypeStruct((n_tok, MAIN_COLS), F32),
        scratch_shapes=[pltpu.VMEM((D_MODEL, tn), BF16)],
        compiler_params=_cparams(("arbitrary", "arbitrary")),
        name="inproj",
    )(a, w_in)


def _conv_chunk(zb, zc, zh, prev, wc, g):
    tt = zb.shape[0]
    u = zc * zh
    row = lax.broadcasted_iota(jnp.int32, (tt, 1), 0)
    u1 = jnp.where(row == 0, prev[7:8], pltpu.roll(u, 1, axis=0))
    u2 = jnp.where(row == 0, prev[6:7], jnp.where(row == 1, prev[7:8], pltpu.roll(u, 2, axis=0)))
    conv = wc[0:1] * u2 + wc[1:2] * u1 + wc[2:3] * u
    return _rms(zb * conv, g).astype(BF16), u


def _tri_dot(tri_bf16, x, tri_first):
    out = None
    for part in _split3(x):
        d = (jnp.dot(tri_bf16, part, preferred_element_type=F32) if tri_first
             else jnp.dot(part, tri_bf16, preferred_element_type=F32))
        out = d if out is None else out + d
    return out


def _mixer_prompt_kernel(zb_ref, zc_ref, zh_ref, q_ref, k_ref, v_ref, o_ref, zg_ref,
                         wc_ref, gc_ref, bg_ref, gn_ref,
                         yc_ref, y_ref, st_out, c_out, n_out, m_out, carry, c_scr, n_scr, m_scr):
    L = CHUNK

    @pl.when(pl.program_id(1) == 0)
    def _():
        carry[...] = jnp.zeros_like(carry)
        c_scr[...] = jnp.zeros_like(c_scr)
        n_scr[...] = jnp.zeros_like(n_scr)
        m_scr[...] = jnp.zeros_like(m_scr)

    lane = lax.broadcasted_iota(jnp.int32, (L, LANES), 1)
    r_i = lax.broadcasted_iota(jnp.int32, (L, L), 0)
    c_i = lax.broadcasted_iota(jnp.int32, (L, L), 1)
    causal = c_i <= r_i
    tril = causal.astype(BF16)
    triu = (r_i <= c_i).astype(BF16)

    nbt = q_ref.shape[0]
    for bi in range(nbt):
        yc, u = _conv_chunk(zb_ref[bi], zc_ref[bi], zh_ref[bi], carry[bi], wc_ref[...], gc_ref[...])
        yc_ref[bi] = yc
        carry[bi] = u[L - 8:L]
        st_out[bi] = u[L - 2:L]

    prs = [(bi, h) for bi in range(nbt) for h in range(HEADS)]
    each = lambda fn: {pr: fn(*pr) for pr in prs}
    nt_dims = (((1,), (1,)), ((), ()))
    tn_dims = (((0,), (0,)), ((), ()))

    gates, gates_t, bcol_all, brow_all = {}, {}, {}, {}
    for bi in range(nbt):
        zg = zg_ref[bi] + bg_ref[...]
        gates[bi] = jnp.where(lane < HEADS, zg, _log_sigmoid(zg))
        gates_t[bi] = gates[bi].T
        bcol_all[bi] = _tri_dot(tril, gates[bi], True)
        brow_all[bi] = _tri_dot(triu, gates_t[bi], False)

    b_col = each(lambda bi, h: bcol_all[bi][:, HEADS + h:HEADS + h + 1])
    b_row = each(lambda bi, h: brow_all[bi][HEADS + h:HEADS + h + 1, :])
    i_col = each(lambda bi, h: gates[bi][:, h:h + 1])
    i_row = each(lambda bi, h: gates_t[bi][h:h + 1, :])
    b_end = each(lambda bi, h: b_col[bi, h][L - 1:L, :])
    m0 = each(lambda bi, h: m_scr[bi, h:h + 1, 0:1])
    n0 = each(lambda bi, h: n_scr[bi, h:h + 1, :])
    c0 = each(lambda bi, h: c_scr[bi, h])
    q = each(lambda bi, h: q_ref[bi, :, h * DK:(h + 1) * DK])
    k = each(lambda bi, h: k_ref[bi, :, h * DK:(h + 1) * DK] * (DK ** -0.5))
    qb = each(lambda bi, h: q[bi, h].astype(BF16))
    vb = each(lambda bi, h: v_ref[bi, :, h * DV:(h + 1) * DV].astype(BF16))

    dmat = each(lambda bi, h: jnp.where(causal, b_col[bi, h] - b_row[bi, h] + i_row[bi, h], NEG_BIG))
    inter = each(lambda bi, h: b_col[bi, h] + m0[bi, h])
    m_t = each(lambda bi, h: jnp.maximum(inter[bi, h], jnp.max(dmat[bi, h], axis=1, keepdims=True)))
    w_int = each(lambda bi, h: jnp.exp(inter[bi, h] - m_t[bi, h]))
    p = each(lambda bi, h: jnp.where(causal, jnp.exp(dmat[bi, h] - m_t[bi, h]), 0.0))
    s = each(lambda bi, h: lax.dot_general(qb[bi, h], k[bi, h].astype(BF16), nt_dims,
                                           preferred_element_type=F32) * p[bi, h])
    num = each(lambda bi, h: jnp.dot(s[bi, h].astype(BF16), vb[bi, h], preferred_element_type=F32)
               + w_int[bi, h] * jnp.dot(qb[bi, h], c0[bi, h].astype(BF16), preferred_element_type=F32))
    den = each(lambda bi, h: jnp.sum(s[bi, h], axis=1, keepdims=True)
               + w_int[bi, h] * jnp.sum(q[bi, h] * n0[bi, h], axis=1, keepdims=True))
    hh = each(lambda bi, h: num[bi, h] / jnp.maximum(jnp.abs(den[bi, h]), jnp.exp(-m_t[bi, h])))
    hn = each(lambda bi, h: hh[bi, h] * lax.rsqrt(jnp.mean(hh[bi, h] * hh[bi, h], axis=-1, keepdims=True)
                                                  + EPS) * gn_ref[:, h * DV:(h + 1) * DV])
    for bi, h in prs:
        y = jax.nn.sigmoid(o_ref[bi, :, h * DV:(h + 1) * DV]) * hn[bi, h]
        y_ref[bi, :, h * DV:(h + 1) * DV] = y.astype(BF16)

    g_col = each(lambda bi, h: b_end[bi, h] - b_col[bi, h] + i_col[bi, h])
    g_row = each(lambda bi, h: b_end[bi, h] - b_row[bi, h] + i_row[bi, h])
    m_new = each(lambda bi, h: jnp.maximum(b_end[bi, h] + m0[bi, h],
                                           jnp.max(g_row[bi, h], axis=1, keepdims=True)))
    decay = each(lambda bi, h: jnp.exp(b_end[bi, h] + m0[bi, h] - m_new[bi, h]))
    kw = each(lambda bi, h: k[bi, h] * jnp.exp(g_col[bi, h] - m_new[bi, h]))
    c_new = each(lambda bi, h: decay[bi, h] * c0[bi, h]
                 + lax.dot_general(kw[bi, h].astype(BF16), vb[bi, h], tn_dims, preferred_element_type=F32))
    n_new = each(lambda bi, h: decay[bi, h] * n0[bi, h] + jnp.sum(kw[bi, h], axis=0, keepdims=True))
    for bi, h in prs:
        c_scr[bi, h] = c_new[bi, h]
        n_scr[bi, h:h + 1, :] = n_new[bi, h]
        m_scr[bi, h:h + 1, :] = jnp.broadcast_to(m_new[bi, h], (1, LANES))
        c_out[bi, h] = c_new[bi, h]
        n_out[bi, h:h + 1, :] = n_new[bi, h]
        m_out[bi, h:h + 1, :] = jnp.broadcast_to(m_new[bi, h], (1, LANES))


def _mixer_prompt(z3, zg3, w_conv, g_conv, b_gates, g_norm, nbt):
    batch, seq, _ = z3.shape
    nc = seq // CHUNK
    cb, qw, vw = CONV_DIM, HEADS * DK, HEADS * DV
    zcol = lambda w, col: pl.BlockSpec((nbt, CHUNK, w), lambda b, c: (b, c, col // w))
    const = lambda shape: pl.BlockSpec(shape, lambda b, c: (0,) * len(shape))
    per_seq = lambda *tail: pl.BlockSpec((nbt,) + tail, lambda b, c: (b,) + (0,) * len(tail))
    return pl.pallas_call(
        _mixer_prompt_kernel,
        grid=(batch // nbt, nc),
        in_specs=[
            zcol(cb, COL_B), zcol(cb, COL_C), zcol(cb, COL_H),
            zcol(qw, COL_Q), zcol(qw, COL_K), zcol(vw, COL_V), zcol(vw, COL_O),
            pl.BlockSpec((nbt, CHUNK, LANES), lambda b, c: (b, c, 0)),
            const((8, cb)), const((1, cb)), const((1, LANES)), const((1, vw)),
        ],
        out_specs=[
            pl.BlockSpec((nbt, CHUNK, cb), lambda b, c: (b, c, 0)),
            pl.BlockSpec((nbt, CHUNK, vw), lambda b, c: (b, c, 0)),
            per_seq(2, cb), per_seq(HEADS, DK, DV), per_seq(HEADS, DK), per_seq(HEADS, LANES),
        ],
        out_shape=[
            jax.ShapeDtypeStruct((batch, seq, cb), BF16),
            jax.ShapeDtypeStruct((batch, seq, vw), BF16),
            jax.ShapeDtypeStruct((batch, 2, cb), F32),
            jax.ShapeDtypeStruct((batch, HEADS, DK, DV), F32),
            jax.ShapeDtypeStruct((batch, HEADS, DK), F32),
            jax.ShapeDtypeStruct((batch, HEADS, LANES), F32),
        ],
        scratch_shapes=[
            pltpu.VMEM((nbt, 8, cb), F32),
            pltpu.VMEM((nbt, HEADS, DK, DV), F32),
            pltpu.VMEM((nbt, 8, DK), F32),
            pltpu.VMEM((nbt, 8, LANES), F32),
        ],
        compiler_params=_cparams(("parallel", "arbitrary")),
        name="mixer_prompt",
    )(z3, z3, z3, z3, z3, z3, z3, zg3, w_conv, g_conv, b_gates, g_norm)


def _mixer_sample_kernel(zb_ref, zc_ref, zh_ref, q_ref, k_ref, v_ref, o_ref, zg_ref,
                         st_ref, c_ref, n_ref, m_ref, wc_ref, gc_ref, bg_ref, gn_ref,
                         yc_ref, ym_ref, st_out, c_out, n_out, m_out):
    sb = zb_ref.shape[0]
    u = zc_ref[...] * zh_ref[...]
    s0 = st_ref[:, 0:CONV_DIM]
    s1 = st_ref[:, CONV_DIM:2 * CONV_DIM]
    wc = wc_ref[...]
    conv = wc[0:1] * s0 + wc[1:2] * s1 + wc[2:3] * u
    yc_ref[...] = _rms(zb_ref[...] * conv, gc_ref[...]).astype(BF16)
    st_out[:, 0:CONV_DIM] = s1
    st_out[:, CONV_DIM:2 * CONV_DIM] = u

    lane_row = lax.broadcasted_iota(jnp.int32, (1, LANES), 1)
    eye = (lax.broadcasted_iota(jnp.int32, (DK, DK), 0)
           == lax.broadcasted_iota(jnp.int32, (DK, DK), 1)).astype(F32)

    def to_col(row):
        return jnp.sum(eye * row, axis=1, keepdims=True)

    def pick(row, j):
        return jnp.sum(jnp.where(lane_row == j, row, 0.0), axis=1, keepdims=True)

    def per_sample(s, carry):
        zg = zg_ref[s] + bg_ref[...]
        lf_all = _log_sigmoid(zg)
        m_row = m_ref[s]
        m_new_row = jnp.zeros((1, LANES), F32)
        for h in range(HEADS):
            q = q_ref[s, :, h * DK:(h + 1) * DK]
            k = k_ref[s, :, h * DK:(h + 1) * DK] * (DK ** -0.5)
            v = v_ref[s, :, h * DV:(h + 1) * DV]
            o = o_ref[s, :, h * DV:(h + 1) * DV]
            i_pre = pick(zg, h)
            lf = pick(lf_all, HEADS + h)
            m0 = pick(m_row, h)
            c0 = c_ref[s, h]
            n0 = n_ref[s, h:h + 1, :]
            inter = lf + m0
            m_t = jnp.maximum(inter, i_pre)
            w_int = jnp.exp(inter - m_t)
            p = jnp.exp(i_pre - m_t)
            sc = jnp.sum(q * k, axis=1, keepdims=True) * p
            qc = jnp.sum(to_col(q) * c0, axis=0, keepdims=True)
            num = sc * v + w_int * qc
            den = sc + w_int * jnp.sum(q * n0, axis=1, keepdims=True)
            hh = num / jnp.maximum(jnp.abs(den), jnp.exp(-m_t))
            hn = hh * lax.rsqrt(jnp.mean(hh * hh, axis=-1, keepdims=True) + EPS)
            hn = hn * gn_ref[:, h * DV:(h + 1) * DV]
            ym_ref[s, :, h * DV:(h + 1) * DV] = jax.nn.sigmoid(o) * hn
            c_out[s, h] = w_int * c0 + (p * to_col(k)) * v
            n_out[s, h:h + 1, :] = w_int * n0 + p * k
            m_new_row = jnp.where(lane_row == h, m_t, m_new_row)
        m_out[s] = m_new_row
        return carry

    lax.fori_loop(0, sb, per_sample, 0)


def _mixer_sample(z, zs3, zg3, st, c_s, n_s, m3, w_conv, g_conv, b_gates, g_norm, row0, n_dec, sb):
    cb, qw, vw = CONV_DIM, HEADS * DK, HEADS * DV
    r0 = row0 // sb
    row = lambda i: r0 + i
    return pl.pallas_call(
        _mixer_sample_kernel,
        grid=(n_dec // sb,),
        in_specs=[
            pl.BlockSpec((sb, cb), lambda i: (row(i), COL_B // cb)),
            pl.BlockSpec((sb, cb), lambda i: (row(i), COL_C // cb)),
            pl.BlockSpec((sb, cb), lambda i: (row(i), COL_H // cb)),
            pl.BlockSpec((sb, 1, qw), lambda i: (i, 0, COL_Q // qw)),
            pl.BlockSpec((sb, 1, qw), lambda i: (i, 0, COL_K // qw)),
            pl.BlockSpec((sb, 1, vw), lambda i: (i, 0, COL_V // vw)),
            pl.BlockSpec((sb, 1, vw), lambda i: (i, 0, COL_O // vw)),
            pl.BlockSpec((sb, 1, LANES), lambda i: (i, 0, 0)),
            pl.BlockSpec((sb, 2 * cb), lambda i: (i, 0)),
            pl.BlockSpec((sb, HEADS, DK, DV), lambda i: (i, 0, 0, 0)),
            pl.BlockSpec((sb, HEADS, DK), lambda i: (i, 0, 0)),
            pl.BlockSpec((sb, 1, LANES), lambda i: (i, 0, 0)),
            pl.BlockSpec((8, cb), lambda i: (0, 0)),
            pl.BlockSpec((1, cb), lambda i: (0, 0)),
            pl.BlockSpec((1, LANES), lambda i: (0, 0)),
            pl.BlockSpec((1, vw), lambda i: (0, 0)),
        ],
        out_specs=[
            pl.BlockSpec((sb, cb), lambda i: (i, 0)),
            pl.BlockSpec((sb, 1, vw), lambda i: (i, 0, 0)),
            pl.BlockSpec((sb, 2 * cb), lambda i: (i, 0)),
            pl.BlockSpec((sb, HEADS, DK, DV), lambda i: (i, 0, 0, 0)),
            pl.BlockSpec((sb, HEADS, DK), lambda i: (i, 0, 0)),
            pl.BlockSpec((sb, 1, LANES), lambda i: (i, 0, 0)),
        ],
        out_shape=[
            jax.ShapeDtypeStruct((n_dec, cb), BF16),
            jax.ShapeDtypeStruct((n_dec, 1, vw), F32),
            jax.ShapeDtypeStruct((n_dec, 2 * cb), F32),
            jax.ShapeDtypeStruct((n_dec, HEADS, DK, DV), F32),
            jax.ShapeDtypeStruct((n_dec, HEADS, DK), F32),
            jax.ShapeDtypeStruct((n_dec, 1, LANES), F32),
        ],
        compiler_params=_cparams(("arbitrary",)),
        name="mixer_sample",
    )(z, z, z, zs3, zs3, zs3, zs3, zg3, st, c_s, n_s, m3, w_conv, g_conv, b_gates, g_norm)


def _outproj_router_kernel(yc_ref, ym_ref, x_ref, wo1_ref, wo2_ref, g_ref, wr_hi_ref, wr_lo_ref, br_ref,
                           cin_ref, h1_ref, xn_ref, e_ref, gate_ref, rank_ref, cnt_ref, carry):
    tm = x_ref.shape[0]

    @pl.when(pl.program_id(0) == 0)
    def _():
        carry[...] = cin_ref[...]

    mix = (jnp.dot(yc_ref[...], wo1_ref[...], preferred_element_type=F32)
           + jnp.dot(ym_ref[...], wo2_ref[...], preferred_element_type=F32))
    h1 = x_ref[...] + mix
    h1_ref[...] = h1
    xn = _rms(h1, g_ref[...])
    xn_ref[...] = xn

    xh = xn.astype(BF16)
    xl = (xn - xh.astype(F32)).astype(BF16)
    logits = (jnp.dot(xh, wr_hi_ref[...], preferred_element_type=F32)
              + jnp.dot(xl, wr_hi_ref[...], preferred_element_type=F32)
              + jnp.dot(xh, wr_lo_ref[...], preferred_element_type=F32)) + br_ref[...]
    lane = lax.broadcasted_iota(jnp.int32, (tm, LANES), 1)
    lane_f = lane.astype(F32)
    work = jnp.where(lane < N_EXPERTS, logits, NEG_BIG)

    tops, idxs = [], []
    chosen = jnp.zeros((tm, LANES), F32)
    for _ in range(TOP_K):
        mx = jnp.max(work, axis=1, keepdims=True)
        idx = jnp.min(jnp.where(work == mx, lane_f, float(LANES)), axis=1, keepdims=True)
        sel = lane_f == idx
        tops.append(mx)
        idxs.append(idx)
        chosen = jnp.where(sel, 1.0, chosen)
        work = jnp.where(sel, NEG_BIG, work)

    exps = [jnp.exp(t - tops[0]) for t in tops]
    denom = exps[0] + exps[1] + exps[2] + exps[3]

    r_i = lax.broadcasted_iota(jnp.int32, (tm, tm), 0)
    c_i = lax.broadcasted_iota(jnp.int32, (tm, tm), 1)
    before = jnp.dot((c_i < r_i).astype(BF16), chosen.astype(BF16),
                     preferred_element_type=F32) + carry[...]

    e_out = jnp.zeros((tm, LANES), F32)
    g_out = jnp.zeros((tm, LANES), F32)
    r_out = jnp.zeros((tm, LANES), F32)
    for kk in range(TOP_K):
        rank = jnp.sum(jnp.where(lane_f == idxs[kk], before, 0.0), axis=1, keepdims=True)
        e_out = jnp.where(lane == kk, idxs[kk], e_out)
        g_out = jnp.where(lane == kk, exps[kk] / denom, g_out)
        r_out = jnp.where(lane == kk, rank, r_out)
    gate_ref[...] = g_out
    for c in range(tm // LANES):
        rows = slice(c * LANES, (c + 1) * LANES)
        e_ref[:, rows] = e_out[rows].T[0:8].astype(jnp.int32)
        rank_ref[:, rows] = r_out[rows].T[0:8].astype(jnp.int32)

    new_carry = carry[...] + jnp.sum(chosen, axis=0, keepdims=True)
    carry[...] = new_carry
    cnt_ref[...] = new_carry


def _outproj_router(y_conv, y_mlstm, x, wo1, wo2, g_ffn, wr_hi, wr_lo, b_router, cnt_in, tm):
    n_tok = x.shape[0]
    full = lambda shape: pl.BlockSpec(shape, lambda i: (0,) * len(shape))
    tile = lambda w: pl.BlockSpec((tm, w), lambda i: (i, 0))
    choice_major = pl.BlockSpec((8, tm), lambda i: (0, i))
    return pl.pallas_call(
        _outproj_router_kernel,
        grid=(n_tok // tm,),
        in_specs=[
            tile(CONV_DIM), tile(HEADS * DV), tile(D_MODEL),
            full((CONV_DIM, D_MODEL)), full((HEADS * DV, D_MODEL)), full((1, D_MODEL)),
            full((D_MODEL, LANES)), full((D_MODEL, LANES)), full((1, LANES)), full((1, LANES)),
        ],
        out_specs=[tile(D_MODEL), tile(D_MODEL), choice_major, tile(LANES), choice_major, full((1, LANES))],
        out_shape=[
            jax.ShapeDtypeStruct((n_tok, D_MODEL), F32),
            jax.ShapeDtypeStruct((n_tok, D_MODEL), F32),
            jax.ShapeDtypeStruct((8, n_tok), jnp.int32),
            jax.ShapeDtypeStruct((n_tok, LANES), F32),
            jax.ShapeDtypeStruct((8, n_tok), jnp.int32),
            jax.ShapeDtypeStruct((1, LANES), F32),
        ],
        scratch_shapes=[pltpu.VMEM((1, LANES), F32)],
        compiler_params=_cparams(("arbitrary",)),
        name="outproj_router",
    )(y_conv, y_mlstm, x, wo1, wo2, g_ffn, wr_hi, wr_lo, b_router, cnt_in)


def _zero_fill_rows(dst, zero_scr, sem, start, count, act):
    low = count & 7
    for t in range(7):
        @pl.when(t < low)
        def _(t=t):
            act(pltpu.make_async_copy(zero_scr.at[pl.ds(0, 1)], dst.at[pl.ds(start + t, 1)], sem))

    off = start + low
    bit = 8
    while bit < SUB:
        take = count & bit

        @pl.when(take != 0)
        def _(off=off, bit=bit):
            row = pl.multiple_of(off, 8)
            act(pltpu.make_async_copy(zero_scr.at[pl.ds(0, bit)], dst.at[pl.ds(row, bit)], sem))

        off = off + take
        bit *= 2


def _zero_fill_chunks(dst, zero_scr, sem, start, act):
    def body(j, c):
        row = pl.multiple_of(start + j * SUB, SUB)
        act(pltpu.make_async_copy(zero_scr, dst.at[pl.ds(row, SUB)], sem))
        return c

    lax.fori_loop(0, (dst.shape[0] - start) // SUB, body, 0)


def _dispatch_kernel(d0_ref, d1_ref, d2_ref, d3_ref, cnt_ref, pstart_ref, xn_ref, xn_s_ref, xs_out,
                     zero_scr, sem, zsem):
    dest_refs = (d0_ref, d1_ref, d2_ref, d3_ref)
    tm = xn_ref.shape[0]
    step = pl.program_id(0)
    last = pl.num_programs(0) - 1

    def fill(act):
        for e in range(N_EXPERTS):
            c = cnt_ref[e]
            _zero_fill_rows(xs_out, zero_scr, zsem, pstart_ref[e] + c, (SUB - c % SUB) % SUB, act)
        _zero_fill_chunks(xs_out, zero_scr, zsem, pstart_ref[N_EXPERTS], act)

    @pl.when(step == 0)
    def _():
        zero_scr[...] = jnp.zeros_like(zero_scr)
        fill(lambda cp: cp.start())

    def scatter_rows(src, base):
        rows = src.shape[0]

        def issue(r, c):
            for kk in range(TOP_K):
                d = dest_refs[kk][base + r]
                pltpu.make_async_copy(src.at[pl.ds(r, 1)], xs_out.at[pl.ds(d, 1)], sem).start()
            return c

        lax.fori_loop(0, rows, issue, 0, unroll=ISSUE_UNROLL)
        for _ in range(TOP_K):
            pltpu.make_async_copy(src, xs_out.at[pl.ds(0, rows)], sem).wait()

    scatter_rows(xn_ref, step * tm)

    @pl.when(step == last)
    def _():
        scatter_rows(xn_s_ref, pl.num_programs(0) * tm)

    @pl.when(step == 0)
    def _():
        fill(lambda cp: cp.wait())


def _dispatch(dest, counts, pstart, xn_p, xn_s, n_slots, tm):
    n_prompt = xn_p.shape[0]
    return pl.pallas_call(
        _dispatch_kernel,
        grid_spec=pltpu.PrefetchScalarGridSpec(
            num_scalar_prefetch=TOP_K + 2,
            grid=(n_prompt // tm,),
            in_specs=[pl.BlockSpec((tm, D_MODEL), lambda i, *_: (i, 0)),
                      pl.BlockSpec(xn_s.shape, lambda i, *_: (0, 0))],
            out_specs=pl.BlockSpec(memory_space=pl.ANY),
            scratch_shapes=[pltpu.VMEM((SUB, D_MODEL), F32), pltpu.SemaphoreType.DMA(()),
                            pltpu.SemaphoreType.DMA(())],
        ),
        out_shape=jax.ShapeDtypeStruct((n_slots, D_MODEL), F32),
        compiler_params=_cparams(("arbitrary",)),
        name="dispatch",
    )(*[dest[k] for k in range(TOP_K)], counts, pstart, xn_p, xn_s)


def _expert_kernel(se_ref, st_ref, sn_ref, end_ref, xs_ref, wg_ref, wu_ref, bg_ref, bu_ref, wd_ref, bd_ref,
                   y_ref, x_scr, xb_scr, acc, zero_scr, sem, zsem, xsem):
    i, f = pl.program_id(0), pl.program_id(1)
    nf = pl.num_programs(1)
    tmb = acc.shape[0]
    n = sn_ref[i]
    start = st_ref[i]

    def x_copy(j):
        row = pl.multiple_of(st_ref[j], SUB)
        return pltpu.make_async_copy(xs_ref.at[pl.ds(row, tmb)], x_scr, xsem)

    @pl.when((i == 0) & (f == 0))
    def _():
        zero_scr[...] = jnp.zeros_like(zero_scr)
        _zero_fill_chunks(y_ref, zero_scr, zsem, end_ref[0], lambda cp: cp.start())
        _zero_fill_chunks(y_ref, zero_scr, zsem, end_ref[0], lambda cp: cp.wait())

    kc = D_MODEL // MOE_SPLIT

    def y_copy(o, mc, c):
        row = pl.multiple_of(start + o, SUB)
        cols = pl.ds(c * kc, kc)
        return pltpu.make_async_copy(acc.at[pl.ds(o, mc), cols], y_ref.at[pl.ds(row, mc), cols], sem)

    def for_each_sub_block(fn):
        kmax = tmb // SUB
        half = SUB // 2
        nh = (n + half - 1) // half
        for kh in range(2 * kmax, 2 * kmax - 4, -1):
            pl.when(nh == kh)(functools.partial(fn, 0, kh * half, -(-kh // 2) * SUB))
        nb = (n + SUB - 1) // SUB
        bits = [b for b in (64, 32, 16, 8, 4, 2, 1) if b <= kmax - 2]
        small = nh <= 2 * kmax - 4
        for b in bits:
            higher = [h for h in bits if h > b]
            for mask in range(1 << len(higher)):
                above = sum(h for j, h in enumerate(higher) if (mask >> j) & 1)
                if above + b > kmax - 2:
                    continue
                cond = small & ((nb & b) != 0) & ((nb & sum(higher)) == above)
                pl.when(cond)(functools.partial(fn, above * SUB, b * SUB, b * SUB))

    @pl.when(n > 0)
    def _():
        @pl.when(f == 0)
        def _():
            @pl.when(i == 0)
            def _():
                x_copy(0).start()

            x_copy(i).wait()
            xb_scr[...] = x_scr[...].astype(BF16)
            nxt = jnp.minimum(i + 1, pl.num_programs(0) - 1)

            @pl.when((i + 1 < pl.num_programs(0)) & (sn_ref[nxt] > 0))
            def _():
                x_copy(nxt).start()

            acc[...] = jnp.broadcast_to(bd_ref[0], acc.shape)

        def sub_block(o, m, mc):
            g = bg_ref[0]
            u = bu_ref[0]
            for c in range(MOE_SPLIT):
                xb = xb_scr[o:o + m, c * kc:(c + 1) * kc]
                g = g + jnp.dot(xb, wg_ref[0, c * kc:(c + 1) * kc, :].astype(BF16), preferred_element_type=F32)
                u = u + jnp.dot(xb, wu_ref[0, c * kc:(c + 1) * kc, :].astype(BF16), preferred_element_type=F32)
            g = jnp.minimum(g, SWIGLU_LIMIT)
            u = jnp.clip(u, -SWIGLU_LIMIT, SWIGLU_LIMIT)
            act = (g * jax.nn.sigmoid(SWIGLU_ALPHA * g) * (u + 1.0)).astype(BF16)
            for c in range(MOE_SPLIT):
                cols = slice(c * kc, (c + 1) * kc)
                acc[o:o + m, cols] += jnp.dot(act, wd_ref[0, :, cols].astype(BF16), preferred_element_type=F32)

                @pl.when(f == nf - 1)
                def _(c=c):
                    y_copy(o, mc, c).start()

        for_each_sub_block(sub_block)

        @pl.when(f == nf - 1)
        def _():
            def drain(o, m, mc):
                for c in range(MOE_SPLIT):
                    y_copy(o, mc, c).wait()

            for_each_sub_block(drain)


def _experts(sb_e, sb_start, sb_n, end, xs, w_gate_up, b_gate_up, w_down, b_down, nsb, tmb, tf):
    n_slots = xs.shape[0]
    nf = D_FF // tf

    def fidx(i, f, sn):
        return jnp.where(sn[i] > 0, f, nf - 1)

    return pl.pallas_call(
        _expert_kernel,
        grid_spec=pltpu.PrefetchScalarGridSpec(
            num_scalar_prefetch=4,
            grid=(nsb, nf),
            in_specs=[
                pl.BlockSpec(memory_space=pl.ANY),
                pl.BlockSpec((1, D_MODEL, tf), lambda i, f, se, st, sn, en: (se[i], 0, fidx(i, f, sn))),
                pl.BlockSpec((1, D_MODEL, tf), lambda i, f, se, st, sn, en: (se[i], 0, nf + fidx(i, f, sn))),
                pl.BlockSpec((1, 1, tf), lambda i, f, se, st, sn, en: (se[i], 0, fidx(i, f, sn))),
                pl.BlockSpec((1, 1, tf), lambda i, f, se, st, sn, en: (se[i], 0, nf + fidx(i, f, sn))),
                pl.BlockSpec((1, tf, D_MODEL), lambda i, f, se, st, sn, en: (se[i], fidx(i, f, sn), 0)),
                pl.BlockSpec((1, 1, D_MODEL), lambda i, f, se, st, sn, en: (se[i], 0, 0)),
            ],
            out_specs=pl.BlockSpec(memory_space=pl.ANY),
            scratch_shapes=[pltpu.VMEM((tmb, D_MODEL), F32), pltpu.VMEM((tmb, D_MODEL), BF16),
                            pltpu.VMEM((tmb, D_MODEL), F32), pltpu.VMEM((SUB, D_MODEL), F32),
                            pltpu.SemaphoreType.DMA(()), pltpu.SemaphoreType.DMA(()),
                            pltpu.SemaphoreType.DMA(())],
        ),
        out_shape=jax.ShapeDtypeStruct((n_slots, D_MODEL), F32),
        compiler_params=_cparams(("arbitrary", "arbitrary")),
        name="experts",
    )(sb_e, sb_start, sb_n, end, xs, w_gate_up, w_gate_up, b_gate_up, b_gate_up, w_down, b_down)


def _combine_kernel(d0_ref, d1_ref, d2_ref, d3_ref, h1_ref, gate_ref, p_ref, yb_ref, wpg_ref, wpp_ref,
                    gp_ref, gf_ref, out_ref, gbuf_a, gbuf_b, sem):
    dest_refs = (d0_ref, d1_ref, d2_ref, d3_ref)
    tm = h1_ref.shape[0]
    step = pl.program_id(0)
    last = pl.num_programs(0) - 1
    bufs = (gbuf_a, gbuf_b)

    def row_gather(t, r, kk, s):
        d = dest_refs[kk][t * tm + r]
        return pltpu.make_async_copy(yb_ref.at[pl.ds(d, 1)], bufs[s].at[kk, pl.ds(r, 1)], sem.at[s])

    def wait_tile(s):
        for kk in range(TOP_K):
            pltpu.make_async_copy(yb_ref.at[pl.ds(0, tm)], bufs[s].at[kk], sem.at[s]).wait()

    @pl.when(step == 0)
    def _():
        def issue(r, c):
            for kk in range(TOP_K):
                row_gather(0, r, kk, 0).start()
            return c

        lax.fori_loop(0, tm, issue, 0, unroll=ISSUE_UNROLL)

    def tile_step(cur):
        nxt = 1 - cur
        wait_tile(cur)
        t_next = jnp.minimum(step + 1, last)
        for r in range(tm):
            for kk in range(TOP_K):
                row_gather(t_next, r, kk, nxt).start()

        gate = gate_ref[...]
        moe = gate[:, 0:1] * bufs[cur][0]
        for kk in range(1, TOP_K):
            moe = moe + gate[:, kk:kk + 1] * bufs[cur][kk]
        h2 = h1_ref[...] + moe
        a = _rms(h2, gp_ref[...]).astype(BF16)
        pg = jax.nn.sigmoid(jnp.dot(a, wpg_ref[...], preferred_element_type=F32))
        pp = jnp.dot(p_ref[...].astype(BF16), wpp_ref[...], preferred_element_type=F32)
        h3 = h2 + pg * pp
        out_ref[...] = _rms(h3, gf_ref[...])

        @pl.when(step == last)
        def _():
            wait_tile(nxt)

    pl.when(step % 2 == 0)(functools.partial(tile_step, 0))
    pl.when(step % 2 == 1)(functools.partial(tile_step, 1))


def _combine(dest, h1, gate, p, yb, wpg, wpp, g_ple, g_final, tm):
    n_tok = h1.shape[0]
    full = lambda shape: pl.BlockSpec(shape, lambda i, *_: (0,) * len(shape))
    tile = lambda w: pl.BlockSpec((tm, w), lambda i, *_: (i, 0))
    return pl.pallas_call(
        _combine_kernel,
        grid_spec=pltpu.PrefetchScalarGridSpec(
            num_scalar_prefetch=TOP_K,
            grid=(n_tok // tm,),
            in_specs=[
                tile(D_MODEL), tile(LANES), tile(PLE_DIM),
                pl.BlockSpec(memory_space=pl.ANY),
                full((D_MODEL, D_MODEL)), full((PLE_DIM, D_MODEL)), full((1, D_MODEL)), full((1, D_MODEL)),
            ],
            out_specs=tile(D_MODEL),
            scratch_shapes=[pltpu.VMEM((TOP_K, tm, D_MODEL), F32), pltpu.VMEM((TOP_K, tm, D_MODEL), F32),
                            pltpu.SemaphoreType.DMA((2,))],
        ),
        out_shape=jax.ShapeDtypeStruct((n_tok, D_MODEL), F32),
        compiler_params=_cparams(("arbitrary",)),
        name="combine",
    )(*[dest[k] for k in range(TOP_K)], h1, gate, p, yb, wpg, wpp, g_ple, g_final)


def kernel(x_prompt, x_sample, state_conv, state_mlstm_C, state_mlstm_n, state_mlstm_m, p_prompt, p_sample,
           g_mix, w_in, b_gates, w_conv, g_conv_out, g_mlstm_out, w_out, g_ffn, w_router, b_router,
           w_gate_up, b_gate_up, w_down, b_down, g_ple, w_ple_gate, w_ple_proj, g_final):
    batch, seq, _ = x_prompt.shape
    n_dec = x_sample.shape[0]
    assert w_in.shape[0] == 1 and x_sample.shape[1] == 1
    n_prompt = batch * seq
    n_tok = n_prompt + n_dec

    x_p = x_prompt.reshape(n_prompt, D_MODEL)
    x_s = x_sample.reshape(n_dec, D_MODEL)

    row = lambda a: a.reshape(1, -1)
    pad_lanes = lambda a: jnp.pad(a, ((0, 0), (0, LANES - a.shape[1])))
    w_gates = pad_lanes(w_in[0, :, MAIN_COLS:]).astype(BF16)
    bg = pad_lanes(row(b_gates[0]))
    wc = jnp.pad(w_conv[0], ((0, 8 - w_conv.shape[1]), (0, 0)))
    g_norm = row(g_mlstm_out[0])

    a_p, zg_p = _norm_gates(x_p, row(g_mix[0]), w_gates, tm=1024)
    a_s, zg_s = _norm_gates(x_s, row(g_mix[0]), w_gates, tm=n_dec)
    z_p = _inproj(a_p, w_in, tm=1024, tn=1024)
    z_s = _inproj(a_s, w_in, tm=n_dec, tn=1024)

    yc_p, ym_p, conv_p, c_p, n_p, m_p = _mixer_prompt(
        z_p.reshape(batch, seq, MAIN_COLS), zg_p.reshape(batch, seq, LANES), wc, row(g_conv_out[0]), bg,
        g_norm, nbt=4)
    yc_p = yc_p.reshape(n_prompt, CONV_DIM)
    ym_p = ym_p.reshape(n_prompt, HEADS * DV)
    yc_s, ym_s, conv_s, c_s, n_s, m_s = _mixer_sample(
        z_s, z_s.reshape(n_dec, 1, MAIN_COLS), zg_s.reshape(n_dec, 1, LANES),
        state_conv[0].reshape(n_dec, 2 * CONV_DIM), state_mlstm_C[0], state_mlstm_n[0],
        pad_lanes(state_mlstm_m[0]).reshape(n_dec, 1, LANES), wc, row(g_conv_out[0]), bg, g_norm,
        row0=0, n_dec=n_dec, sb=16)
    m_s = m_s[:, 0, :HEADS]
    ym_s = ym_s.reshape(n_dec, HEADS * DV).astype(BF16)

    wo = w_out[0].astype(BF16)
    wr = pad_lanes(w_router[0])
    wr_hi = wr.astype(BF16)
    wr_lo = (wr - wr_hi.astype(F32)).astype(BF16)
    router_args = (wo[:CONV_DIM], wo[CONV_DIM:], row(g_ffn[0]), wr_hi, wr_lo, pad_lanes(row(b_router[0])))
    h1_p, xn_p, e_p, gate_p, rank_p, cnt_p = _outproj_router(
        yc_p, ym_p, x_p, *router_args, jnp.zeros((1, LANES), F32), tm=512)
    h1_s, xn_s, e_s, gate_s, rank_s, cnt = _outproj_router(
        yc_s, ym_s, x_s, *router_args, cnt_p, tm=n_dec)

    n_assign = n_tok * TOP_K
    tmb = MOE_TMB
    nsb = n_assign // tmb + N_EXPERTS
    n_slots = -(-(n_assign + N_EXPERTS * (SUB - 1)) // SUB) * SUB + tmb
    counts = cnt[0, :N_EXPERTS].astype(jnp.int32)
    padded = (counts + SUB - 1) // SUB * SUB
    pend = jnp.cumsum(padded)
    pstart = jnp.concatenate([jnp.zeros((1,), jnp.int32), pend]).astype(jnp.int32)
    sb_per_e = (counts + tmb - 1) // tmb
    sb_cum = jnp.cumsum(sb_per_e)
    n_used = sb_cum[-1]
    sb = jnp.minimum(jnp.arange(nsb, dtype=jnp.int32), n_used - 1)
    sb_e = jnp.minimum(jnp.sum((sb_cum[None, :] <= sb[:, None]).astype(jnp.int32), axis=1), N_EXPERTS - 1)
    sb_j = sb - (sb_cum - sb_per_e)[sb_e]
    sb_start = (pstart[sb_e] + sb_j * tmb).astype(jnp.int32)
    sb_n = jnp.where(jnp.arange(nsb) < n_used, jnp.clip(counts[sb_e] - sb_j * tmb, 0, tmb), 0).astype(jnp.int32)
    expert_ids = jnp.arange(N_EXPERTS, dtype=jnp.int32)[:, None, None]

    def slot_of(e, r):
        first = jnp.sum(jnp.where(e[None, :TOP_K] == expert_ids, pstart[:N_EXPERTS, None, None], 0), axis=0)
        return (first + r[:TOP_K]).astype(jnp.int32)
    dest_p = slot_of(e_p, rank_p)
    dest_s = slot_of(e_s, rank_s)

    xs = _dispatch(jnp.concatenate([dest_p, dest_s], axis=1), counts, pstart, xn_p, xn_s, n_slots, tm=512)
    yb = _experts(sb_e.astype(jnp.int32), sb_start, sb_n, pstart[N_EXPERTS:], xs, w_gate_up[0],
                  b_gate_up[0].reshape(N_EXPERTS, 1, 2 * D_FF), w_down[0],
                  b_down[0].reshape(N_EXPERTS, 1, D_MODEL), nsb, tmb, MOE_TF)

    ple_args = (w_ple_gate[0].astype(BF16), w_ple_proj[0].astype(BF16), row(g_ple[0]), row(g_final))
    out_p = _combine(dest_p, h1_p, gate_p, p_prompt[0].reshape(n_prompt, PLE_DIM), yb, *ple_args, tm=256)
    out_s = _combine(dest_s, h1_s, gate_s, p_sample[0].reshape(n_dec, PLE_DIM), yb, *ple_args, tm=n_dec)

    y_prompt = out_p.reshape(batch, seq, D_MODEL)
    y_sample = out_s.reshape(n_dec, 1, D_MODEL)
    return (y_prompt, y_sample,
            conv_p[None], c_p[None], n_p[None], m_p[None, :, :, 0],
            conv_s.reshape(1, n_dec, 2, CONV_DIM), c_s[None], n_s[None], m_s[None])
```

```python
import functools

import jax
import jax.numpy as jnp
from jax import lax
from jax.experimental import pallas as pl
from jax.experimental.pallas import tpu as pltpu

F32 = jnp.float32
BF16 = jnp.bfloat16

D_MODEL = 2048
CONV_DIM = 1024
HEADS = 4
DK = 128
DV = 256
CHUNK = 128
N_EXPERTS = 32
TOP_K = 4
D_FF = 2048
PLE_DIM = 256
SWIGLU_LIMIT = 7.0
SWIGLU_ALPHA = 1.702
EPS = 1e-6
LANES = 128
NEG_BIG = -1e30
MAIN_COLS = 3 * CONV_DIM + HEADS * (2 * DK + 2 * DV)

COL_B, COL_C, COL_H = 0, CONV_DIM, 2 * CONV_DIM
COL_Q = 3 * CONV_DIM
COL_K = COL_Q + HEADS * DK
COL_V = COL_K + HEADS * DK
COL_O = COL_V + HEADS * DV

VMEM_LIMIT = 56 * 1024 * 1024

SUB = 128
MOE_TMB = 9 * SUB
MOE_TF = 512
MOE_SPLIT = 4
ISSUE_UNROLL = 8
SAMPLE_GROUP = 2


def _cparams(sem, vmem=VMEM_LIMIT):
    return pltpu.CompilerParams(dimension_semantics=sem, vmem_limit_bytes=vmem)


def _rms(x, g):
    return x * lax.rsqrt(jnp.mean(x * x, axis=-1, keepdims=True) + EPS) * g


def _split3(x):
    x1 = x.astype(BF16)
    r1 = x - x1.astype(F32)
    x2 = r1.astype(BF16)
    x3 = (r1 - x2.astype(F32)).astype(BF16)
    return x1, x2, x3


def _log_sigmoid(x):
    return jnp.minimum(x, 0.0) - jnp.log(1.0 + jnp.exp(-jnp.abs(x)))


def _norm_gates_kernel(x_ref, g_ref, wg_ref, a_ref, zg_ref):
    a = _rms(x_ref[...], g_ref[...]).astype(BF16)
    a_ref[...] = a
    zg_ref[...] = jnp.dot(a, wg_ref[...], preferred_element_type=F32)


def _norm_gates(x, g, w_gates, tm):
    n_tok = x.shape[0]
    return pl.pallas_call(
        _norm_gates_kernel,
        grid=(n_tok // tm,),
        in_specs=[
            pl.BlockSpec((tm, D_MODEL), lambda i: (i, 0)),
            pl.BlockSpec((1, D_MODEL), lambda i: (0, 0)),
            pl.BlockSpec((D_MODEL, LANES), lambda i: (0, 0)),
        ],
        out_specs=[
            pl.BlockSpec((tm, D_MODEL), lambda i: (i, 0)),
            pl.BlockSpec((tm, LANES), lambda i: (i, 0)),
        ],
        out_shape=[
            jax.ShapeDtypeStruct((n_tok, D_MODEL), BF16),
            jax.ShapeDtypeStruct((n_tok, LANES), F32),
        ],
        compiler_params=_cparams(("parallel",)),
        name="norm_gates",
    )(x, g, w_gates)


def _inproj_kernel(a_ref, w_ref, z_ref, w_scr):
    @pl.when(pl.program_id(1) == 0)
    def _():
        w_scr[...] = w_ref[0].astype(BF16)

    z_ref[...] = jnp.dot(a_ref[...], w_scr[...], preferred_element_type=F32)


def _inproj(a, w_in, tm, tn):
    n_tok = a.shape[0]
    return pl.pallas_call(
        _inproj_kernel,
        grid=(MAIN_COLS // tn, n_tok // tm),
        in_specs=[
            pl.BlockSpec((tm, D_MODEL), lambda j, i: (i, 0)),
            pl.BlockSpec((1, D_MODEL, tn), lambda j, i: (0, 0, j)),
        ],
        out_specs=pl.BlockSpec((tm, tn), lambda j, i: (i, j)),
        out_shape=jax.ShapeDtypeStruct((n_tok, MAIN_COLS), F32),
        scratch_shapes=[pltpu.VMEM((D_MODEL, tn), BF16)],
        compiler_params=_cparams(("arbitrary", "arbitrary")),
        name="inproj",
    )(a, w_in)


def _conv_chunk(zb, zc, zh, prev, wc, g):
    tt = zb.shape[0]
    u = zc * zh
    row = lax.broadcasted_iota(jnp.int32, (tt, 1), 0)
    u1 = jnp.where(row == 0, prev[7:8], pltpu.roll(u, 1, axis=0))
    u2 = jnp.where(row == 0, prev[6:7], jnp.where(row == 1, prev[7:8], pltpu.roll(u, 2, axis=0)))
    conv = wc[0:1] * u2 + wc[1:2] * u1 + wc[2:3] * u
    return _rms(zb * conv, g).astype(BF16), u


def _tri_dot(tri_bf16, x, tri_first):
    out = None
    for part in _split3(x):
        d = (jnp.dot(tri_bf16, part, preferred_element_type=F32) if tri_first
             else jnp.dot(part, tri_bf16, preferred_element_type=F32))
        out = d if out is None else out + d
    return out


def _mixer_prompt_kernel(zb_ref, zc_ref, zh_ref, q_ref, k_ref, v_ref, o_ref, zg_ref,
                         wc_ref, gc_ref, bg_ref, gn_ref,
                         yc_ref, y_ref, st_out, c_out, n_out, m_out, carry, c_scr, n_scr, m_scr):
    L = CHUNK

    @pl.when(pl.program_id(1) == 0)
    def _():
        carry[...] = jnp.zeros_like(carry)
        c_scr[...] = jnp.zeros_like(c_scr)
        n_scr[...] = jnp.zeros_like(n_scr)
        m_scr[...] = jnp.zeros_like(m_scr)

    lane = lax.broadcasted_iota(jnp.int32, (L, LANES), 1)
    r_i = lax.broadcasted_iota(jnp.int32, (L, L), 0)
    c_i = lax.broadcasted_iota(jnp.int32, (L, L), 1)
    causal = c_i <= r_i
    tril = causal.astype(BF16)
    triu = (r_i <= c_i).astype(BF16)

    nbt = q_ref.shape[0]
    for bi in range(nbt):
        yc, u = _conv_chunk(zb_ref[bi], zc_ref[bi], zh_ref[bi], carry[bi], wc_ref[...], gc_ref[...])
        yc_ref[bi] = yc
        carry[bi] = u[L - 8:L]
        st_out[bi] = u[L - 2:L]

    prs = [(bi, h) for bi in range(nbt) for h in range(HEADS)]
    each = lambda fn: {pr: fn(*pr) for pr in prs}
    nt_dims = (((1,), (1,)), ((), ()))
    tn_dims = (((0,), (0,)), ((), ()))

    gates, gates_t, bcol_all, brow_all = {}, {}, {}, {}
    for bi in range(nbt):
        zg = zg_ref[bi] + bg_ref[...]
        gates[bi] = jnp.where(lane < HEADS, zg, _log_sigmoid(zg))
        gates_t[bi] = gates[bi].T
        bcol_all[bi] = _tri_dot(tril, gates[bi], True)
        brow_all[bi] = _tri_dot(triu, gates_t[bi], False)

    b_col = each(lambda bi, h: bcol_all[bi][:, HEADS + h:HEADS + h + 1])
    b_row = each(lambda bi, h: brow_all[bi][HEADS + h:HEADS + h + 1, :])
    i_col = each(lambda bi, h: gates[bi][:, h:h + 1])
    i_row = each(lambda bi, h: gates_t[bi][h:h + 1, :])
    b_end = each(lambda bi, h: b_col[bi, h][L - 1:L, :])
    m0 = each(lambda bi, h: m_scr[bi, h:h + 1, 0:1])
    n0 = each(lambda bi, h: n_scr[bi, h:h + 1, :])
    c0 = each(lambda bi, h: c_scr[bi, h])
    q = each(lambda bi, h: q_ref[bi, :, h * DK:(h + 1) * DK])
    k = each(lambda bi, h: k_ref[bi, :, h * DK:(h + 1) * DK] * (DK ** -0.5))
    qb = each(lambda bi, h: q[bi, h].astype(BF16))
    vb = each(lambda bi, h: v_ref[bi, :, h * DV:(h + 1) * DV].astype(BF16))

    dmat = each(lambda bi, h: jnp.where(causal, b_col[bi, h] - b_row[bi, h] + i_row[bi, h], NEG_BIG))
    inter = each(lambda bi, h: b_col[bi, h] + m0[bi, h])
    m_t = each(lambda bi, h: jnp.maximum(inter[bi, h], jnp.max(dmat[bi, h], axis=1, keepdims=True)))
    w_int = each(lambda bi, h: jnp.exp(inter[bi, h] - m_t[bi, h]))
    p = each(lambda bi, h: jnp.where(causal, jnp.exp(dmat[bi, h] - m_t[bi, h]), 0.0))
    s = each(lambda bi, h: lax.dot_general(qb[bi, h], k[bi, h].astype(BF16), nt_dims,
                                           preferred_element_type=F32) * p[bi, h])
    num = each(lambda bi, h: jnp.dot(s[bi, h].astype(BF16), vb[bi, h], preferred_element_type=F32)
               + w_int[bi, h] * jnp.dot(qb[bi, h], c0[bi, h].astype(BF16), preferred_element_type=F32))
    den = each(lambda bi, h: jnp.sum(s[bi, h], axis=1, keepdims=True)
               + w_int[bi, h] * jnp.sum(q[bi, h] * n0[bi, h], axis=1, keepdims=True))
    hh = each(lambda bi, h: num[bi, h] / jnp.maximum(jnp.abs(den[bi, h]), jnp.exp(-m_t[bi, h])))
    hn = each(lambda bi, h: hh[bi, h] * lax.rsqrt(jnp.mean(hh[bi, h] * hh[bi, h], axis=-1, keepdims=True)
                                                  + EPS) * gn_ref[:, h * DV:(h + 1) * DV])
    for bi, h in prs:
        y = jax.nn.sigmoid(o_ref[bi, :, h * DV:(h + 1) * DV]) * hn[bi, h]
        y_ref[bi, :, h * DV:(h + 1) * DV] = y.astype(BF16)

    g_col = each(lambda bi, h: b_end[bi, h] - b_col[bi, h] + i_col[bi, h])
    g_row = each(lambda bi, h: b_end[bi, h] - b_row[bi, h] + i_row[bi, h])
    m_new = each(lambda bi, h: jnp.maximum(b_end[bi, h] + m0[bi, h],
                                           jnp.max(g_row[bi, h], axis=1, keepdims=True)))
    decay = each(lambda bi, h: jnp.exp(b_end[bi, h] + m0[bi, h] - m_new[bi, h]))
    kw = each(lambda bi, h: k[bi, h] * jnp.exp(g_col[bi, h] - m_new[bi, h]))
    c_new = each(lambda bi, h: decay[bi, h] * c0[bi, h]
                 + lax.dot_general(kw[bi, h].astype(BF16), vb[bi, h], tn_dims, preferred_element_type=F32))
    n_new = each(lambda bi, h: decay[bi, h] * n0[bi, h] + jnp.sum(kw[bi, h], axis=0, keepdims=True))
    for bi, h in prs:
        c_scr[bi, h] = c_new[bi, h]
        n_scr[bi, h:h + 1, :] = n_new[bi, h]
        m_scr[bi, h:h + 1, :] = jnp.broadcast_to(m_new[bi, h], (1, LANES))
        c_out[bi, h] = c_new[bi, h]
        n_out[bi, h:h + 1, :] = n_new[bi, h]
        m_out[bi, h:h + 1, :] = jnp.broadcast_to(m_new[bi, h], (1, LANES))


def _mixer_prompt(z3, zg3, w_conv, g_conv, b_gates, g_norm, nbt):
    batch, seq, _ = z3.shape
    nc = seq // CHUNK
    cb, qw, vw = CONV_DIM, HEADS * DK, HEADS * DV
    zcol = lambda w, col: pl.BlockSpec((nbt, CHUNK, w), lambda b, c: (b, c, col // w))
    const = lambda shape: pl.BlockSpec(shape, lambda b, c: (0,) * len(shape))
    per_seq = lambda *tail: pl.BlockSpec((nbt,) + tail, lambda b, c: (b,) + (0,) * len(tail))
    return pl.pallas_call(
        _mixer_prompt_kernel,
        grid=(batch // nbt, nc),
        in_specs=[
            zcol(cb, COL_B), zcol(cb, COL_C), zcol(cb, COL_H),
            zcol(qw, COL_Q), zcol(qw, COL_K), zcol(vw, COL_V), zcol(vw, COL_O),
            pl.BlockSpec((nbt, CHUNK, LANES), lambda b, c: (b, c, 0)),
            const((8, cb)), const((1, cb)), const((1, LANES)), const((1, vw)),
        ],
        out_specs=[
            pl.BlockSpec((nbt, CHUNK, cb), lambda b, c: (b, c, 0)),
            pl.BlockSpec((nbt, CHUNK, vw), lambda b, c: (b, c, 0)),
            per_seq(2, cb), per_seq(HEADS, DK, DV), per_seq(HEADS, DK), per_seq(HEADS, LANES),
        ],
        out_shape=[
            jax.ShapeDtypeStruct((batch, seq, cb), BF16),
            jax.ShapeDtypeStruct((batch, seq, vw), BF16),
            jax.ShapeDtypeStruct((batch, 2, cb), F32),
            jax.ShapeDtypeStruct((batch, HEADS, DK, DV), F32),
            jax.ShapeDtypeStruct((batch, HEADS, DK), F32),
            jax.ShapeDtypeStruct((batch, HEADS, LANES), F32),
        ],
        scratch_shapes=[
            pltpu.VMEM((nbt, 8, cb), F32),
            pltpu.VMEM((nbt, HEADS, DK, DV), F32),
            pltpu.VMEM((nbt, 8, DK), F32),
            pltpu.VMEM((nbt, 8, LANES), F32),
        ],
        compiler_params=_cparams(("parallel", "arbitrary")),
        name="mixer_prompt",
    )(z3, z3, z3, z3, z3, z3, z3, zg3, w_conv, g_conv, b_gates, g_norm)


def _mixer_sample_kernel(zb_ref, zc_ref, zh_ref, q_ref, k_ref, v_ref, o_ref, zg_ref,
                         st_ref, c_ref, n_ref, m_ref, wc_ref, gc_ref, bg_ref, gn_ref,
                         yc_ref, ym_ref, st_out, c_out, n_out, m_out):
    sb = zb_ref.shape[0]
    u = zc_ref[...] * zh_ref[...]
    s0 = st_ref[:, 0:CONV_DIM]
    s1 = st_ref[:, CONV_DIM:2 * CONV_DIM]
    wc = wc_ref[...]
    conv = wc[0:1] * s0 + wc[1:2] * s1 + wc[2:3] * u
    yc_ref[...] = _rms(zb_ref[...] * conv, gc_ref[...]).astype(BF16)
    st_out[:, 0:CONV_DIM] = s1
    st_out[:, CONV_DIM:2 * CONV_DIM] = u

    lane_row = lax.broadcasted_iota(jnp.int32, (1, LANES), 1)
    eye = (lax.broadcasted_iota(jnp.int32, (DK, DK), 0)
           == lax.broadcasted_iota(jnp.int32, (DK, DK), 1)).astype(F32)

    def to_col(row):
        return jnp.sum(eye * row, axis=1, keepdims=True)

    def pick(row, j):
        return jnp.sum(jnp.where(lane_row == j, row, 0.0), axis=1, keepdims=True)

    def per_group(gi, carry):
        ss = [gi * SAMPLE_GROUP + j for j in range(SAMPLE_GROUP)]
        prs = [(j, h) for j in range(SAMPLE_GROUP) for h in range(HEADS)]
        each = lambda fn: {pr: fn(*pr) for pr in prs}
        zg = [zg_ref[s] + bg_ref[...] for s in ss]
        lf_all = [_log_sigmoid(z) for z in zg]
        m_row = [m_ref[s] for s in ss]

        q = each(lambda j, h: q_ref[ss[j], :, h * DK:(h + 1) * DK])
        k = each(lambda j, h: k_ref[ss[j], :, h * DK:(h + 1) * DK] * (DK ** -0.5))
        v = each(lambda j, h: v_ref[ss[j], :, h * DV:(h + 1) * DV])
        c0 = each(lambda j, h: c_ref[ss[j], h])
        n0 = each(lambda j, h: n_ref[ss[j], h:h + 1, :])
        i_pre = each(lambda j, h: pick(zg[j], h))
        lf = each(lambda j, h: pick(lf_all[j], HEADS + h))
        m0 = each(lambda j, h: pick(m_row[j], h))
        q_col = each(lambda j, h: to_col(q[j, h]))
        k_col = each(lambda j, h: to_col(k[j, h]))
        inter = each(lambda j, h: lf[j, h] + m0[j, h])
        m_t = each(lambda j, h: jnp.maximum(inter[j, h], i_pre[j, h]))
        w_int = each(lambda j, h: jnp.exp(inter[j, h] - m_t[j, h]))
        p = each(lambda j, h: jnp.exp(i_pre[j, h] - m_t[j, h]))
        sc = each(lambda j, h: jnp.sum(q[j, h] * k[j, h], axis=1, keepdims=True) * p[j, h])
        qc = each(lambda j, h: jnp.sum(q_col[j, h] * c0[j, h], axis=0, keepdims=True))
        num = each(lambda j, h: sc[j, h] * v[j, h] + w_int[j, h] * qc[j, h])
        den = each(lambda j, h: sc[j, h] + w_int[j, h] * jnp.sum(q[j, h] * n0[j, h], axis=1, keepdims=True))
        hh = each(lambda j, h: num[j, h] / jnp.maximum(jnp.abs(den[j, h]), jnp.exp(-m_t[j, h])))
        hn = each(lambda j, h: hh[j, h] * lax.rsqrt(jnp.mean(hh[j, h] * hh[j, h], axis=-1, keepdims=True) + EPS)
                  * gn_ref[:, h * DV:(h + 1) * DV])
        for j, h in prs:
            o = o_ref[ss[j], :, h * DV:(h + 1) * DV]
            ym_ref[ss[j], :, h * DV:(h + 1) * DV] = jax.nn.sigmoid(o) * hn[j, h]
            c_out[ss[j], h] = w_int[j, h] * c0[j, h] + (p[j, h] * k_col[j, h]) * v[j, h]
            n_out[ss[j], h:h + 1, :] = w_int[j, h] * n0[j, h] + p[j, h] * k[j, h]
        for j in range(SAMPLE_GROUP):
            m_new_row = jnp.zeros((1, LANES), F32)
            for h in range(HEADS):
                m_new_row = jnp.where(lane_row == h, m_t[j, h], m_new_row)
            m_out[ss[j]] = m_new_row
        return carry

    lax.fori_loop(0, sb // SAMPLE_GROUP, per_group, 0)


def _mixer_sample(z, zs3, zg3, st, c_s, n_s, m3, w_conv, g_conv, b_gates, g_norm, row0, n_dec, sb):
    cb, qw, vw = CONV_DIM, HEADS * DK, HEADS * DV
    r0 = row0 // sb
    row = lambda i: r0 + i
    return pl.pallas_call(
        _mixer_sample_kernel,
        grid=(n_dec // sb,),
        in_specs=[
            pl.BlockSpec((sb, cb), lambda i: (row(i), COL_B // cb)),
            pl.BlockSpec((sb, cb), lambda i: (row(i), COL_C // cb)),
            pl.BlockSpec((sb, cb), lambda i: (row(i), COL_H // cb)),
            pl.BlockSpec((sb, 1, qw), lambda i: (i, 0, COL_Q // qw)),
            pl.BlockSpec((sb, 1, qw), lambda i: (i, 0, COL_K // qw)),
            pl.BlockSpec((sb, 1, vw), lambda i: (i, 0, COL_V // vw)),
            pl.BlockSpec((sb, 1, vw), lambda i: (i, 0, COL_O // vw)),
            pl.BlockSpec((sb, 1, LANES), lambda i: (i, 0, 0)),
            pl.BlockSpec((sb, 2 * cb), lambda i: (i, 0)),
            pl.BlockSpec((sb, HEADS, DK, DV), lambda i: (i, 0, 0, 0)),
            pl.BlockSpec((sb, HEADS, DK), lambda i: (i, 0, 0)),
            pl.BlockSpec((sb, 1, LANES), lambda i: (i, 0, 0)),
            pl.BlockSpec((8, cb), lambda i: (0, 0)),
            pl.BlockSpec((1, cb), lambda i: (0, 0)),
            pl.BlockSpec((1, LANES), lambda i: (0, 0)),
            pl.BlockSpec((1, vw), lambda i: (0, 0)),
        ],
        out_specs=[
            pl.BlockSpec((sb, cb), lambda i: (i, 0)),
            pl.BlockSpec((sb, 1, vw), lambda i: (i, 0, 0)),
            pl.BlockSpec((sb, 2 * cb), lambda i: (i, 0)),
            pl.BlockSpec((sb, HEADS, DK, DV), lambda i: (i, 0, 0, 0)),
            pl.BlockSpec((sb, HEADS, DK), lambda i: (i, 0, 0)),
            pl.BlockSpec((sb, 1, LANES), lambda i: (i, 0, 0)),
        ],
        out_shape=[
            jax.ShapeDtypeStruct((n_dec, cb), BF16),
            jax.ShapeDtypeStruct((n_dec, 1, vw), F32),
            jax.ShapeDtypeStruct((n_dec, 2 * cb), F32),
            jax.ShapeDtypeStruct((n_dec, HEADS, DK, DV), F32),
            jax.ShapeDtypeStruct((n_dec, HEADS, DK), F32),
            jax.ShapeDtypeStruct((n_dec, 1, LANES), F32),
        ],
        compiler_params=_cparams(("arbitrary",)),
        name="mixer_sample",
    )(z, z, z, zs3, zs3, zs3, zs3, zg3, st, c_s, n_s, m3, w_conv, g_conv, b_gates, g_norm)


def _outproj_router_kernel(yc_ref, ym_ref, x_ref, wo1_ref, wo2_ref, g_ref, wr_hi_ref, wr_lo_ref, br_ref,
                           cin_ref, h1_ref, xn_ref, e_ref, gate_ref, rank_ref, cnt_ref, carry):
    tm = x_ref.shape[0]

    @pl.when(pl.program_id(0) == 0)
    def _():
        carry[...] = cin_ref[...]

    mix = (jnp.dot(yc_ref[...], wo1_ref[...], preferred_element_type=F32)
           + jnp.dot(ym_ref[...], wo2_ref[...], preferred_element_type=F32))
    h1 = x_ref[...] + mix
    h1_ref[...] = h1
    xn = _rms(h1, g_ref[...])
    xn_ref[...] = xn

    xh = xn.astype(BF16)
    xl = (xn - xh.astype(F32)).astype(BF16)
    logits = (jnp.dot(xh, wr_hi_ref[...], preferred_element_type=F32)
              + jnp.dot(xl, wr_hi_ref[...], preferred_element_type=F32)
              + jnp.dot(xh, wr_lo_ref[...], preferred_element_type=F32)) + br_ref[...]
    lane = lax.broadcasted_iota(jnp.int32, (tm, LANES), 1)
    lane_f = lane.astype(F32)
    work = jnp.where(lane < N_EXPERTS, logits, NEG_BIG)

    tops, idxs = [], []
    chosen = jnp.zeros((tm, LANES), F32)
    for _ in range(TOP_K):
        mx = jnp.max(work, axis=1, keepdims=True)
        idx = jnp.min(jnp.where(work == mx, lane_f, float(LANES)), axis=1, keepdims=True)
        sel = lane_f == idx
        tops.append(mx)
        idxs.append(idx)
        chosen = jnp.where(sel, 1.0, chosen)
        work = jnp.where(sel, NEG_BIG, work)

    exps = [jnp.exp(t - tops[0]) for t in tops]
    denom = exps[0] + exps[1] + exps[2] + exps[3]

    r_i = lax.broadcasted_iota(jnp.int32, (tm, tm), 0)
    c_i = lax.broadcasted_iota(jnp.int32, (tm, tm), 1)
    before = jnp.dot((c_i < r_i).astype(BF16), chosen.astype(BF16),
                     preferred_element_type=F32) + carry[...]

    e_out = jnp.zeros((tm, LANES), F32)
    g_out = jnp.zeros((tm, LANES), F32)
    r_out = jnp.zeros((tm, LANES), F32)
    for kk in range(TOP_K):
        rank = jnp.sum(jnp.where(lane_f == idxs[kk], before, 0.0), axis=1, keepdims=True)
        e_out = jnp.where(lane == kk, idxs[kk], e_out)
        g_out = jnp.where(lane == kk, exps[kk] / denom, g_out)
        r_out = jnp.where(lane == kk, rank, r_out)
    gate_ref[...] = g_out
    for c in range(tm // LANES):
        rows = slice(c * LANES, (c + 1) * LANES)
        e_ref[:, rows] = e_out[rows].T[0:8].astype(jnp.int32)
        rank_ref[:, rows] = r_out[rows].T[0:8].astype(jnp.int32)

    new_carry = carry[...] + jnp.sum(chosen, axis=0, keepdims=True)
    carry[...] = new_carry
    cnt_ref[...] = new_carry


def _outproj_router(y_conv, y_mlstm, x, wo1, wo2, g_ffn, wr_hi, wr_lo, b_router, cnt_in, tm):
    n_tok = x.shape[0]
    full = lambda shape: pl.BlockSpec(shape, lambda i: (0,) * len(shape))
    tile = lambda w: pl.BlockSpec((tm, w), lambda i: (i, 0))
    choice_major = pl.BlockSpec((8, tm), lambda i: (0, i))
    return pl.pallas_call(
        _outproj_router_kernel,
        grid=(n_tok // tm,),
        in_specs=[
            tile(CONV_DIM), tile(HEADS * DV), tile(D_MODEL),
            full((CONV_DIM, D_MODEL)), full((HEADS * DV, D_MODEL)), full((1, D_MODEL)),
            full((D_MODEL, LANES)), full((D_MODEL, LANES)), full((1, LANES)), full((1, LANES)),
        ],
        out_specs=[tile(D_MODEL), tile(D_MODEL), choice_major, tile(LANES), choice_major, full((1, LANES))],
        out_shape=[
            jax.ShapeDtypeStruct((n_tok, D_MODEL), F32),
            jax.ShapeDtypeStruct((n_tok, D_MODEL), F32),
            jax.ShapeDtypeStruct((8, n_tok), jnp.int32),
            jax.ShapeDtypeStruct((n_tok, LANES), F32),
            jax.ShapeDtypeStruct((8, n_tok), jnp.int32),
            jax.ShapeDtypeStruct((1, LANES), F32),
        ],
        scratch_shapes=[pltpu.VMEM((1, LANES), F32)],
        compiler_params=_cparams(("arbitrary",)),
        name="outproj_router",
    )(y_conv, y_mlstm, x, wo1, wo2, g_ffn, wr_hi, wr_lo, b_router, cnt_in)


def _zero_fill_rows(dst, zero_scr, sem, start, count, act):
    low = count & 7
    for t in range(7):
        @pl.when(t < low)
        def _(t=t):
            act(pltpu.make_async_copy(zero_scr.at[pl.ds(0, 1)], dst.at[pl.ds(start + t, 1)], sem))

    off = start + low
    bit = 8
    while bit < SUB:
        take = count & bit

        @pl.when(take != 0)
        def _(off=off, bit=bit):
            row = pl.multiple_of(off, 8)
            act(pltpu.make_async_copy(zero_scr.at[pl.ds(0, bit)], dst.at[pl.ds(row, bit)], sem))

        off = off + take
        bit *= 2


def _zero_fill_chunks(dst, zero_scr, sem, start, act):
    def body(j, c):
        row = pl.multiple_of(start + j * SUB, SUB)
        act(pltpu.make_async_copy(zero_scr, dst.at[pl.ds(row, SUB)], sem))
        return c

    lax.fori_loop(0, (dst.shape[0] - start) // SUB, body, 0)


def _dispatch_kernel(d0_ref, d1_ref, d2_ref, d3_ref, cnt_ref, pstart_ref, xn_ref, xn_s_ref, xs_out,
                     zero_scr, sem, zsem):
    dest_refs = (d0_ref, d1_ref, d2_ref, d3_ref)
    tm = xn_ref.shape[0]
    step = pl.program_id(0)
    last = pl.num_programs(0) - 1

    def fill(act):
        for e in range(N_EXPERTS):
            c = cnt_ref[e]
            _zero_fill_rows(xs_out, zero_scr, zsem, pstart_ref[e] + c, (SUB - c % SUB) % SUB, act)
        _zero_fill_chunks(xs_out, zero_scr, zsem, pstart_ref[N_EXPERTS], act)

    @pl.when(step == 0)
    def _():
        zero_scr[...] = jnp.zeros_like(zero_scr)
        fill(lambda cp: cp.start())

    def scatter_rows(src, base):
        rows = src.shape[0]

        def issue(r, c):
            for kk in range(TOP_K):
                d = dest_refs[kk][base + r]
                pltpu.make_async_copy(src.at[pl.ds(r, 1)], xs_out.at[pl.ds(d, 1)], sem).start()
            return c

        lax.fori_loop(0, rows, issue, 0, unroll=ISSUE_UNROLL)
        for _ in range(TOP_K):
            pltpu.make_async_copy(src, xs_out.at[pl.ds(0, rows)], sem).wait()

    scatter_rows(xn_ref, step * tm)

    @pl.when(step == last)
    def _():
        scatter_rows(xn_s_ref, pl.num_programs(0) * tm)

    @pl.when(step == 0)
    def _():
        fill(lambda cp: cp.wait())


def _dispatch(dest, counts, pstart, xn_p, xn_s, n_slots, tm):
    n_prompt = xn_p.shape[0]
    return pl.pallas_call(
        _dispatch_kernel,
        grid_spec=pltpu.PrefetchScalarGridSpec(
            num_scalar_prefetch=TOP_K + 2,
            grid=(n_prompt // tm,),
            in_specs=[pl.BlockSpec((tm, D_MODEL), lambda i, *_: (i, 0)),
                      pl.BlockSpec(xn_s.shape, lambda i, *_: (0, 0))],
            out_specs=pl.BlockSpec(memory_space=pl.ANY),
            scratch_shapes=[pltpu.VMEM((SUB, D_MODEL), F32), pltpu.SemaphoreType.DMA(()),
                            pltpu.SemaphoreType.DMA(())],
        ),
        out_shape=jax.ShapeDtypeStruct((n_slots, D_MODEL), F32),
        compiler_params=_cparams(("arbitrary",)),
        name="dispatch",
    )(*[dest[k] for k in range(TOP_K)], counts, pstart, xn_p, xn_s)


def _expert_kernel(se_ref, st_ref, sn_ref, end_ref, xs_ref, wg_ref, wu_ref, bg_ref, bu_ref, wd_ref, bd_ref,
                   y_ref, x_scr, xb_scr, acc, zero_scr, sem, zsem, xsem):
    i, f = pl.program_id(0), pl.program_id(1)
    nf = pl.num_programs(1)
    tmb = acc.shape[0]
    n = sn_ref[i]
    start = st_ref[i]

    def x_copy(j):
        row = pl.multiple_of(st_ref[j], SUB)
        return pltpu.make_async_copy(xs_ref.at[pl.ds(row, tmb)], x_scr, xsem)

    @pl.when((i == 0) & (f == 0))
    def _():
        zero_scr[...] = jnp.zeros_like(zero_scr)
        _zero_fill_chunks(y_ref, zero_scr, zsem, end_ref[0], lambda cp: cp.start())
        _zero_fill_chunks(y_ref, zero_scr, zsem, end_ref[0], lambda cp: cp.wait())

    kc = D_MODEL // MOE_SPLIT

    def y_copy(o, mc, c):
        row = pl.multiple_of(start + o, SUB)
        cols = pl.ds(c * kc, kc)
        return pltpu.make_async_copy(acc.at[pl.ds(o, mc), cols], y_ref.at[pl.ds(row, mc), cols], sem)

    def for_each_sub_block(fn):
        kmax = tmb // SUB
        quarter = SUB // 4
        nq = (n + quarter - 1) // quarter
        for kq in range(4 * kmax, 4 * (kmax - 1), -1):
            pl.when(nq == kq)(functools.partial(fn, 0, kq * quarter, kmax * SUB))
        pl.when((nq <= 4 * (kmax - 1)) & (nq > 4 * (kmax - 2)))(
            functools.partial(fn, 0, (kmax - 1) * SUB, (kmax - 1) * SUB))
        nb = (n + SUB - 1) // SUB
        bits = [b for b in (64, 32, 16, 8, 4, 2, 1) if b <= kmax - 2]
        small = nq <= 4 * (kmax - 2)
        for b in bits:
            higher = [h for h in bits if h > b]
            for mask in range(1 << len(higher)):
                above = sum(h for j, h in enumerate(higher) if (mask >> j) & 1)
                if above + b > kmax - 2:
                    continue
                cond = small & ((nb & b) != 0) & ((nb & sum(higher)) == above)
                pl.when(cond)(functools.partial(fn, above * SUB, b * SUB, b * SUB))

    @pl.when(n > 0)
    def _():
        @pl.when(f == 0)
        def _():
            @pl.when(i == 0)
            def _():
                x_copy(0).start()

            x_copy(i).wait()
            xb_scr[...] = x_scr[...].astype(BF16)
            nxt = jnp.minimum(i + 1, pl.num_programs(0) - 1)

            @pl.when((i + 1 < pl.num_programs(0)) & (sn_ref[nxt] > 0))
            def _():
                x_copy(nxt).start()

            acc[...] = jnp.broadcast_to(bd_ref[0], acc.shape)

        def sub_block(o, m, mc):
            g = bg_ref[0]
            u = bu_ref[0]
            for c in range(MOE_SPLIT):
                xb = xb_scr[o:o + m, c * kc:(c + 1) * kc]
                g = g + jnp.dot(xb, wg_ref[0, c * kc:(c + 1) * kc, :].astype(BF16), preferred_element_type=F32)
                u = u + jnp.dot(xb, wu_ref[0, c * kc:(c + 1) * kc, :].astype(BF16), preferred_element_type=F32)
            g = jnp.minimum(g, SWIGLU_LIMIT)
            u = jnp.clip(u, -SWIGLU_LIMIT, SWIGLU_LIMIT)
            act = (g * jax.nn.sigmoid(SWIGLU_ALPHA * g) * (u + 1.0)).astype(BF16)
            for c in range(MOE_SPLIT):
                cols = slice(c * kc, (c + 1) * kc)
                acc[o:o + m, cols] += jnp.dot(act, wd_ref[0, :, cols].astype(BF16), preferred_element_type=F32)

                @pl.when(f == nf - 1)
                def _(c=c):
                    y_copy(o, mc, c).start()

        for_each_sub_block(sub_block)

        @pl.when(f == nf - 1)
        def _():
            def drain(o, m, mc):
                for c in range(MOE_SPLIT):
                    y_copy(o, mc, c).wait()

            for_each_sub_block(drain)


def _experts(sb_e, sb_start, sb_n, end, xs, w_gate_up, b_gate_up, w_down, b_down, nsb, tmb, tf):
    n_slots = xs.shape[0]
    nf = D_FF // tf

    def fidx(i, f, sn):
        return jnp.where(sn[i] > 0, f, nf - 1)

    return pl.pallas_call(
        _expert_kernel,
        grid_spec=pltpu.PrefetchScalarGridSpec(
            num_scalar_prefetch=4,
            grid=(nsb, nf),
            in_specs=[
                pl.BlockSpec(memory_space=pl.ANY),
                pl.BlockSpec((1, D_MODEL, tf), lambda i, f, se, st, sn, en: (se[i], 0, fidx(i, f, sn))),
                pl.BlockSpec((1, D_MODEL, tf), lambda i, f, se, st, sn, en: (se[i], 0, nf + fidx(i, f, sn))),
                pl.BlockSpec((1, 1, tf), lambda i, f, se, st, sn, en: (se[i], 0, fidx(i, f, sn))),
                pl.BlockSpec((1, 1, tf), lambda i, f, se, st, sn, en: (se[i], 0, nf + fidx(i, f, sn))),
                pl.BlockSpec((1, tf, D_MODEL), lambda i, f, se, st, sn, en: (se[i], fidx(i, f, sn), 0)),
                pl.BlockSpec((1, 1, D_MODEL), lambda i, f, se, st, sn, en: (se[i], 0, 0)),
            ],
            out_specs=pl.BlockSpec(memory_space=pl.ANY),
            scratch_shapes=[pltpu.VMEM((tmb, D_MODEL), F32), pltpu.VMEM((tmb, D_MODEL), BF16),
                            pltpu.VMEM((tmb, D_MODEL), F32), pltpu.VMEM((SUB, D_MODEL), F32),
                            pltpu.SemaphoreType.DMA(()), pltpu.SemaphoreType.DMA(()),
                            pltpu.SemaphoreType.DMA(())],
        ),
        out_shape=jax.ShapeDtypeStruct((n_slots, D_MODEL), F32),
        compiler_params=_cparams(("arbitrary", "arbitrary")),
        name="experts",
    )(sb_e, sb_start, sb_n, end, xs, w_gate_up, w_gate_up, b_gate_up, b_gate_up, w_down, b_down)


def _combine_kernel(d0_ref, d1_ref, d2_ref, d3_ref, h1_ref, gate_ref, p_ref, yb_ref, wpg_ref, wpp_ref,
                    gp_ref, gf_ref, out_ref, gbuf_a, gbuf_b, sem):
    dest_refs = (d0_ref, d1_ref, d2_ref, d3_ref)
    tm = h1_ref.shape[0]
    step = pl.program_id(0)
    last = pl.num_programs(0) - 1
    bufs = (gbuf_a, gbuf_b)

    def row_gather(t, r, kk, s):
        d = dest_refs[kk][t * tm + r]
        return pltpu.make_async_copy(yb_ref.at[pl.ds(d, 1)], bufs[s].at[kk, pl.ds(r, 1)], sem.at[s])

    def wait_tile(s):
        for kk in range(TOP_K):
            pltpu.make_async_copy(yb_ref.at[pl.ds(0, tm)], bufs[s].at[kk], sem.at[s]).wait()

    @pl.when(step == 0)
    def _():
        def issue(r, c):
            for kk in range(TOP_K):
                row_gather(0, r, kk, 0).start()
            return c

        lax.fori_loop(0, tm, issue, 0, unroll=ISSUE_UNROLL)

    def tile_step(cur):
        nxt = 1 - cur
        wait_tile(cur)
        t_next = jnp.minimum(step + 1, last)
        for r in range(tm):
            for kk in range(TOP_K):
                row_gather(t_next, r, kk, nxt).start()

        gate = gate_ref[...]
        moe = gate[:, 0:1] * bufs[cur][0]
        for kk in range(1, TOP_K):
            moe = moe + gate[:, kk:kk + 1] * bufs[cur][kk]
        h2 = h1_ref[...] + moe
        a = _rms(h2, gp_ref[...]).astype(BF16)
        pg = jax.nn.sigmoid(jnp.dot(a, wpg_ref[...], preferred_element_type=F32))
        pp = jnp.dot(p_ref[...].astype(BF16), wpp_ref[...], preferred_element_type=F32)
        h3 = h2 + pg * pp
        out_ref[...] = _rms(h3, gf_ref[...])

        @pl.when(step == last)
        def _():
            wait_tile(nxt)

    pl.when(step % 2 == 0)(functools.partial(tile_step, 0))
    pl.when(step % 2 == 1)(functools.partial(tile_step, 1))


def _combine(dest, h1, gate, p, yb, wpg, wpp, g_ple, g_final, tm):
    n_tok = h1.shape[0]
    full = lambda shape: pl.BlockSpec(shape, lambda i, *_: (0,) * len(shape))
    tile = lambda w: pl.BlockSpec((tm, w), lambda i, *_: (i, 0))
    return pl.pallas_call(
        _combine_kernel,
        grid_spec=pltpu.PrefetchScalarGridSpec(
            num_scalar_prefetch=TOP_K,
            grid=(n_tok // tm,),
            in_specs=[
                tile(D_MODEL), tile(LANES), tile(PLE_DIM),
                pl.BlockSpec(memory_space=pl.ANY),
                full((D_MODEL, D_MODEL)), full((PLE_DIM, D_MODEL)), full((1, D_MODEL)), full((1, D_MODEL)),
            ],
            out_specs=tile(D_MODEL),
            scratch_shapes=[pltpu.VMEM((TOP_K, tm, D_MODEL), F32), pltpu.VMEM((TOP_K, tm, D_MODEL), F32),
                            pltpu.SemaphoreType.DMA((2,))],
        ),
        out_shape=jax.ShapeDtypeStruct((n_tok, D_MODEL), F32),
        compiler_params=_cparams(("arbitrary",)),
        name="combine",
    )(*[dest[k] for k in range(TOP_K)], h1, gate, p, yb, wpg, wpp, g_ple, g_final)


def kernel(x_prompt, x_sample, state_conv, state_mlstm_C, state_mlstm_n, state_mlstm_m, p_prompt, p_sample,
           g_mix, w_in, b_gates, w_conv, g_conv_out, g_mlstm_out, w_out, g_ffn, w_router, b_router,
           w_gate_up, b_gate_up, w_down, b_down, g_ple, w_ple_gate, w_ple_proj, g_final):
    batch, seq, _ = x_prompt.shape
    n_dec = x_sample.shape[0]
    assert w_in.shape[0] == 1 and x_sample.shape[1] == 1
    n_prompt = batch * seq
    n_tok = n_prompt + n_dec

    x_p = x_prompt.reshape(n_prompt, D_MODEL)
    x_s = x_sample.reshape(n_dec, D_MODEL)

    row = lambda a: a.reshape(1, -1)
    pad_lanes = lambda a: jnp.pad(a, ((0, 0), (0, LANES - a.shape[1])))
    w_gates = pad_lanes(w_in[0, :, MAIN_COLS:]).astype(BF16)
    bg = pad_lanes(row(b_gates[0]))
    wc = jnp.pad(w_conv[0], ((0, 8 - w_conv.shape[1]), (0, 0)))
    g_norm = row(g_mlstm_out[0])

    a_p, zg_p = _norm_gates(x_p, row(g_mix[0]), w_gates, tm=1024)
    a_s, zg_s = _norm_gates(x_s, row(g_mix[0]), w_gates, tm=n_dec)
    z_p = _inproj(a_p, w_in, tm=1024, tn=1024)
    z_s = _inproj(a_s, w_in, tm=n_dec, tn=1024)

    yc_p, ym_p, conv_p, c_p, n_p, m_p = _mixer_prompt(
        z_p.reshape(batch, seq, MAIN_COLS), zg_p.reshape(batch, seq, LANES), wc, row(g_conv_out[0]), bg,
        g_norm, nbt=4)
    yc_p = yc_p.reshape(n_prompt, CONV_DIM)
    ym_p = ym_p.reshape(n_prompt, HEADS * DV)
    yc_s, ym_s, conv_s, c_s, n_s, m_s = _mixer_sample(
        z_s, z_s.reshape(n_dec, 1, MAIN_COLS), zg_s.reshape(n_dec, 1, LANES),
        state_conv[0].reshape(n_dec, 2 * CONV_DIM), state_mlstm_C[0], state_mlstm_n[0],
        pad_lanes(state_mlstm_m[0]).reshape(n_dec, 1, LANES), wc, row(g_conv_out[0]), bg, g_norm,
        row0=0, n_dec=n_dec, sb=16)
    m_s = m_s[:, 0, :HEADS]
    ym_s = ym_s.reshape(n_dec, HEADS * DV).astype(BF16)

    wo = w_out[0].astype(BF16)
    wr = pad_lanes(w_router[0])
    wr_hi = wr.astype(BF16)
    wr_lo = (wr - wr_hi.astype(F32)).astype(BF16)
    router_args = (wo[:CONV_DIM], wo[CONV_DIM:], row(g_ffn[0]), wr_hi, wr_lo, pad_lanes(row(b_router[0])))
    h1_p, xn_p, e_p, gate_p, rank_p, cnt_p = _outproj_router(
        yc_p, ym_p, x_p, *router_args, jnp.zeros((1, LANES), F32), tm=512)
    h1_s, xn_s, e_s, gate_s, rank_s, cnt = _outproj_router(
        yc_s, ym_s, x_s, *router_args, cnt_p, tm=n_dec)

    n_assign = n_tok * TOP_K
    tmb = MOE_TMB
    nsb = n_assign // tmb + N_EXPERTS
    n_slots = -(-(n_assign + N_EXPERTS * (SUB - 1)) // SUB) * SUB + tmb
    counts = cnt[0, :N_EXPERTS].astype(jnp.int32)
    padded = (counts + SUB - 1) // SUB * SUB
    pend = jnp.cumsum(padded)
    pstart = jnp.concatenate([jnp.zeros((1,), jnp.int32), pend]).astype(jnp.int32)
    sb_per_e = (counts + tmb - 1) // tmb
    sb_cum = jnp.cumsum(sb_per_e)
    n_used = sb_cum[-1]
    sb = jnp.minimum(jnp.arange(nsb, dtype=jnp.int32), n_used - 1)
    sb_e = jnp.minimum(jnp.sum((sb_cum[None, :] <= sb[:, None]).astype(jnp.int32), axis=1), N_EXPERTS - 1)
    sb_j = sb - (sb_cum - sb_per_e)[sb_e]
    sb_start = (pstart[sb_e] + sb_j * tmb).astype(jnp.int32)
    sb_n = jnp.where(jnp.arange(nsb) < n_used, jnp.clip(counts[sb_e] - sb_j * tmb, 0, tmb), 0).astype(jnp.int32)
    expert_ids = jnp.arange(N_EXPERTS, dtype=jnp.int32)[:, None, None]

    def slot_of(e, r):
        first = jnp.sum(jnp.where(e[None, :TOP_K] == expert_ids, pstart[:N_EXPERTS, None, None], 0), axis=0)
        return (first + r[:TOP_K]).astype(jnp.int32)
    dest_p = slot_of(e_p, rank_p)
    dest_s = slot_of(e_s, rank_s)

    xs = _dispatch(jnp.concatenate([dest_p, dest_s], axis=1), counts, pstart, xn_p, xn_s, n_slots, tm=512)
    yb = _experts(sb_e.astype(jnp.int32), sb_start, sb_n, pstart[N_EXPERTS:], xs, w_gate_up[0],
                  b_gate_up[0].reshape(N_EXPERTS, 1, 2 * D_FF), w_down[0],
                  b_down[0].reshape(N_EXPERTS, 1, D_MODEL), nsb, tmb, MOE_TF)

    ple_args = (w_ple_gate[0].astype(BF16), w_ple_proj[0].astype(BF16), row(g_ple[0]), row(g_final))
    out_p = _combine(dest_p, h1_p, gate_p, p_prompt[0].reshape(n_prompt, PLE_DIM), yb, *ple_args, tm=256)
    out_s = _combine(dest_s, h1_s, gate_s, p_sample[0].reshape(n_dec, PLE_DIM), yb, *ple_args, tm=n_dec)

    y_prompt = out_p.reshape(batch, seq, D_MODEL)
    y_sample = out_s.reshape(n_dec, 1, D_MODEL)
    return (y_prompt, y_sample,
            conv_p[None], c_p[None], n_p[None], m_p[None, :, :, 0],
            conv_s.reshape(1, n_dec, 2, CONV_DIM), c_s[None], n_s[None], m_s[None])
```

```python
import functools

import jax
import jax.numpy as jnp
from jax import lax
from jax.experimental import pallas as pl
from jax.experimental.pallas import tpu as pltpu

F32 = jnp.float32
BF16 = jnp.bfloat16

D_MODEL = 2048
CONV_DIM = 1024
HEADS = 4
DK = 128
DV = 256
CHUNK = 128
N_EXPERTS = 32
TOP_K = 4
D_FF = 2048
PLE_DIM = 256
SWIGLU_LIMIT = 7.0
SWIGLU_ALPHA = 1.702
EPS = 1e-6
LANES = 128
NEG_BIG = -1e30
MAIN_COLS = 3 * CONV_DIM + HEADS * (2 * DK + 2 * DV)

COL_B, COL_C, COL_H = 0, CONV_DIM, 2 * CONV_DIM
COL_Q = 3 * CONV_DIM
COL_K = COL_Q + HEADS * DK
COL_V = COL_K + HEADS * DK
COL_O = COL_V + HEADS * DV

VMEM_LIMIT = 56 * 1024 * 1024

SUB = 128
MOE_TMB = 9 * SUB
MOE_TF = 512
MOE_SPLIT = 4
ISSUE_UNROLL = 8
SAMPLE_GROUP = 2


def _cparams(sem, vmem=VMEM_LIMIT):
    return pltpu.CompilerParams(dimension_semantics=sem, vmem_limit_bytes=vmem)


def _rms(x, g):
    return x * lax.rsqrt(jnp.mean(x * x, axis=-1, keepdims=True) + EPS) * g


def _split3(x):
    x1 = x.astype(BF16)
    r1 = x - x1.astype(F32)
    x2 = r1.astype(BF16)
    x3 = (r1 - x2.astype(F32)).astype(BF16)
    return x1, x2, x3


def _log_sigmoid(x):
    return jnp.minimum(x, 0.0) - jnp.log(1.0 + jnp.exp(-jnp.abs(x)))


def _norm_gates_kernel(x_ref, g_ref, wg_ref, a_ref, zg_ref):
    a = _rms(x_ref[...], g_ref[...]).astype(BF16)
    a_ref[...] = a
    zg_ref[...] = jnp.dot(a, wg_ref[...], preferred_element_type=F32)


def _norm_gates(x, g, w_gates, tm):
    n_tok = x.shape[0]
    return pl.pallas_call(
        _norm_gates_kernel,
        grid=(n_tok // tm,),
        in_specs=[
            pl.BlockSpec((tm, D_MODEL), lambda i: (i, 0)),
            pl.BlockSpec((1, D_MODEL), lambda i: (0, 0)),
            pl.BlockSpec((D_MODEL, LANES), lambda i: (0, 0)),
        ],
        out_specs=[
            pl.BlockSpec((tm, D_MODEL), lambda i: (i, 0)),
            pl.BlockSpec((tm, LANES), lambda i: (i, 0)),
        ],
        out_shape=[
            jax.ShapeDtypeStruct((n_tok, D_MODEL), BF16),
            jax.ShapeDtypeStruct((n_tok, LANES), F32),
        ],
        compiler_params=_cparams(("parallel",)),
        name="norm_gates",
    )(x, g, w_gates)


def _inproj_kernel(a_ref, w_ref, z_ref, w_scr):
    @pl.when(pl.program_id(1) == 0)
    def _():
        w_scr[...] = w_ref[0].astype(BF16)

    z_ref[...] = jnp.dot(a_ref[...], w_scr[...], preferred_element_type=F32)


def _inproj(a, w_in, tm, tn):
    n_tok = a.shape[0]
    return pl.pallas_call(
        _inproj_kernel,
        grid=(MAIN_COLS // tn, n_tok // tm),
        in_specs=[
            pl.BlockSpec((tm, D_MODEL), lambda j, i: (i, 0)),
            pl.BlockSpec((1, D_MODEL, tn), lambda j, i: (0, 0, j)),
        ],
        out_specs=pl.BlockSpec((tm, tn), lambda j, i: (i, j)),
        out_shape=jax.ShapeDtypeStruct((n_tok, MAIN_COLS), F32),
        scratch_shapes=[pltpu.VMEM((D_MODEL, tn), BF16)],
        compiler_params=_cparams(("arbitrary", "arbitrary")),
        name="inproj",
    )(a, w_in)


def _conv_chunk(zb, zc, zh, prev, wc, g):
    tt = zb.shape[0]
    u = zc * zh
    row = lax.broadcasted_iota(jnp.int32, (tt, 1), 0)
    u1 = jnp.where(row == 0, prev[7:8], pltpu.roll(u, 1, axis=0))
    u2 = jnp.where(row == 0, prev[6:7], jnp.where(row == 1, prev[7:8], pltpu.roll(u, 2, axis=0)))
    conv = wc[0:1] * u2 + wc[1:2] * u1 + wc[2:3] * u
    return _rms(zb * conv, g).astype(BF16), u


def _tri_dot(tri_bf16, x, tri_first):
    out = None
    for part in _split3(x):
        d = (jnp.dot(tri_bf16, part, preferred_element_type=F32) if tri_first
             else jnp.dot(part, tri_bf16, preferred_element_type=F32))
        out = d if out is None else out + d
    return out


def _mixer_prompt_kernel(zb_ref, zc_ref, zh_ref, q_ref, k_ref, v_ref, o_ref, zg_ref,
                         wc_ref, gc_ref, bg_ref, gn_ref,
                         yc_ref, y_ref, st_out, c_out, n_out, m_out, carry, c_scr, n_scr, m_scr):
    L = CHUNK

    @pl.when(pl.program_id(1) == 0)
    def _():
        carry[...] = jnp.zeros_like(carry)
        c_scr[...] = jnp.zeros_like(c_scr)
        n_scr[...] = jnp.zeros_like(n_scr)
        m_scr[...] = jnp.zeros_like(m_scr)

    lane = lax.broadcasted_iota(jnp.int32, (L, LANES), 1)
    r_i = lax.broadcasted_iota(jnp.int32, (L, L), 0)
    c_i = lax.broadcasted_iota(jnp.int32, (L, L), 1)
    causal = c_i <= r_i
    tril = causal.astype(BF16)
    triu = (r_i <= c_i).astype(BF16)

    nbt = q_ref.shape[0]
    for bi in range(nbt):
        yc, u = _conv_chunk(zb_ref[bi], zc_ref[bi], zh_ref[bi], carry[bi], wc_ref[...], gc_ref[...])
        yc_ref[bi] = yc
        carry[bi] = u[L - 8:L]
        st_out[bi] = u[L - 2:L]

    prs = [(bi, h) for bi in range(nbt) for h in range(HEADS)]
    each = lambda fn: {pr: fn(*pr) for pr in prs}
    nt_dims = (((1,), (1,)), ((), ()))
    tn_dims = (((0,), (0,)), ((), ()))

    gates, gates_t, bcol_all, brow_all = {}, {}, {}, {}
    for bi in range(nbt):
        zg = zg_ref[bi] + bg_ref[...]
        gates[bi] = jnp.where(lane < HEADS, zg, _log_sigmoid(zg))
        gates_t[bi] = gates[bi].T
        bcol_all[bi] = _tri_dot(tril, gates[bi], True)
        brow_all[bi] = _tri_dot(triu, gates_t[bi], False)

    b_col = each(lambda bi, h: bcol_all[bi][:, HEADS + h:HEADS + h + 1])
    b_row = each(lambda bi, h: brow_all[bi][HEADS + h:HEADS + h + 1, :])
    i_col = each(lambda bi, h: gates[bi][:, h:h + 1])
    i_row = each(lambda bi, h: gates_t[bi][h:h + 1, :])
    b_end = each(lambda bi, h: b_col[bi, h][L - 1:L, :])
    m0 = each(lambda bi, h: m_scr[bi, h:h + 1, 0:1])
    n0 = each(lambda bi, h: n_scr[bi, h:h + 1, :])
    c0 = each(lambda bi, h: c_scr[bi, h])
    q = each(lambda bi, h: q_ref[bi, :, h * DK:(h + 1) * DK])
    k = each(lambda bi, h: k_ref[bi, :, h * DK:(h + 1) * DK] * (DK ** -0.5))
    qb = each(lambda bi, h: q[bi, h].astype(BF16))
    vb = each(lambda bi, h: v_ref[bi, :, h * DV:(h + 1) * DV].astype(BF16))

    dmat = each(lambda bi, h: jnp.where(causal, b_col[bi, h] - b_row[bi, h] + i_row[bi, h], NEG_BIG))
    inter = each(lambda bi, h: b_col[bi, h] + m0[bi, h])
    m_t = each(lambda bi, h: jnp.maximum(inter[bi, h], jnp.max(dmat[bi, h], axis=1, keepdims=True)))
    w_int = each(lambda bi, h: jnp.exp(inter[bi, h] - m_t[bi, h]))
    p = each(lambda bi, h: jnp.where(causal, jnp.exp(dmat[bi, h] - m_t[bi, h]), 0.0))
    s = each(lambda bi, h: lax.dot_general(qb[bi, h], k[bi, h].astype(BF16), nt_dims,
                                           preferred_element_type=F32) * p[bi, h])
    num = each(lambda bi, h: jnp.dot(s[bi, h].astype(BF16), vb[bi, h], preferred_element_type=F32)
               + w_int[bi, h] * jnp.dot(qb[bi, h], c0[bi, h].astype(BF16), preferred_element_type=F32))
    den = each(lambda bi, h: jnp.sum(s[bi, h], axis=1, keepdims=True)
               + w_int[bi, h] * jnp.sum(q[bi, h] * n0[bi, h], axis=1, keepdims=True))
    hh = each(lambda bi, h: num[bi, h] / jnp.maximum(jnp.abs(den[bi, h]), jnp.exp(-m_t[bi, h])))
    hn = each(lambda bi, h: hh[bi, h] * lax.rsqrt(jnp.mean(hh[bi, h] * hh[bi, h], axis=-1, keepdims=True)
                                                  + EPS) * gn_ref[:, h * DV:(h + 1) * DV])
    for bi, h in prs:
        y = jax.nn.sigmoid(o_ref[bi, :, h * DV:(h + 1) * DV]) * hn[bi, h]
        y_ref[bi, :, h * DV:(h + 1) * DV] = y.astype(BF16)

    g_col = each(lambda bi, h: b_end[bi, h] - b_col[bi, h] + i_col[bi, h])
    g_row = each(lambda bi, h: b_end[bi, h] - b_row[bi, h] + i_row[bi, h])
    m_new = each(lambda bi, h: jnp.maximum(b_end[bi, h] + m0[bi, h],
                                           jnp.max(g_row[bi, h], axis=1, keepdims=True)))
    decay = each(lambda bi, h: jnp.exp(b_end[bi, h] + m0[bi, h] - m_new[bi, h]))
    kw = each(lambda bi, h: k[bi, h] * jnp.exp(g_col[bi, h] - m_new[bi, h]))
    c_new = each(lambda bi, h: decay[bi, h] * c0[bi, h]
                 + lax.dot_general(kw[bi, h].astype(BF16), vb[bi, h], tn_dims, preferred_element_type=F32))
    n_new = each(lambda bi, h: decay[bi, h] * n0[bi, h] + jnp.sum(kw[bi, h], axis=0, keepdims=True))
    for bi, h in prs:
        c_scr[bi, h] = c_new[bi, h]
        n_scr[bi, h:h + 1, :] = n_new[bi, h]
        m_scr[bi, h:h + 1, :] = jnp.broadcast_to(m_new[bi, h], (1, LANES))
        c_out[bi, h] = c_new[bi, h]
        n_out[bi, h:h + 1, :] = n_new[bi, h]
        m_out[bi, h:h + 1, :] = jnp.broadcast_to(m_new[bi, h], (1, LANES))


def _mixer_prompt(z3, zg3, w_conv, g_conv, b_gates, g_norm, nbt):
    batch, seq, _ = z3.shape
    nc = seq // CHUNK
    cb, qw, vw = CONV_DIM, HEADS * DK, HEADS * DV
    zcol = lambda w, col: pl.BlockSpec((nbt, CHUNK, w), lambda b, c: (b, c, col // w))
    const = lambda shape: pl.BlockSpec(shape, lambda b, c: (0,) * len(shape))
    per_seq = lambda *tail: pl.BlockSpec((nbt,) + tail, lambda b, c: (b,) + (0,) * len(tail))
    return pl.pallas_call(
        _mixer_prompt_kernel,
        grid=(batch // nbt, nc),
        in_specs=[
            zcol(cb, COL_B), zcol(cb, COL_C), zcol(cb, COL_H),
            zcol(qw, COL_Q), zcol(qw, COL_K), zcol(vw, COL_V), zcol(vw, COL_O),
            pl.BlockSpec((nbt, CHUNK, LANES), lambda b, c: (b, c, 0)),
            const((8, cb)), const((1, cb)), const((1, LANES)), const((1, vw)),
        ],
        out_specs=[
            pl.BlockSpec((nbt, CHUNK, cb), lambda b, c: (b, c, 0)),
            pl.BlockSpec((nbt, CHUNK, vw), lambda b, c: (b, c, 0)),
            per_seq(2, cb), per_seq(HEADS, DK, DV), per_seq(HEADS, DK), per_seq(HEADS, LANES),
        ],
        out_shape=[
            jax.ShapeDtypeStruct((batch, seq, cb), BF16),
            jax.ShapeDtypeStruct((batch, seq, vw), BF16),
            jax.ShapeDtypeStruct((batch, 2, cb), F32),
            jax.ShapeDtypeStruct((batch, HEADS, DK, DV), F32),
            jax.ShapeDtypeStruct((batch, HEADS, DK), F32),
            jax.ShapeDtypeStruct((batch, HEADS, LANES), F32),
        ],
        scratch_shapes=[
            pltpu.VMEM((nbt, 8, cb), F32),
            pltpu.VMEM((nbt, HEADS, DK, DV), F32),
            pltpu.VMEM((nbt, 8, DK), F32),
            pltpu.VMEM((nbt, 8, LANES), F32),
        ],
        compiler_params=_cparams(("parallel", "arbitrary")),
        name="mixer_prompt",
    )(z3, z3, z3, z3, z3, z3, z3, zg3, w_conv, g_conv, b_gates, g_norm)


def _mixer_sample_kernel(zb_ref, zc_ref, zh_ref, q_ref, k_ref, v_ref, o_ref, zg_ref,
                         st_ref, c_ref, n_ref, m_ref, wc_ref, gc_ref, bg_ref, gn_ref,
                         yc_ref, ym_ref, st_out, c_out, n_out, m_out):
    sb = zb_ref.shape[0]
    u = zc_ref[...] * zh_ref[...]
    s0 = st_ref[:, 0:CONV_DIM]
    s1 = st_ref[:, CONV_DIM:2 * CONV_DIM]
    wc = wc_ref[...]
    conv = wc[0:1] * s0 + wc[1:2] * s1 + wc[2:3] * u
    yc_ref[...] = _rms(zb_ref[...] * conv, gc_ref[...]).astype(BF16)
    st_out[:, 0:CONV_DIM] = s1
    st_out[:, CONV_DIM:2 * CONV_DIM] = u

    lane_row = lax.broadcasted_iota(jnp.int32, (1, LANES), 1)
    eye = (lax.broadcasted_iota(jnp.int32, (DK, DK), 0)
           == lax.broadcasted_iota(jnp.int32, (DK, DK), 1)).astype(F32)

    def to_col(row):
        return jnp.sum(eye * row, axis=1, keepdims=True)

    def pick(row, j):
        return jnp.sum(jnp.where(lane_row == j, row, 0.0), axis=1, keepdims=True)

    def per_group(gi, carry):
        ss = [gi * SAMPLE_GROUP + j for j in range(SAMPLE_GROUP)]
        prs = [(j, h) for j in range(SAMPLE_GROUP) for h in range(HEADS)]
        each = lambda fn: {pr: fn(*pr) for pr in prs}
        zg = [zg_ref[s] + bg_ref[...] for s in ss]
        lf_all = [_log_sigmoid(z) for z in zg]
        m_row = [m_ref[s] for s in ss]

        q = each(lambda j, h: q_ref[ss[j], :, h * DK:(h + 1) * DK])
        k = each(lambda j, h: k_ref[ss[j], :, h * DK:(h + 1) * DK] * (DK ** -0.5))
        v = each(lambda j, h: v_ref[ss[j], :, h * DV:(h + 1) * DV])
        c0 = each(lambda j, h: c_ref[ss[j], h])
        n0 = each(lambda j, h: n_ref[ss[j], h:h + 1, :])
        i_pre = each(lambda j, h: pick(zg[j], h))
        lf = each(lambda j, h: pick(lf_all[j], HEADS + h))
        m0 = each(lambda j, h: pick(m_row[j], h))
        q_col = each(lambda j, h: to_col(q[j, h]))
        k_col = each(lambda j, h: to_col(k[j, h]))
        inter = each(lambda j, h: lf[j, h] + m0[j, h])
        m_t = each(lambda j, h: jnp.maximum(inter[j, h], i_pre[j, h]))
        w_int = each(lambda j, h: jnp.exp(inter[j, h] - m_t[j, h]))
        p = each(lambda j, h: jnp.exp(i_pre[j, h] - m_t[j, h]))
        sc = each(lambda j, h: jnp.sum(q[j, h] * k[j, h], axis=1, keepdims=True) * p[j, h])
        qc = each(lambda j, h: jnp.sum(q_col[j, h] * c0[j, h], axis=0, keepdims=True))
        num = each(lambda j, h: sc[j, h] * v[j, h] + w_int[j, h] * qc[j, h])
        den = each(lambda j, h: sc[j, h] + w_int[j, h] * jnp.sum(q[j, h] * n0[j, h], axis=1, keepdims=True))
        hh = each(lambda j, h: num[j, h] / jnp.maximum(jnp.abs(den[j, h]), jnp.exp(-m_t[j, h])))
        hn = each(lambda j, h: hh[j, h] * lax.rsqrt(jnp.mean(hh[j, h] * hh[j, h], axis=-1, keepdims=True) + EPS)
                  * gn_ref[:, h * DV:(h + 1) * DV])
        for j, h in prs:
            o = o_ref[ss[j], :, h * DV:(h + 1) * DV]
            ym_ref[ss[j], :, h * DV:(h + 1) * DV] = jax.nn.sigmoid(o) * hn[j, h]
            c_out[ss[j], h] = w_int[j, h] * c0[j, h] + (p[j, h] * k_col[j, h]) * v[j, h]
            n_out[ss[j], h:h + 1, :] = w_int[j, h] * n0[j, h] + p[j, h] * k[j, h]
        for j in range(SAMPLE_GROUP):
            m_new_row = jnp.zeros((1, LANES), F32)
            for h in range(HEADS):
                m_new_row = jnp.where(lane_row == h, m_t[j, h], m_new_row)
            m_out[ss[j]] = m_new_row
        return carry

    lax.fori_loop(0, sb // SAMPLE_GROUP, per_group, 0)


def _mixer_sample(z, zs3, zg3, st, c_s, n_s, m3, w_conv, g_conv, b_gates, g_norm, row0, n_dec, sb):
    cb, qw, vw = CONV_DIM, HEADS * DK, HEADS * DV
    r0 = row0 // sb
    row = lambda i: r0 + i
    return pl.pallas_call(
        _mixer_sample_kernel,
        grid=(n_dec // sb,),
        in_specs=[
            pl.BlockSpec((sb, cb), lambda i: (row(i), COL_B // cb)),
            pl.BlockSpec((sb, cb), lambda i: (row(i), COL_C // cb)),
            pl.BlockSpec((sb, cb), lambda i: (row(i), COL_H // cb)),
            pl.BlockSpec((sb, 1, qw), lambda i: (i, 0, COL_Q // qw)),
            pl.BlockSpec((sb, 1, qw), lambda i: (i, 0, COL_K // qw)),
            pl.BlockSpec((sb, 1, vw), lambda i: (i, 0, COL_V // vw)),
            pl.BlockSpec((sb, 1, vw), lambda i: (i, 0, COL_O // vw)),
            pl.BlockSpec((sb, 1, LANES), lambda i: (i, 0, 0)),
            pl.BlockSpec((sb, 2 * cb), lambda i: (i, 0)),
            pl.BlockSpec((sb, HEADS, DK, DV), lambda i: (i, 0, 0, 0)),
            pl.BlockSpec((sb, HEADS, DK), lambda i: (i, 0, 0)),
            pl.BlockSpec((sb, 1, LANES), lambda i: (i, 0, 0)),
            pl.BlockSpec((8, cb), lambda i: (0, 0)),
            pl.BlockSpec((1, cb), lambda i: (0, 0)),
            pl.BlockSpec((1, LANES), lambda i: (0, 0)),
            pl.BlockSpec((1, vw), lambda i: (0, 0)),
        ],
        out_specs=[
            pl.BlockSpec((sb, cb), lambda i: (i, 0)),
            pl.BlockSpec((sb, 1, vw), lambda i: (i, 0, 0)),
            pl.BlockSpec((sb, 2 * cb), lambda i: (i, 0)),
            pl.BlockSpec((sb, HEADS, DK, DV), lambda i: (i, 0, 0, 0)),
            pl.BlockSpec((sb, HEADS, DK), lambda i: (i, 0, 0)),
            pl.BlockSpec((sb, 1, LANES), lambda i: (i, 0, 0)),
        ],
        out_shape=[
            jax.ShapeDtypeStruct((n_dec, cb), BF16),
            jax.ShapeDtypeStruct((n_dec, 1, vw), F32),
            jax.ShapeDtypeStruct((n_dec, 2 * cb), F32),
            jax.ShapeDtypeStruct((n_dec, HEADS, DK, DV), F32),
            jax.ShapeDtypeStruct((n_dec, HEADS, DK), F32),
            jax.ShapeDtypeStruct((n_dec, 1, LANES), F32),
        ],
        compiler_params=_cparams(("arbitrary",)),
        name="mixer_sample",
    )(z, z, z, zs3, zs3, zs3, zs3, zg3, st, c_s, n_s, m3, w_conv, g_conv, b_gates, g_norm)


def _outproj_router_kernel(yc_ref, ym_ref, x_ref, wo1_ref, wo2_ref, g_ref, wr2_ref, br_ref,
                           cin_ref, h1_ref, xn_ref, e_ref, gate_ref, rank_ref, cnt_ref, carry, earlier):
    tm = x_ref.shape[0]

    @pl.when(pl.program_id(0) == 0)
    def _():
        carry[...] = cin_ref[...]
        r_i = lax.broadcasted_iota(jnp.int32, (tm, tm), 0)
        c_i = lax.broadcasted_iota(jnp.int32, (tm, tm), 1)
        earlier[...] = (c_i < r_i).astype(BF16)

    mix = (jnp.dot(yc_ref[...], wo1_ref[...], preferred_element_type=F32)
           + jnp.dot(ym_ref[...], wo2_ref[...], preferred_element_type=F32))
    h1 = x_ref[...] + mix
    h1_ref[...] = h1
    xn = _rms(h1, g_ref[...])
    xn_ref[...] = xn

    xh = xn.astype(BF16)
    xl = (xn - xh.astype(F32)).astype(BF16)
    rh = jnp.dot(xh, wr2_ref[...], preferred_element_type=F32)
    rl = jnp.dot(xl, wr2_ref[...], preferred_element_type=F32)
    logits = (rh[:, :LANES] + rh[:, LANES:]) + (rl[:, :LANES] + rl[:, LANES:]) + br_ref[...]
    lane = lax.broadcasted_iota(jnp.int32, (tm, LANES), 1)
    lane_f = lane.astype(F32)
    work = jnp.where(lane < N_EXPERTS, logits, NEG_BIG)

    tops, idxs = [], []
    chosen = jnp.zeros((tm, LANES), F32)
    for _ in range(TOP_K):
        mx = jnp.max(work, axis=1, keepdims=True)
        idx = jnp.min(jnp.where(work == mx, lane_f, float(LANES)), axis=1, keepdims=True)
        sel = lane_f == idx
        tops.append(mx)
        idxs.append(idx)
        chosen = jnp.where(sel, 1.0, chosen)
        work = jnp.where(sel, NEG_BIG, work)

    exps = [jnp.exp(t - tops[0]) for t in tops]
    denom = exps[0] + exps[1] + exps[2] + exps[3]

    before = jnp.dot(earlier[...], chosen.astype(BF16), preferred_element_type=F32) + carry[...]

    e_out = jnp.zeros((tm, LANES), F32)
    g_out = jnp.zeros((tm, LANES), F32)
    r_out = jnp.zeros((tm, LANES), F32)
    for kk in range(TOP_K):
        rank = jnp.sum(jnp.where(lane_f == idxs[kk], before, 0.0), axis=1, keepdims=True)
        e_out = jnp.where(lane == kk, idxs[kk], e_out)
        g_out = jnp.where(lane == kk, exps[kk] / denom, g_out)
        r_out = jnp.where(lane == kk, rank, r_out)
    gate_ref[...] = g_out
    for c in range(tm // LANES):
        rows = slice(c * LANES, (c + 1) * LANES)
        e_ref[:, rows] = e_out[rows].T[0:8].astype(jnp.int32)
        rank_ref[:, rows] = r_out[rows].T[0:8].astype(jnp.int32)

    new_carry = carry[...] + jnp.sum(chosen, axis=0, keepdims=True)
    carry[...] = new_carry
    cnt_ref[...] = new_carry


def _outproj_router(y_conv, y_mlstm, x, wo1, wo2, g_ffn, wr2, b_router, cnt_in, tm):
    n_tok = x.shape[0]
    full = lambda shape: pl.BlockSpec(shape, lambda i: (0,) * len(shape))
    tile = lambda w: pl.BlockSpec((tm, w), lambda i: (i, 0))
    choice_major = pl.BlockSpec((8, tm), lambda i: (0, i))
    return pl.pallas_call(
        _outproj_router_kernel,
        grid=(n_tok // tm,),
        in_specs=[
            tile(CONV_DIM), tile(HEADS * DV), tile(D_MODEL),
            full((CONV_DIM, D_MODEL)), full((HEADS * DV, D_MODEL)), full((1, D_MODEL)),
            full((D_MODEL, 2 * LANES)), full((1, LANES)), full((1, LANES)),
        ],
        out_specs=[tile(D_MODEL), tile(D_MODEL), choice_major, tile(LANES), choice_major, full((1, LANES))],
        out_shape=[
            jax.ShapeDtypeStruct((n_tok, D_MODEL), F32),
            jax.ShapeDtypeStruct((n_tok, D_MODEL), F32),
            jax.ShapeDtypeStruct((8, n_tok), jnp.int32),
            jax.ShapeDtypeStruct((n_tok, LANES), F32),
            jax.ShapeDtypeStruct((8, n_tok), jnp.int32),
            jax.ShapeDtypeStruct((1, LANES), F32),
        ],
        scratch_shapes=[pltpu.VMEM((1, LANES), F32), pltpu.VMEM((tm, tm), BF16)],
        compiler_params=_cparams(("arbitrary",)),
        name="outproj_router",
    )(y_conv, y_mlstm, x, wo1, wo2, g_ffn, wr2, b_router, cnt_in)


def _zero_fill_rows(dst, zero_scr, sem, start, count, act):
    low = count & 7
    for t in range(7):
        @pl.when(t < low)
        def _(t=t):
            act(pltpu.make_async_copy(zero_scr.at[pl.ds(0, 1)], dst.at[pl.ds(start + t, 1)], sem))

    off = start + low
    bit = 8
    while bit < SUB:
        take = count & bit

        @pl.when(take != 0)
        def _(off=off, bit=bit):
            row = pl.multiple_of(off, 8)
            act(pltpu.make_async_copy(zero_scr.at[pl.ds(0, bit)], dst.at[pl.ds(row, bit)], sem))

        off = off + take
        bit *= 2


def _zero_fill_chunks(dst, zero_scr, sem, start, act):
    def body(j, c):
        row = pl.multiple_of(start + j * SUB, SUB)
        act(pltpu.make_async_copy(zero_scr, dst.at[pl.ds(row, SUB)], sem))
        return c

    lax.fori_loop(0, (dst.shape[0] - start) // SUB, body, 0)


def _dispatch_kernel(d0_ref, d1_ref, d2_ref, d3_ref, cnt_ref, pstart_ref, xn_ref, xn_s_ref, xs_out,
                     zero_scr, sem, zsem):
    dest_refs = (d0_ref, d1_ref, d2_ref, d3_ref)
    tm = xn_ref.shape[0]
    step = pl.program_id(0)
    last = pl.num_programs(0) - 1

    def fill(act):
        for e in range(N_EXPERTS):
            c = cnt_ref[e]
            _zero_fill_rows(xs_out, zero_scr, zsem, pstart_ref[e] + c, (SUB - c % SUB) % SUB, act)
        _zero_fill_chunks(xs_out, zero_scr, zsem, pstart_ref[N_EXPERTS], act)

    @pl.when(step == 0)
    def _():
        zero_scr[...] = jnp.zeros_like(zero_scr)
        fill(lambda cp: cp.start())

    def scatter_rows(src, base):
        rows = src.shape[0]

        def issue(r, c):
            for kk in range(TOP_K):
                d = dest_refs[kk][base + r]
                pltpu.make_async_copy(src.at[pl.ds(r, 1)], xs_out.at[pl.ds(d, 1)], sem).start(priority=kk % 2)
            return c

        lax.fori_loop(0, rows, issue, 0, unroll=ISSUE_UNROLL)
        for _ in range(TOP_K):
            pltpu.make_async_copy(src, xs_out.at[pl.ds(0, rows)], sem).wait()

    scatter_rows(xn_ref, step * tm)

    @pl.when(step == last)
    def _():
        scatter_rows(xn_s_ref, pl.num_programs(0) * tm)

    @pl.when(step == 0)
    def _():
        fill(lambda cp: cp.wait())


def _dispatch(dest, counts, pstart, xn_p, xn_s, n_slots, tm):
    n_prompt = xn_p.shape[0]
    return pl.pallas_call(
        _dispatch_kernel,
        grid_spec=pltpu.PrefetchScalarGridSpec(
            num_scalar_prefetch=TOP_K + 2,
            grid=(n_prompt // tm,),
            in_specs=[pl.BlockSpec((tm, D_MODEL), lambda i, *_: (i, 0)),
                      pl.BlockSpec(xn_s.shape, lambda i, *_: (0, 0))],
            out_specs=pl.BlockSpec(memory_space=pl.ANY),
            scratch_shapes=[pltpu.VMEM((SUB, D_MODEL), F32), pltpu.SemaphoreType.DMA(()),
                            pltpu.SemaphoreType.DMA(())],
        ),
        out_shape=jax.ShapeDtypeStruct((n_slots, D_MODEL), F32),
        compiler_params=_cparams(("arbitrary",)),
        name="dispatch",
    )(*[dest[k] for k in range(TOP_K)], counts, pstart, xn_p, xn_s)


def _expert_kernel(se_ref, st_ref, sn_ref, end_ref, xs_ref, wg_ref, wu_ref, bg_ref, bu_ref, wd_ref, bd_ref,
                   y_ref, x_scr, xb_scr, acc, zero_scr, sem, zsem, xsem):
    i, f = pl.program_id(0), pl.program_id(1)
    nf = pl.num_programs(1)
    tmb = acc.shape[0]
    n = sn_ref[i]
    start = st_ref[i]

    def x_copy(j):
        row = pl.multiple_of(st_ref[j], SUB)
        return pltpu.make_async_copy(xs_ref.at[pl.ds(row, tmb)], x_scr, xsem)

    @pl.when((i == 0) & (f == 0))
    def _():
        zero_scr[...] = jnp.zeros_like(zero_scr)
        _zero_fill_chunks(y_ref, zero_scr, zsem, end_ref[0], lambda cp: cp.start())
        _zero_fill_chunks(y_ref, zero_scr, zsem, end_ref[0], lambda cp: cp.wait())

    kc = D_MODEL // MOE_SPLIT

    def y_copy(o, mc, c):
        row = pl.multiple_of(start + o, SUB)
        cols = pl.ds(c * kc, kc)
        return pltpu.make_async_copy(acc.at[pl.ds(o, mc), cols], y_ref.at[pl.ds(row, mc), cols], sem)

    def for_each_sub_block(fn):
        kmax = tmb // SUB
        quarter = SUB // 4
        nq = (n + quarter - 1) // quarter
        for kq in range(4 * kmax, 4 * (kmax - 1), -1):
            pl.when(nq == kq)(functools.partial(fn, 0, kq * quarter, kmax * SUB))
        pl.when((nq <= 4 * (kmax - 1)) & (nq > 4 * (kmax - 2)))(
            functools.partial(fn, 0, (kmax - 1) * SUB, (kmax - 1) * SUB))

        @pl.when(nq <= 4 * (kmax - 2))
        def _():
            def piece(j, c):
                fn(pl.multiple_of(j * SUB, SUB), SUB, SUB)
                return c

            lax.fori_loop(0, (n + SUB - 1) // SUB, piece, 0)

    @pl.when(n > 0)
    def _():
        @pl.when(f == 0)
        def _():
            @pl.when(i == 0)
            def _():
                x_copy(0).start()

            x_copy(i).wait()
            xb_scr[...] = x_scr[...].astype(BF16)
            nxt = jnp.minimum(i + 1, pl.num_programs(0) - 1)

            @pl.when((i + 1 < pl.num_programs(0)) & (sn_ref[nxt] > 0))
            def _():
                x_copy(nxt).start()

            acc[...] = jnp.broadcast_to(bd_ref[0], acc.shape)

        def sub_block(o, m, mc):
            g = bg_ref[0]
            u = bu_ref[0]
            for c in range(MOE_SPLIT):
                xb = xb_scr[pl.ds(o, m), c * kc:(c + 1) * kc]
                g = g + jnp.dot(xb, wg_ref[0, c * kc:(c + 1) * kc, :].astype(BF16), preferred_element_type=F32)
                u = u + jnp.dot(xb, wu_ref[0, c * kc:(c + 1) * kc, :].astype(BF16), preferred_element_type=F32)
            g = jnp.minimum(g, SWIGLU_LIMIT)
            u = jnp.clip(u, -SWIGLU_LIMIT, SWIGLU_LIMIT)
            act = (g * jax.nn.sigmoid(SWIGLU_ALPHA * g) * (u + 1.0)).astype(BF16)
            for c in range(MOE_SPLIT):
                cols = slice(c * kc, (c + 1) * kc)
                acc[pl.ds(o, m), cols] += jnp.dot(act, wd_ref[0, :, cols].astype(BF16),
                                                  preferred_element_type=F32)

                @pl.when(f == nf - 1)
                def _(c=c):
                    y_copy(o, mc, c).start()

        for_each_sub_block(sub_block)

        @pl.when(f == nf - 1)
        def _():
            def drain(o, m, mc):
                for c in range(MOE_SPLIT):
                    y_copy(o, mc, c).wait()

            for_each_sub_block(drain)


def _experts(sb_e, sb_start, sb_n, end, xs, w_gate_up, b_gate_up, w_down, b_down, nsb, tmb, tf):
    n_slots = xs.shape[0]
    nf = D_FF // tf

    def fidx(i, f, sn):
        return jnp.where(sn[i] > 0, f, nf - 1)

    return pl.pallas_call(
        _expert_kernel,
        grid_spec=pltpu.PrefetchScalarGridSpec(
            num_scalar_prefetch=4,
            grid=(nsb, nf),
            in_specs=[
                pl.BlockSpec(memory_space=pl.ANY),
                pl.BlockSpec((1, D_MODEL, tf), lambda i, f, se, st, sn, en: (se[i], 0, fidx(i, f, sn))),
                pl.BlockSpec((1, D_MODEL, tf), lambda i, f, se, st, sn, en: (se[i], 0, nf + fidx(i, f, sn))),
                pl.BlockSpec((1, 1, tf), lambda i, f, se, st, sn, en: (se[i], 0, fidx(i, f, sn))),
                pl.BlockSpec((1, 1, tf), lambda i, f, se, st, sn, en: (se[i], 0, nf + fidx(i, f, sn))),
                pl.BlockSpec((1, tf, D_MODEL), lambda i, f, se, st, sn, en: (se[i], fidx(i, f, sn), 0)),
                pl.BlockSpec((1, 1, D_MODEL), lambda i, f, se, st, sn, en: (se[i], 0, 0)),
            ],
            out_specs=pl.BlockSpec(memory_space=pl.ANY),
            scratch_shapes=[pltpu.VMEM((tmb, D_MODEL), F32), pltpu.VMEM((tmb, D_MODEL), BF16),
                            pltpu.VMEM((tmb, D_MODEL), F32), pltpu.VMEM((SUB, D_MODEL), F32),
                            pltpu.SemaphoreType.DMA(()), pltpu.SemaphoreType.DMA(()),
                            pltpu.SemaphoreType.DMA(())],
        ),
        out_shape=jax.ShapeDtypeStruct((n_slots, D_MODEL), F32),
        compiler_params=_cparams(("arbitrary", "arbitrary")),
        name="experts",
    )(sb_e, sb_start, sb_n, end, xs, w_gate_up, w_gate_up, b_gate_up, b_gate_up, w_down, b_down)


def _combine_kernel(d0_ref, d1_ref, d2_ref, d3_ref, h1_ref, gate_ref, p_ref, yb_ref, wpg_ref, wpp_ref,
                    gp_ref, gf_ref, out_ref, gbuf_a, gbuf_b, sem):
    dest_refs = (d0_ref, d1_ref, d2_ref, d3_ref)
    tm = h1_ref.shape[0]
    step = pl.program_id(0)
    last = pl.num_programs(0) - 1
    bufs = (gbuf_a, gbuf_b)

    def row_gather(t, r, kk, s):
        d = dest_refs[kk][t * tm + r]
        return pltpu.make_async_copy(yb_ref.at[pl.ds(d, 1)], bufs[s].at[kk, pl.ds(r, 1)], sem.at[s])

    def wait_tile(s):
        for kk in range(TOP_K):
            pltpu.make_async_copy(yb_ref.at[pl.ds(0, tm)], bufs[s].at[kk], sem.at[s]).wait()

    @pl.when(step == 0)
    def _():
        def issue(r, c):
            for kk in range(TOP_K):
                row_gather(0, r, kk, 0).start(priority=kk % 2)
            return c

        lax.fori_loop(0, tm, issue, 0, unroll=ISSUE_UNROLL)

    def tile_step(cur):
        nxt = 1 - cur
        wait_tile(cur)
        t_next = jnp.minimum(step + 1, last)
        for r in range(tm):
            for kk in range(TOP_K):
                row_gather(t_next, r, kk, nxt).start(priority=kk % 2)

        gate = gate_ref[...]
        moe = gate[:, 0:1] * bufs[cur][0]
        for kk in range(1, TOP_K):
            moe = moe + gate[:, kk:kk + 1] * bufs[cur][kk]
        h2 = h1_ref[...] + moe
        a = _rms(h2, gp_ref[...]).astype(BF16)
        pg = jax.nn.sigmoid(jnp.dot(a, wpg_ref[...], preferred_element_type=F32))
        pp = jnp.dot(p_ref[...].astype(BF16), wpp_ref[...], preferred_element_type=F32)
        h3 = h2 + pg * pp
        out_ref[...] = _rms(h3, gf_ref[...])

        @pl.when(step == last)
        def _():
            wait_tile(nxt)

    pl.when(step % 2 == 0)(functools.partial(tile_step, 0))
    pl.when(step % 2 == 1)(functools.partial(tile_step, 1))


def _combine(dest, h1, gate, p, yb, wpg, wpp, g_ple, g_final, tm):
    n_tok = h1.shape[0]
    full = lambda shape: pl.BlockSpec(shape, lambda i, *_: (0,) * len(shape))
    tile = lambda w: pl.BlockSpec((tm, w), lambda i, *_: (i, 0))
    return pl.pallas_call(
        _combine_kernel,
        grid_spec=pltpu.PrefetchScalarGridSpec(
            num_scalar_prefetch=TOP_K,
            grid=(n_tok // tm,),
            in_specs=[
                tile(D_MODEL), tile(LANES), tile(PLE_DIM),
                pl.BlockSpec(memory_space=pl.ANY),
                full((D_MODEL, D_MODEL)), full((PLE_DIM, D_MODEL)), full((1, D_MODEL)), full((1, D_MODEL)),
            ],
            out_specs=tile(D_MODEL),
            scratch_shapes=[pltpu.VMEM((TOP_K, tm, D_MODEL), F32), pltpu.VMEM((TOP_K, tm, D_MODEL), F32),
                            pltpu.SemaphoreType.DMA((2,))],
        ),
        out_shape=jax.ShapeDtypeStruct((n_tok, D_MODEL), F32),
        compiler_params=_cparams(("arbitrary",)),
        name="combine",
    )(*[dest[k] for k in range(TOP_K)], h1, gate, p, yb, wpg, wpp, g_ple, g_final)


def kernel(x_prompt, x_sample, state_conv, state_mlstm_C, state_mlstm_n, state_mlstm_m, p_prompt, p_sample,
           g_mix, w_in, b_gates, w_conv, g_conv_out, g_mlstm_out, w_out, g_ffn, w_router, b_router,
           w_gate_up, b_gate_up, w_down, b_down, g_ple, w_ple_gate, w_ple_proj, g_final):
    batch, seq, _ = x_prompt.shape
    n_dec = x_sample.shape[0]
    assert w_in.shape[0] == 1 and x_sample.shape[1] == 1
    n_prompt = batch * seq
    n_tok = n_prompt + n_dec

    x_p = x_prompt.reshape(n_prompt, D_MODEL)
    x_s = x_sample.reshape(n_dec, D_MODEL)

    row = lambda a: a.reshape(1, -1)
    pad_lanes = lambda a: jnp.pad(a, ((0, 0), (0, LANES - a.shape[1])))
    w_gates = pad_lanes(w_in[0, :, MAIN_COLS:]).astype(BF16)
    bg = pad_lanes(row(b_gates[0]))
    wc = jnp.pad(w_conv[0], ((0, 8 - w_conv.shape[1]), (0, 0)))
    g_norm = row(g_mlstm_out[0])

    a_p, zg_p = _norm_gates(x_p, row(g_mix[0]), w_gates, tm=1024)
    a_s, zg_s = _norm_gates(x_s, row(g_mix[0]), w_gates, tm=n_dec)
    z_p = _inproj(a_p, w_in, tm=1024, tn=1024)
    z_s = _inproj(a_s, w_in, tm=n_dec, tn=1024)

    yc_p, ym_p, conv_p, c_p, n_p, m_p = _mixer_prompt(
        z_p.reshape(batch, seq, MAIN_COLS), zg_p.reshape(batch, seq, LANES), wc, row(g_conv_out[0]), bg,
        g_norm, nbt=4)
    yc_p = yc_p.reshape(n_prompt, CONV_DIM)
    ym_p = ym_p.reshape(n_prompt, HEADS * DV)
    yc_s, ym_s, conv_s, c_s, n_s, m_s = _mixer_sample(
        z_s, z_s.reshape(n_dec, 1, MAIN_COLS), zg_s.reshape(n_dec, 1, LANES),
        state_conv[0].reshape(n_dec, 2 * CONV_DIM), state_mlstm_C[0], state_mlstm_n[0],
        pad_lanes(state_mlstm_m[0]).reshape(n_dec, 1, LANES), wc, row(g_conv_out[0]), bg, g_norm,
        row0=0, n_dec=n_dec, sb=16)
    m_s = m_s[:, 0, :HEADS]
    ym_s = ym_s.reshape(n_dec, HEADS * DV).astype(BF16)

    wo = w_out[0].astype(BF16)
    wr = pad_lanes(w_router[0])
    wr_hi = wr.astype(BF16)
    wr_lo = (wr - wr_hi.astype(F32)).astype(BF16)
    wr2 = jnp.concatenate([wr_hi, wr_lo], axis=1)
    router_args = (wo[:CONV_DIM], wo[CONV_DIM:], row(g_ffn[0]), wr2, pad_lanes(row(b_router[0])))
    h1_p, xn_p, e_p, gate_p, rank_p, cnt_p = _outproj_router(
        yc_p, ym_p, x_p, *router_args, jnp.zeros((1, LANES), F32), tm=512)
    h1_s, xn_s, e_s, gate_s, rank_s, cnt = _outproj_router(
        yc_s, ym_s, x_s, *router_args, cnt_p, tm=n_dec)

    n_assign = n_tok * TOP_K
    tmb = MOE_TMB
    nsb = n_assign // tmb + N_EXPERTS
    n_slots = -(-(n_assign + N_EXPERTS * (SUB - 1)) // SUB) * SUB + tmb
    counts = cnt[0, :N_EXPERTS].astype(jnp.int32)
    padded = (counts + SUB - 1) // SUB * SUB
    pend = jnp.cumsum(padded)
    pstart = jnp.concatenate([jnp.zeros((1,), jnp.int32), pend]).astype(jnp.int32)
    sb_per_e = (counts + tmb - 1) // tmb
    sb_cum = jnp.cumsum(sb_per_e)
    n_used = sb_cum[-1]
    sb = jnp.minimum(jnp.arange(nsb, dtype=jnp.int32), n_used - 1)
    sb_e = jnp.minimum(jnp.sum((sb_cum[None, :] <= sb[:, None]).astype(jnp.int32), axis=1), N_EXPERTS - 1)
    sb_j = sb - (sb_cum - sb_per_e)[sb_e]
    sb_start = (pstart[sb_e] + sb_j * tmb).astype(jnp.int32)
    sb_n = jnp.where(jnp.arange(nsb) < n_used, jnp.clip(counts[sb_e] - sb_j * tmb, 0, tmb), 0).astype(jnp.int32)
    expert_ids = jnp.arange(N_EXPERTS, dtype=jnp.int32)[:, None, None]

    def slot_of(e, r):
        first = jnp.sum(jnp.where(e[None, :TOP_K] == expert_ids, pstart[:N_EXPERTS, None, None], 0), axis=0)
        return (first + r[:TOP_K]).astype(jnp.int32)
    dest_p = slot_of(e_p, rank_p)
    dest_s = slot_of(e_s, rank_s)

    xs = _dispatch(jnp.concatenate([dest_p, dest_s], axis=1), counts, pstart, xn_p, xn_s, n_slots, tm=512)
    yb = _experts(sb_e.astype(jnp.int32), sb_start, sb_n, pstart[N_EXPERTS:], xs, w_gate_up[0],
                  b_gate_up[0].reshape(N_EXPERTS, 1, 2 * D_FF), w_down[0],
                  b_down[0].reshape(N_EXPERTS, 1, D_MODEL), nsb, tmb, MOE_TF)

    ple_args = (w_ple_gate[0].astype(BF16), w_ple_proj[0].astype(BF16), row(g_ple[0]), row(g_final))
    out_p = _combine(dest_p, h1_p, gate_p, p_prompt[0].reshape(n_prompt, PLE_DIM), yb, *ple_args, tm=256)
    out_s = _combine(dest_s, h1_s, gate_s, p_sample[0].reshape(n_dec, PLE_DIM), yb, *ple_args, tm=n_dec)

    y_prompt = out_p.reshape(batch, seq, D_MODEL)
    y_sample = out_s.reshape(n_dec, 1, D_MODEL)
    return (y_prompt, y_sample,
            conv_p[None], c_p[None], n_p[None], m_p[None, :, :, 0],
            conv_s.reshape(1, n_dec, 2, CONV_DIM), c_s[None], n_s[None], m_s[None])
```

```python
import functools

import jax
import jax.numpy as jnp
from jax import lax
from jax.experimental import pallas as pl
from jax.experimental.pallas import tpu as pltpu

F32 = jnp.float32
BF16 = jnp.bfloat16

D_MODEL = 2048
CONV_DIM = 1024
HEADS = 4
DK = 128
DV = 256
CHUNK = 128
N_EXPERTS = 32
TOP_K = 4
D_FF = 2048
PLE_DIM = 256
SWIGLU_LIMIT = 7.0
SWIGLU_ALPHA = 1.702
EPS = 1e-6
LANES = 128
NEG_BIG = -1e30
MAIN_COLS = 3 * CONV_DIM + HEADS * (2 * DK + 2 * DV)

COL_B, COL_C, COL_H = 0, CONV_DIM, 2 * CONV_DIM
COL_Q = 3 * CONV_DIM
COL_K = COL_Q + HEADS * DK
COL_V = COL_K + HEADS * DK
COL_O = COL_V + HEADS * DV

VMEM_LIMIT = 56 * 1024 * 1024

SUB = 128
MOE_TMB = 9 * SUB
MOE_TF = 512
MOE_SPLIT = 4
ISSUE_UNROLL = 8
SAMPLE_GROUP = 2


def _cparams(sem, vmem=VMEM_LIMIT):
    return pltpu.CompilerParams(dimension_semantics=sem, vmem_limit_bytes=vmem)


def _rms(x, g):
    return x * lax.rsqrt(jnp.mean(x * x, axis=-1, keepdims=True) + EPS) * g


def _split3(x):
    x1 = x.astype(BF16)
    r1 = x - x1.astype(F32)
    x2 = r1.astype(BF16)
    x3 = (r1 - x2.astype(F32)).astype(BF16)
    return x1, x2, x3


def _log_sigmoid(x):
    return jnp.minimum(x, 0.0) - jnp.log(1.0 + jnp.exp(-jnp.abs(x)))


def _norm_gates_kernel(x_ref, g_ref, wg_ref, a_ref, zg_ref):
    a = _rms(x_ref[...], g_ref[...]).astype(BF16)
    a_ref[...] = a
    zg_ref[...] = jnp.dot(a, wg_ref[...], preferred_element_type=F32)


def _norm_gates(x, g, w_gates, tm):
    n_tok = x.shape[0]
    return pl.pallas_call(
        _norm_gates_kernel,
        grid=(n_tok // tm,),
        in_specs=[
            pl.BlockSpec((tm, D_MODEL), lambda i: (i, 0)),
            pl.BlockSpec((1, D_MODEL), lambda i: (0, 0)),
            pl.BlockSpec((D_MODEL, LANES), lambda i: (0, 0)),
        ],
        out_specs=[
            pl.BlockSpec((tm, D_MODEL), lambda i: (i, 0)),
            pl.BlockSpec((tm, LANES), lambda i: (i, 0)),
        ],
        out_shape=[
            jax.ShapeDtypeStruct((n_tok, D_MODEL), BF16),
            jax.ShapeDtypeStruct((n_tok, LANES), F32),
        ],
        compiler_params=_cparams(("parallel",)),
        name="norm_gates",
    )(x, g, w_gates)


def _inproj_kernel(a_ref, w_ref, z_ref, w_scr):
    @pl.when(pl.program_id(1) == 0)
    def _():
        w_scr[...] = w_ref[0].astype(BF16)

    z_ref[...] = jnp.dot(a_ref[...], w_scr[...], preferred_element_type=F32)


def _inproj(a, w_in, tm, tn):
    n_tok = a.shape[0]
    return pl.pallas_call(
        _inproj_kernel,
        grid=(MAIN_COLS // tn, n_tok // tm),
        in_specs=[
            pl.BlockSpec((tm, D_MODEL), lambda j, i: (i, 0)),
            pl.BlockSpec((1, D_MODEL, tn), lambda j, i: (0, 0, j)),
        ],
        out_specs=pl.BlockSpec((tm, tn), lambda j, i: (i, j)),
        out_shape=jax.ShapeDtypeStruct((n_tok, MAIN_COLS), F32),
        scratch_shapes=[pltpu.VMEM((D_MODEL, tn), BF16)],
        compiler_params=_cparams(("arbitrary", "arbitrary")),
        name="inproj",
    )(a, w_in)


def _conv_chunk(zb, zc, zh, prev, wc, g):
    tt = zb.shape[0]
    u = zc * zh
    row = lax.broadcasted_iota(jnp.int32, (tt, 1), 0)
    u1 = jnp.where(row == 0, prev[7:8], pltpu.roll(u, 1, axis=0))
    u2 = jnp.where(row == 0, prev[6:7], jnp.where(row == 1, prev[7:8], pltpu.roll(u, 2, axis=0)))
    conv = wc[0:1] * u2 + wc[1:2] * u1 + wc[2:3] * u
    return _rms(zb * conv, g).astype(BF16), u


def _tri_dot(tri_bf16, x, tri_first):
    out = None
    for part in _split3(x):
        d = (jnp.dot(tri_bf16, part, preferred_element_type=F32) if tri_first
             else jnp.dot(part, tri_bf16, preferred_element_type=F32))
        out = d if out is None else out + d
    return out


def _mixer_prompt_kernel(zb_ref, zc_ref, zh_ref, q_ref, k_ref, v_ref, o_ref, zg_ref,
                         wc_ref, gc_ref, bg_ref, gn_ref,
                         yc_ref, y_ref, st_out, c_out, n_out, m_out, carry, c_scr, n_scr, m_scr):
    L = CHUNK

    @pl.when(pl.program_id(1) == 0)
    def _():
        carry[...] = jnp.zeros_like(carry)
        c_scr[...] = jnp.zeros_like(c_scr)
        n_scr[...] = jnp.zeros_like(n_scr)
        m_scr[...] = jnp.zeros_like(m_scr)

    lane = lax.broadcasted_iota(jnp.int32, (L, LANES), 1)
    r_i = lax.broadcasted_iota(jnp.int32, (L, L), 0)
    c_i = lax.broadcasted_iota(jnp.int32, (L, L), 1)
    causal = c_i <= r_i
    tril = causal.astype(BF16)
    triu = (r_i <= c_i).astype(BF16)

    nbt = q_ref.shape[0]
    for bi in range(nbt):
        yc, u = _conv_chunk(zb_ref[bi], zc_ref[bi], zh_ref[bi], carry[bi], wc_ref[...], gc_ref[...])
        yc_ref[bi] = yc
        carry[bi] = u[L - 8:L]
        st_out[bi] = u[L - 2:L]

    prs = [(bi, h) for bi in range(nbt) for h in range(HEADS)]
    each = lambda fn: {pr: fn(*pr) for pr in prs}
    nt_dims = (((1,), (1,)), ((), ()))
    tn_dims = (((0,), (0,)), ((), ()))

    gates, gates_t, bcol_all, brow_all = {}, {}, {}, {}
    for bi in range(nbt):
        zg = zg_ref[bi] + bg_ref[...]
        gates[bi] = jnp.where(lane < HEADS, zg, _log_sigmoid(zg))
        gates_t[bi] = gates[bi].T
        bcol_all[bi] = _tri_dot(tril, gates[bi], True)
        brow_all[bi] = _tri_dot(triu, gates_t[bi], False)

    b_col = each(lambda bi, h: bcol_all[bi][:, HEADS + h:HEADS + h + 1])
    b_row = each(lambda bi, h: brow_all[bi][HEADS + h:HEADS + h + 1, :])
    i_col = each(lambda bi, h: gates[bi][:, h:h + 1])
    i_row = each(lambda bi, h: gates_t[bi][h:h + 1, :])
    b_end = each(lambda bi, h: b_col[bi, h][L - 1:L, :])
    m0 = each(lambda bi, h: m_scr[bi, h:h + 1, 0:1])
    n0 = each(lambda bi, h: n_scr[bi, h:h + 1, :])
    c0 = each(lambda bi, h: c_scr[bi, h])
    q = each(lambda bi, h: q_ref[bi, :, h * DK:(h + 1) * DK])
    k = each(lambda bi, h: k_ref[bi, :, h * DK:(h + 1) * DK] * (DK ** -0.5))
    qb = each(lambda bi, h: q[bi, h].astype(BF16))
    vb = each(lambda bi, h: v_ref[bi, :, h * DV:(h + 1) * DV].astype(BF16))

    dmat = each(lambda bi, h: jnp.where(causal, b_col[bi, h] - b_row[bi, h] + i_row[bi, h], NEG_BIG))
    inter = each(lambda bi, h: b_col[bi, h] + m0[bi, h])
    m_t = each(lambda bi, h: jnp.maximum(inter[bi, h], jnp.max(dmat[bi, h], axis=1, keepdims=True)))
    w_int = each(lambda bi, h: jnp.exp(inter[bi, h] - m_t[bi, h]))
    p = each(lambda bi, h: jnp.where(causal, jnp.exp(dmat[bi, h] - m_t[bi, h]), 0.0))
    s = each(lambda bi, h: lax.dot_general(qb[bi, h], k[bi, h].astype(BF16), nt_dims,
                                           preferred_element_type=F32) * p[bi, h])
    num = each(lambda bi, h: jnp.dot(s[bi, h].astype(BF16), vb[bi, h], preferred_element_type=F32)
               + w_int[bi, h] * jnp.dot(qb[bi, h], c0[bi, h].astype(BF16), preferred_element_type=F32))
    den = each(lambda bi, h: jnp.sum(s[bi, h], axis=1, keepdims=True)
               + w_int[bi, h] * jnp.sum(q[bi, h] * n0[bi, h], axis=1, keepdims=True))
    hh = each(lambda bi, h: num[bi, h] / jnp.maximum(jnp.abs(den[bi, h]), jnp.exp(-m_t[bi, h])))
    hn = each(lambda bi, h: hh[bi, h] * lax.rsqrt(jnp.mean(hh[bi, h] * hh[bi, h], axis=-1, keepdims=True)
                                                  + EPS) * gn_ref[:, h * DV:(h + 1) * DV])
    for bi, h in prs:
        y = jax.nn.sigmoid(o_ref[bi, :, h * DV:(h + 1) * DV]) * hn[bi, h]
        y_ref[bi, :, h * DV:(h + 1) * DV] = y.astype(BF16)

    g_col = each(lambda bi, h: b_end[bi, h] - b_col[bi, h] + i_col[bi, h])
    g_row = each(lambda bi, h: b_end[bi, h] - b_row[bi, h] + i_row[bi, h])
    m_new = each(lambda bi, h: jnp.maximum(b_end[bi, h] + m0[bi, h],
                                           jnp.max(g_row[bi, h], axis=1, keepdims=True)))
    decay = each(lambda bi, h: jnp.exp(b_end[bi, h] + m0[bi, h] - m_new[bi, h]))
    kw = each(lambda bi, h: k[bi, h] * jnp.exp(g_col[bi, h] - m_new[bi, h]))
    c_new = each(lambda bi, h: decay[bi, h] * c0[bi, h]
                 + lax.dot_general(kw[bi, h].astype(BF16), vb[bi, h], tn_dims, preferred_element_type=F32))
    n_new = each(lambda bi, h: decay[bi, h] * n0[bi, h] + jnp.sum(kw[bi, h], axis=0, keepdims=True))
    for bi, h in prs:
        c_scr[bi, h] = c_new[bi, h]
        n_scr[bi, h:h + 1, :] = n_new[bi, h]
        m_scr[bi, h:h + 1, :] = jnp.broadcast_to(m_new[bi, h], (1, LANES))
        c_out[bi, h] = c_new[bi, h]
        n_out[bi, h:h + 1, :] = n_new[bi, h]
        m_out[bi, h:h + 1, :] = jnp.broadcast_to(m_new[bi, h], (1, LANES))


def _mixer_prompt(z3, zg3, w_conv, g_conv, b_gates, g_norm, nbt):
    batch, seq, _ = z3.shape
    nc = seq // CHUNK
    cb, qw, vw = CONV_DIM, HEADS * DK, HEADS * DV
    zcol = lambda w, col: pl.BlockSpec((nbt, CHUNK, w), lambda b, c: (b, c, col // w))
    const = lambda shape: pl.BlockSpec(shape, lambda b, c: (0,) * len(shape))
    per_seq = lambda *tail: pl.BlockSpec((nbt,) + tail, lambda b, c: (b,) + (0,) * len(tail))
    return pl.pallas_call(
        _mixer_prompt_kernel,
        grid=(batch // nbt, nc),
        in_specs=[
            zcol(cb, COL_B), zcol(cb, COL_C), zcol(cb, COL_H),
            zcol(qw, COL_Q), zcol(qw, COL_K), zcol(vw, COL_V), zcol(vw, COL_O),
            pl.BlockSpec((nbt, CHUNK, LANES), lambda b, c: (b, c, 0)),
            const((8, cb)), const((1, cb)), const((1, LANES)), const((1, vw)),
        ],
        out_specs=[
            pl.BlockSpec((nbt, CHUNK, cb), lambda b, c: (b, c, 0)),
            pl.BlockSpec((nbt, CHUNK, vw), lambda b, c: (b, c, 0)),
            per_seq(2, cb), per_seq(HEADS, DK, DV), per_seq(HEADS, DK), per_seq(HEADS, LANES),
        ],
        out_shape=[
            jax.ShapeDtypeStruct((batch, seq, cb), BF16),
            jax.ShapeDtypeStruct((batch, seq, vw), BF16),
            jax.ShapeDtypeStruct((batch, 2, cb), F32),
            jax.ShapeDtypeStruct((batch, HEADS, DK, DV), F32),
            jax.ShapeDtypeStruct((batch, HEADS, DK), F32),
            jax.ShapeDtypeStruct((batch, HEADS, LANES), F32),
        ],
        scratch_shapes=[
            pltpu.VMEM((nbt, 8, cb), F32),
            pltpu.VMEM((nbt, HEADS, DK, DV), F32),
            pltpu.VMEM((nbt, 8, DK), F32),
            pltpu.VMEM((nbt, 8, LANES), F32),
        ],
        compiler_params=_cparams(("parallel", "arbitrary")),
        name="mixer_prompt",
    )(z3, z3, z3, z3, z3, z3, z3, zg3, w_conv, g_conv, b_gates, g_norm)


def _mixer_sample_kernel(zb_ref, zc_ref, zh_ref, q_ref, k_ref, v_ref, o_ref, zg_ref,
                         st_ref, c_ref, n_ref, m_ref, wc_ref, gc_ref, bg_ref, gn_ref,
                         yc_ref, ym_ref, st_out, c_out, n_out, m_out):
    sb = zb_ref.shape[0]
    u = zc_ref[...] * zh_ref[...]
    s0 = st_ref[:, 0:CONV_DIM]
    s1 = st_ref[:, CONV_DIM:2 * CONV_DIM]
    wc = wc_ref[...]
    conv = wc[0:1] * s0 + wc[1:2] * s1 + wc[2:3] * u
    yc_ref[...] = _rms(zb_ref[...] * conv, gc_ref[...]).astype(BF16)
    st_out[:, 0:CONV_DIM] = s1
    st_out[:, CONV_DIM:2 * CONV_DIM] = u

    lane_row = lax.broadcasted_iota(jnp.int32, (1, LANES), 1)
    eye = (lax.broadcasted_iota(jnp.int32, (DK, DK), 0)
           == lax.broadcasted_iota(jnp.int32, (DK, DK), 1)).astype(F32)

    def to_col(row):
        return jnp.sum(eye * row, axis=1, keepdims=True)

    def pick(row, j):
        return jnp.sum(jnp.where(lane_row == j, row, 0.0), axis=1, keepdims=True)

    def per_group(gi, carry):
        ss = [gi * SAMPLE_GROUP + j for j in range(SAMPLE_GROUP)]
        prs = [(j, h) for j in range(SAMPLE_GROUP) for h in range(HEADS)]
        each = lambda fn: {pr: fn(*pr) for pr in prs}
        zg = [zg_ref[s] + bg_ref[...] for s in ss]
        lf_all = [_log_sigmoid(z) for z in zg]
        m_row = [m_ref[s] for s in ss]

        q = each(lambda j, h: q_ref[ss[j], :, h * DK:(h + 1) * DK])
        k = each(lambda j, h: k_ref[ss[j], :, h * DK:(h + 1) * DK] * (DK ** -0.5))
        v = each(lambda j, h: v_ref[ss[j], :, h * DV:(h + 1) * DV])
        c0 = each(lambda j, h: c_ref[ss[j], h])
        n0 = each(lambda j, h: n_ref[ss[j], h:h + 1, :])
        i_pre = each(lambda j, h: pick(zg[j], h))
        lf = each(lambda j, h: pick(lf_all[j], HEADS + h))
        m0 = each(lambda j, h: pick(m_row[j], h))
        q_col = each(lambda j, h: to_col(q[j, h]))
        k_col = each(lambda j, h: to_col(k[j, h]))
        inter = each(lambda j, h: lf[j, h] + m0[j, h])
        m_t = each(lambda j, h: jnp.maximum(inter[j, h], i_pre[j, h]))
        w_int = each(lambda j, h: jnp.exp(inter[j, h] - m_t[j, h]))
        p = each(lambda j, h: jnp.exp(i_pre[j, h] - m_t[j, h]))
        sc = each(lambda j, h: jnp.sum(q[j, h] * k[j, h], axis=1, keepdims=True) * p[j, h])
        qc = each(lambda j, h: jnp.sum(q_col[j, h] * c0[j, h], axis=0, keepdims=True))
        num = each(lambda j, h: sc[j, h] * v[j, h] + w_int[j, h] * qc[j, h])
        den = each(lambda j, h: sc[j, h] + w_int[j, h] * jnp.sum(q[j, h] * n0[j, h], axis=1, keepdims=True))
        hh = each(lambda j, h: num[j, h] / jnp.maximum(jnp.abs(den[j, h]), jnp.exp(-m_t[j, h])))
        hn = each(lambda j, h: hh[j, h] * lax.rsqrt(jnp.mean(hh[j, h] * hh[j, h], axis=-1, keepdims=True) + EPS)
                  * gn_ref[:, h * DV:(h + 1) * DV])
        for j, h in prs:
            o = o_ref[ss[j], :, h * DV:(h + 1) * DV]
            ym_ref[ss[j], :, h * DV:(h + 1) * DV] = jax.nn.sigmoid(o) * hn[j, h]
            c_out[ss[j], h] = w_int[j, h] * c0[j, h] + (p[j, h] * k_col[j, h]) * v[j, h]
            n_out[ss[j], h:h + 1, :] = w_int[j, h] * n0[j, h] + p[j, h] * k[j, h]
        for j in range(SAMPLE_GROUP):
            m_new_row = jnp.zeros((1, LANES), F32)
            for h in range(HEADS):
                m_new_row = jnp.where(lane_row == h, m_t[j, h], m_new_row)
            m_out[ss[j]] = m_new_row
        return carry

    lax.fori_loop(0, sb // SAMPLE_GROUP, per_group, 0)


def _mixer_sample(z, zs3, zg3, st, c_s, n_s, m3, w_conv, g_conv, b_gates, g_norm, row0, n_dec, sb):
    cb, qw, vw = CONV_DIM, HEADS * DK, HEADS * DV
    r0 = row0 // sb
    row = lambda i: r0 + i
    return pl.pallas_call(
        _mixer_sample_kernel,
        grid=(n_dec // sb,),
        in_specs=[
            pl.BlockSpec((sb, cb), lambda i: (row(i), COL_B // cb)),
            pl.BlockSpec((sb, cb), lambda i: (row(i), COL_C // cb)),
            pl.BlockSpec((sb, cb), lambda i: (row(i), COL_H // cb)),
            pl.BlockSpec((sb, 1, qw), lambda i: (i, 0, COL_Q // qw)),
            pl.BlockSpec((sb, 1, qw), lambda i: (i, 0, COL_K // qw)),
            pl.BlockSpec((sb, 1, vw), lambda i: (i, 0, COL_V // vw)),
            pl.BlockSpec((sb, 1, vw), lambda i: (i, 0, COL_O // vw)),
            pl.BlockSpec((sb, 1, LANES), lambda i: (i, 0, 0)),
            pl.BlockSpec((sb, 2 * cb), lambda i: (i, 0)),
            pl.BlockSpec((sb, HEADS, DK, DV), lambda i: (i, 0, 0, 0)),
            pl.BlockSpec((sb, HEADS, DK), lambda i: (i, 0, 0)),
            pl.BlockSpec((sb, 1, LANES), lambda i: (i, 0, 0)),
            pl.BlockSpec((8, cb), lambda i: (0, 0)),
            pl.BlockSpec((1, cb), lambda i: (0, 0)),
            pl.BlockSpec((1, LANES), lambda i: (0, 0)),
            pl.BlockSpec((1, vw), lambda i: (0, 0)),
        ],
        out_specs=[
            pl.BlockSpec((sb, cb), lambda i: (i, 0)),
            pl.BlockSpec((sb, 1, vw), lambda i: (i, 0, 0)),
            pl.BlockSpec((sb, 2 * cb), lambda i: (i, 0)),
            pl.BlockSpec((sb, HEADS, DK, DV), lambda i: (i, 0, 0, 0)),
            pl.BlockSpec((sb, HEADS, DK), lambda i: (i, 0, 0)),
            pl.BlockSpec((sb, 1, LANES), lambda i: (i, 0, 0)),
        ],
        out_shape=[
            jax.ShapeDtypeStruct((n_dec, cb), BF16),
            jax.ShapeDtypeStruct((n_dec, 1, vw), F32),
            jax.ShapeDtypeStruct((n_dec, 2 * cb), F32),
            jax.ShapeDtypeStruct((n_dec, HEADS, DK, DV), F32),
            jax.ShapeDtypeStruct((n_dec, HEADS, DK), F32),
            jax.ShapeDtypeStruct((n_dec, 1, LANES), F32),
        ],
        compiler_params=_cparams(("arbitrary",)),
        name="mixer_sample",
    )(z, z, z, zs3, zs3, zs3, zs3, zg3, st, c_s, n_s, m3, w_conv, g_conv, b_gates, g_norm)


def _outproj_router_kernel(yc_ref, ym_ref, x_ref, wo1_ref, wo2_ref, g_ref, wr2_ref, br_ref,
                           cin_ref, h1_ref, xn_ref, e_ref, gate_ref, rank_ref, cnt_ref, carry, earlier):
    tm = x_ref.shape[0]

    @pl.when(pl.program_id(0) == 0)
    def _():
        carry[...] = cin_ref[...]
        r_i = lax.broadcasted_iota(jnp.int32, (tm, tm), 0)
        c_i = lax.broadcasted_iota(jnp.int32, (tm, tm), 1)
        earlier[...] = (c_i < r_i).astype(BF16)

    mix = (jnp.dot(yc_ref[...], wo1_ref[...], preferred_element_type=F32)
           + jnp.dot(ym_ref[...], wo2_ref[...], preferred_element_type=F32))
    h1 = x_ref[...] + mix
    h1_ref[...] = h1
    xn = _rms(h1, g_ref[...])
    xn_ref[...] = xn

    xh = xn.astype(BF16)
    xl = (xn - xh.astype(F32)).astype(BF16)
    rh = jnp.dot(xh, wr2_ref[...], preferred_element_type=F32)
    rl = jnp.dot(xl, wr2_ref[...], preferred_element_type=F32)
    logits = (rh[:, :LANES] + rh[:, LANES:]) + (rl[:, :LANES] + rl[:, LANES:]) + br_ref[...]
    lane = lax.broadcasted_iota(jnp.int32, (tm, LANES), 1)
    lane_f = lane.astype(F32)
    work = jnp.where(lane < N_EXPERTS, logits, NEG_BIG)

    tops, idxs = [], []
    chosen = jnp.zeros((tm, LANES), F32)
    for _ in range(TOP_K):
        mx = jnp.max(work, axis=1, keepdims=True)
        idx = jnp.min(jnp.where(work == mx, lane_f, float(LANES)), axis=1, keepdims=True)
        sel = lane_f == idx
        tops.append(mx)
        idxs.append(idx)
        chosen = jnp.where(sel, 1.0, chosen)
        work = jnp.where(sel, NEG_BIG, work)

    exps = [jnp.exp(t - tops[0]) for t in tops]
    denom = exps[0] + exps[1] + exps[2] + exps[3]

    before = jnp.dot(earlier[...], chosen.astype(BF16), preferred_element_type=F32) + carry[...]

    e_out = jnp.zeros((tm, LANES), F32)
    g_out = jnp.zeros((tm, LANES), F32)
    r_out = jnp.zeros((tm, LANES), F32)
    for kk in range(TOP_K):
        rank = jnp.sum(jnp.where(lane_f == idxs[kk], before, 0.0), axis=1, keepdims=True)
        e_out = jnp.where(lane == kk, idxs[kk], e_out)
        g_out = jnp.where(lane == kk, exps[kk] / denom, g_out)
        r_out = jnp.where(lane == kk, rank, r_out)
    gate_ref[...] = g_out
    for c in range(tm // LANES):
        rows = slice(c * LANES, (c + 1) * LANES)
        e_ref[:, rows] = e_out[rows].T[0:8].astype(jnp.int32)
        rank_ref[:, rows] = r_out[rows].T[0:8].astype(jnp.int32)

    new_carry = carry[...] + jnp.sum(chosen, axis=0, keepdims=True)
    carry[...] = new_carry
    cnt_ref[...] = new_carry


def _outproj_router(y_conv, y_mlstm, x, wo1, wo2, g_ffn, wr2, b_router, cnt_in, tm):
    n_tok = x.shape[0]
    full = lambda shape: pl.BlockSpec(shape, lambda i: (0,) * len(shape))
    tile = lambda w: pl.BlockSpec((tm, w), lambda i: (i, 0))
    choice_major = pl.BlockSpec((8, tm), lambda i: (0, i))
    return pl.pallas_call(
        _outproj_router_kernel,
        grid=(n_tok // tm,),
        in_specs=[
            tile(CONV_DIM), tile(HEADS * DV), tile(D_MODEL),
            full((CONV_DIM, D_MODEL)), full((HEADS * DV, D_MODEL)), full((1, D_MODEL)),
            full((D_MODEL, 2 * LANES)), full((1, LANES)), full((1, LANES)),
        ],
        out_specs=[tile(D_MODEL), tile(D_MODEL), choice_major, tile(LANES), choice_major, full((1, LANES))],
        out_shape=[
            jax.ShapeDtypeStruct((n_tok, D_MODEL), F32),
            jax.ShapeDtypeStruct((n_tok, D_MODEL), F32),
            jax.ShapeDtypeStruct((8, n_tok), jnp.int32),
            jax.ShapeDtypeStruct((n_tok, LANES), F32),
            jax.ShapeDtypeStruct((8, n_tok), jnp.int32),
            jax.ShapeDtypeStruct((1, LANES), F32),
        ],
        scratch_shapes=[pltpu.VMEM((1, LANES), F32), pltpu.VMEM((tm, tm), BF16)],
        compiler_params=_cparams(("arbitrary",)),
        name="outproj_router",
    )(y_conv, y_mlstm, x, wo1, wo2, g_ffn, wr2, b_router, cnt_in)


def _zero_fill_rows(dst, zero_scr, sem, start, count, act):
    low = count & 7
    for t in range(7):
        @pl.when(t < low)
        def _(t=t):
            act(pltpu.make_async_copy(zero_scr.at[pl.ds(0, 1)], dst.at[pl.ds(start + t, 1)], sem))

    off = start + low
    bit = 8
    while bit < SUB:
        take = count & bit

        @pl.when(take != 0)
        def _(off=off, bit=bit):
            row = pl.multiple_of(off, 8)
            act(pltpu.make_async_copy(zero_scr.at[pl.ds(0, bit)], dst.at[pl.ds(row, bit)], sem))

        off = off + take
        bit *= 2


def _zero_fill_chunks(dst, zero_scr, sem, start, act):
    def body(j, c):
        row = pl.multiple_of(start + j * SUB, SUB)
        act(pltpu.make_async_copy(zero_scr, dst.at[pl.ds(row, SUB)], sem))
        return c

    lax.fori_loop(0, (dst.shape[0] - start) // SUB, body, 0)


def _dispatch_kernel(d0_ref, d1_ref, d2_ref, d3_ref, cnt_ref, pstart_ref, xn_ref, xn_s_ref, xs_out,
                     zero_scr, sem, zsem):
    dest_refs = (d0_ref, d1_ref, d2_ref, d3_ref)
    tm = xn_ref.shape[0]
    step = pl.program_id(0)
    last = pl.num_programs(0) - 1

    def fill(act):
        for e in range(N_EXPERTS):
            c = cnt_ref[e]
            _zero_fill_rows(xs_out, zero_scr, zsem, pstart_ref[e] + c, (SUB - c % SUB) % SUB, act)
        _zero_fill_chunks(xs_out, zero_scr, zsem, pstart_ref[N_EXPERTS], act)

    @pl.when(step == 0)
    def _():
        zero_scr[...] = jnp.zeros_like(zero_scr)
        fill(lambda cp: cp.start())

    def scatter_rows(src, base):
        rows = src.shape[0]

        def issue(r, c):
            for kk in range(TOP_K):
                d = dest_refs[kk][base + r]
                pltpu.make_async_copy(src.at[pl.ds(r, 1)], xs_out.at[pl.ds(d, 1)], sem).start(priority=kk % 2)
            return c

        lax.fori_loop(0, rows, issue, 0, unroll=ISSUE_UNROLL)
        for _ in range(TOP_K):
            pltpu.make_async_copy(src, xs_out.at[pl.ds(0, rows)], sem).wait()

    scatter_rows(xn_ref, step * tm)

    @pl.when(step == last)
    def _():
        scatter_rows(xn_s_ref, pl.num_programs(0) * tm)

    @pl.when(step == 0)
    def _():
        fill(lambda cp: cp.wait())


def _dispatch(dest, counts, pstart, xn_p, xn_s, n_slots, tm):
    n_prompt = xn_p.shape[0]
    return pl.pallas_call(
        _dispatch_kernel,
        grid_spec=pltpu.PrefetchScalarGridSpec(
            num_scalar_prefetch=TOP_K + 2,
            grid=(n_prompt // tm,),
            in_specs=[pl.BlockSpec((tm, D_MODEL), lambda i, *_: (i, 0)),
                      pl.BlockSpec(xn_s.shape, lambda i, *_: (0, 0))],
            out_specs=pl.BlockSpec(memory_space=pl.ANY),
            scratch_shapes=[pltpu.VMEM((SUB, D_MODEL), F32), pltpu.SemaphoreType.DMA(()),
                            pltpu.SemaphoreType.DMA(())],
        ),
        out_shape=jax.ShapeDtypeStruct((n_slots, D_MODEL), F32),
        compiler_params=_cparams(("arbitrary",)),
        name="dispatch",
    )(*[dest[k] for k in range(TOP_K)], counts, pstart, xn_p, xn_s)


def _expert_kernel(se_ref, st_ref, sn_ref, end_ref, xs_ref, wg_ref, wu_ref, bg_ref, bu_ref, wd_ref, bd_ref,
                   y_ref, x_scr, xb_scr, acc, zero_scr, sem, zsem, xsem):
    i, f = pl.program_id(0), pl.program_id(1)
    nf = pl.num_programs(1)
    tmb = acc.shape[0]
    n = sn_ref[i]
    start = st_ref[i]

    def x_copy(j):
        row = pl.multiple_of(st_ref[j], SUB)
        return pltpu.make_async_copy(xs_ref.at[pl.ds(row, tmb)], x_scr, xsem)

    @pl.when((i == 0) & (f == 0))
    def _():
        zero_scr[...] = jnp.zeros_like(zero_scr)
        _zero_fill_chunks(y_ref, zero_scr, zsem, end_ref[0], lambda cp: cp.start())

    @pl.when((i == pl.num_programs(0) - 1) & (f == nf - 1))
    def _():
        _zero_fill_chunks(y_ref, zero_scr, zsem, end_ref[0], lambda cp: cp.wait())

    kc = D_MODEL // MOE_SPLIT

    def y_copy(o, mc, c):
        row = pl.multiple_of(start + o, SUB)
        cols = pl.ds(c * kc, kc)
        return pltpu.make_async_copy(acc.at[pl.ds(o, mc), cols], y_ref.at[pl.ds(row, mc), cols], sem)

    def for_each_sub_block(fn):
        kmax = tmb // SUB
        quarter = SUB // 4
        nq = (n + quarter - 1) // quarter
        for kq in range(4 * kmax, 4 * (kmax - 1), -1):
            pl.when(nq == kq)(functools.partial(fn, 0, kq * quarter, kmax * SUB))
        pl.when((nq <= 4 * (kmax - 1)) & (nq > 4 * (kmax - 2)))(
            functools.partial(fn, 0, (kmax - 1) * SUB, (kmax - 1) * SUB))

        @pl.when(nq <= 4 * (kmax - 2))
        def _():
            def piece(j, c):
                fn(pl.multiple_of(j * SUB, SUB), SUB, SUB)
                return c

            lax.fori_loop(0, (n + SUB - 1) // SUB, piece, 0)

    @pl.when(n > 0)
    def _():
        @pl.when(f == 0)
        def _():
            @pl.when(i == 0)
            def _():
                x_copy(0).start()

            x_copy(i).wait()
            xb_scr[...] = x_scr[...].astype(BF16)
            nxt = jnp.minimum(i + 1, pl.num_programs(0) - 1)

            @pl.when((i + 1 < pl.num_programs(0)) & (sn_ref[nxt] > 0))
            def _():
                x_copy(nxt).start()

            acc[...] = jnp.broadcast_to(bd_ref[0], acc.shape)

        def sub_block(o, m, mc):
            g = bg_ref[0]
            u = bu_ref[0]
            for c in range(MOE_SPLIT):
                xb = xb_scr[pl.ds(o, m), c * kc:(c + 1) * kc]
                g = g + jnp.dot(xb, wg_ref[0, c * kc:(c + 1) * kc, :].astype(BF16), preferred_element_type=F32)
                u = u + jnp.dot(xb, wu_ref[0, c * kc:(c + 1) * kc, :].astype(BF16), preferred_element_type=F32)
            g = jnp.minimum(g, SWIGLU_LIMIT)
            u = jnp.clip(u, -SWIGLU_LIMIT, SWIGLU_LIMIT)
            act = (g * jax.nn.sigmoid(SWIGLU_ALPHA * g) * (u + 1.0)).astype(BF16)
            for c in range(MOE_SPLIT):
                cols = slice(c * kc, (c + 1) * kc)
                acc[pl.ds(o, m), cols] += jnp.dot(act, wd_ref[0, :, cols].astype(BF16),
                                                  preferred_element_type=F32)

                @pl.when(f == nf - 1)
                def _(c=c):
                    y_copy(o, mc, c).start()

        for_each_sub_block(sub_block)

        @pl.when(f == nf - 1)
        def _():
            def drain(o, m, mc):
                for c in range(MOE_SPLIT):
                    y_copy(o, mc, c).wait()

            for_each_sub_block(drain)


def _experts(sb_e, sb_start, sb_n, end, xs, w_gate_up, b_gate_up, w_down, b_down, nsb, tmb, tf):
    n_slots = xs.shape[0]
    nf = D_FF // tf

    def fidx(i, f, sn):
        return jnp.where(sn[i] > 0, f, nf - 1)

    return pl.pallas_call(
        _expert_kernel,
        grid_spec=pltpu.PrefetchScalarGridSpec(
            num_scalar_prefetch=4,
            grid=(nsb, nf),
            in_specs=[
                pl.BlockSpec(memory_space=pl.ANY),
                pl.BlockSpec((1, D_MODEL, tf), lambda i, f, se, st, sn, en: (se[i], 0, fidx(i, f, sn))),
                pl.BlockSpec((1, D_MODEL, tf), lambda i, f, se, st, sn, en: (se[i], 0, nf + fidx(i, f, sn))),
                pl.BlockSpec((1, 1, tf), lambda i, f, se, st, sn, en: (se[i], 0, fidx(i, f, sn))),
                pl.BlockSpec((1, 1, tf), lambda i, f, se, st, sn, en: (se[i], 0, nf + fidx(i, f, sn))),
                pl.BlockSpec((1, tf, D_MODEL), lambda i, f, se, st, sn, en: (se[i], fidx(i, f, sn), 0)),
                pl.BlockSpec((1, 1, D_MODEL), lambda i, f, se, st, sn, en: (se[i], 0, 0)),
            ],
            out_specs=pl.BlockSpec(memory_space=pl.ANY),
            scratch_shapes=[pltpu.VMEM((tmb, D_MODEL), F32), pltpu.VMEM((tmb, D_MODEL), BF16),
                            pltpu.VMEM((tmb, D_MODEL), F32), pltpu.VMEM((SUB, D_MODEL), F32),
                            pltpu.SemaphoreType.DMA(()), pltpu.SemaphoreType.DMA(()),
                            pltpu.SemaphoreType.DMA(())],
        ),
        out_shape=jax.ShapeDtypeStruct((n_slots, D_MODEL), F32),
        compiler_params=_cparams(("arbitrary", "arbitrary")),
        name="experts",
    )(sb_e, sb_start, sb_n, end, xs, w_gate_up, w_gate_up, b_gate_up, b_gate_up, w_down, b_down)


def _combine_kernel(d0_ref, d1_ref, d2_ref, d3_ref, h1_ref, gate_ref, p_ref, yb_ref, wpg_ref, wpp_ref,
                    gp_ref, gf_ref, out_ref, gbuf_a, gbuf_b, sem):
    dest_refs = (d0_ref, d1_ref, d2_ref, d3_ref)
    tm = h1_ref.shape[0]
    step = pl.program_id(0)
    last = pl.num_programs(0) - 1
    bufs = (gbuf_a, gbuf_b)

    def row_gather(t, r, kk, s):
        d = dest_refs[kk][t * tm + r]
        return pltpu.make_async_copy(yb_ref.at[pl.ds(d, 1)], bufs[s].at[kk, pl.ds(r, 1)], sem.at[s])

    def wait_tile(s):
        for kk in range(TOP_K):
            pltpu.make_async_copy(yb_ref.at[pl.ds(0, tm)], bufs[s].at[kk], sem.at[s]).wait()

    @pl.when(step == 0)
    def _():
        def issue(r, c):
            for kk in range(TOP_K):
                row_gather(0, r, kk, 0).start(priority=kk % 2)
            return c

        lax.fori_loop(0, tm, issue, 0, unroll=ISSUE_UNROLL)

    def tile_step(cur):
        nxt = 1 - cur
        wait_tile(cur)
        t_next = jnp.minimum(step + 1, last)
        for r in range(tm):
            for kk in range(TOP_K):
                row_gather(t_next, r, kk, nxt).start(priority=kk % 2)

        gate = gate_ref[...]
        moe = gate[:, 0:1] * bufs[cur][0]
        for kk in range(1, TOP_K):
            moe = moe + gate[:, kk:kk + 1] * bufs[cur][kk]
        h2 = h1_ref[...] + moe
        a = _rms(h2, gp_ref[...]).astype(BF16)
        pg = jax.nn.sigmoid(jnp.dot(a, wpg_ref[...], preferred_element_type=F32))
        pp = jnp.dot(p_ref[...].astype(BF16), wpp_ref[...], preferred_element_type=F32)
        h3 = h2 + pg * pp
        out_ref[...] = _rms(h3, gf_ref[...])

        @pl.when(step == last)
        def _():
            wait_tile(nxt)

    pl.when(step % 2 == 0)(functools.partial(tile_step, 0))
    pl.when(step % 2 == 1)(functools.partial(tile_step, 1))


def _combine(dest, h1, gate, p, yb, wpg, wpp, g_ple, g_final, tm):
    n_tok = h1.shape[0]
    full = lambda shape: pl.BlockSpec(shape, lambda i, *_: (0,) * len(shape))
    tile = lambda w: pl.BlockSpec((tm, w), lambda i, *_: (i, 0))
    return pl.pallas_call(
        _combine_kernel,
        grid_spec=pltpu.PrefetchScalarGridSpec(
            num_scalar_prefetch=TOP_K,
            grid=(n_tok // tm,),
            in_specs=[
                tile(D_MODEL), tile(LANES), tile(PLE_DIM),
                pl.BlockSpec(memory_space=pl.ANY),
                full((D_MODEL, D_MODEL)), full((PLE_DIM, D_MODEL)), full((1, D_MODEL)), full((1, D_MODEL)),
            ],
            out_specs=tile(D_MODEL),
            scratch_shapes=[pltpu.VMEM((TOP_K, tm, D_MODEL), F32), pltpu.VMEM((TOP_K, tm, D_MODEL), F32),
                            pltpu.SemaphoreType.DMA((2,))],
        ),
        out_shape=jax.ShapeDtypeStruct((n_tok, D_MODEL), F32),
        compiler_params=_cparams(("arbitrary",)),
        name="combine",
    )(*[dest[k] for k in range(TOP_K)], h1, gate, p, yb, wpg, wpp, g_ple, g_final)


def kernel(x_prompt, x_sample, state_conv, state_mlstm_C, state_mlstm_n, state_mlstm_m, p_prompt, p_sample,
           g_mix, w_in, b_gates, w_conv, g_conv_out, g_mlstm_out, w_out, g_ffn, w_router, b_router,
           w_gate_up, b_gate_up, w_down, b_down, g_ple, w_ple_gate, w_ple_proj, g_final):
    batch, seq, _ = x_prompt.shape
    n_dec = x_sample.shape[0]
    assert w_in.shape[0] == 1 and x_sample.shape[1] == 1
    n_prompt = batch * seq
    n_tok = n_prompt + n_dec

    x_p = x_prompt.reshape(n_prompt, D_MODEL)
    x_s = x_sample.reshape(n_dec, D_MODEL)

    row = lambda a: a.reshape(1, -1)
    pad_lanes = lambda a: jnp.pad(a, ((0, 0), (0, LANES - a.shape[1])))
    w_gates = pad_lanes(w_in[0, :, MAIN_COLS:]).astype(BF16)
    bg = pad_lanes(row(b_gates[0]))
    wc = jnp.pad(w_conv[0], ((0, 8 - w_conv.shape[1]), (0, 0)))
    g_norm = row(g_mlstm_out[0])

    a_p, zg_p = _norm_gates(x_p, row(g_mix[0]), w_gates, tm=1024)
    a_s, zg_s = _norm_gates(x_s, row(g_mix[0]), w_gates, tm=n_dec)
    z_p = _inproj(a_p, w_in, tm=1024, tn=1024)
    z_s = _inproj(a_s, w_in, tm=n_dec, tn=1024)

    yc_p, ym_p, conv_p, c_p, n_p, m_p = _mixer_prompt(
        z_p.reshape(batch, seq, MAIN_COLS), zg_p.reshape(batch, seq, LANES), wc, row(g_conv_out[0]), bg,
        g_norm, nbt=4)
    yc_p = yc_p.reshape(n_prompt, CONV_DIM)
    ym_p = ym_p.reshape(n_prompt, HEADS * DV)
    yc_s, ym_s, conv_s, c_s, n_s, m_s = _mixer_sample(
        z_s, z_s.reshape(n_dec, 1, MAIN_COLS), zg_s.reshape(n_dec, 1, LANES),
        state_conv[0].reshape(n_dec, 2 * CONV_DIM), state_mlstm_C[0], state_mlstm_n[0],
        pad_lanes(state_mlstm_m[0]).reshape(n_dec, 1, LANES), wc, row(g_conv_out[0]), bg, g_norm,
        row0=0, n_dec=n_dec, sb=16)
    m_s = m_s[:, 0, :HEADS]
    ym_s = ym_s.reshape(n_dec, HEADS * DV).astype(BF16)

    wo = w_out[0].astype(BF16)
    wr = pad_lanes(w_router[0])
    wr_hi = wr.astype(BF16)
    wr_lo = (wr - wr_hi.astype(F32)).astype(BF16)
    wr2 = jnp.concatenate([wr_hi, wr_lo], axis=1)
    router_args = (wo[:CONV_DIM], wo[CONV_DIM:], row(g_ffn[0]), wr2, pad_lanes(row(b_router[0])))
    h1_p, xn_p, e_p, gate_p, rank_p, cnt_p = _outproj_router(
        yc_p, ym_p, x_p, *router_args, jnp.zeros((1, LANES), F32), tm=512)
    h1_s, xn_s, e_s, gate_s, rank_s, cnt = _outproj_router(
        yc_s, ym_s, x_s, *router_args, cnt_p, tm=n_dec)

    n_assign = n_tok * TOP_K
    tmb = MOE_TMB
    nsb = n_assign // tmb + N_EXPERTS
    n_slots = -(-(n_assign + N_EXPERTS * (SUB - 1)) // SUB) * SUB + tmb
    counts = cnt[0, :N_EXPERTS].astype(jnp.int32)
    padded = (counts + SUB - 1) // SUB * SUB
    pend = jnp.cumsum(padded)
    pstart = jnp.concatenate([jnp.zeros((1,), jnp.int32), pend]).astype(jnp.int32)
    sb_per_e = (counts + tmb - 1) // tmb
    sb_cum = jnp.cumsum(sb_per_e)
    n_used = sb_cum[-1]
    sb = jnp.minimum(jnp.arange(nsb, dtype=jnp.int32), n_used - 1)
    sb_e = jnp.minimum(jnp.sum((sb_cum[None, :] <= sb[:, None]).astype(jnp.int32), axis=1), N_EXPERTS - 1)
    sb_j = sb - (sb_cum - sb_per_e)[sb_e]
    sb_start = (pstart[sb_e] + sb_j * tmb).astype(jnp.int32)
    sb_n = jnp.where(jnp.arange(nsb) < n_used, jnp.clip(counts[sb_e] - sb_j * tmb, 0, tmb), 0).astype(jnp.int32)
    expert_ids = jnp.arange(N_EXPERTS, dtype=jnp.int32)[:, None, None]

    def slot_of(e, r):
        first = jnp.sum(jnp.where(e[None, :TOP_K] == expert_ids, pstart[:N_EXPERTS, None, None], 0), axis=0)
        return (first + r[:TOP_K]).astype(jnp.int32)
    dest_p = slot_of(e_p, rank_p)
    dest_s = slot_of(e_s, rank_s)

    xs = _dispatch(jnp.concatenate([dest_p, dest_s], axis=1), counts, pstart, xn_p, xn_s, n_slots, tm=512)
    yb = _experts(sb_e.astype(jnp.int32), sb_start, sb_n, pstart[N_EXPERTS:], xs, w_gate_up[0],
                  b_gate_up[0].reshape(N_EXPERTS, 1, 2 * D_FF), w_down[0],
                  b_down[0].reshape(N_EXPERTS, 1, D_MODEL), nsb, tmb, MOE_TF)

    ple_args = (w_ple_gate[0].astype(BF16), w_ple_proj[0].astype(BF16), row(g_ple[0]), row(g_final))
    out_p = _combine(dest_p, h1_p, gate_p, p_prompt[0].reshape(n_prompt, PLE_DIM), yb, *ple_args, tm=256)
    out_s = _combine(dest_s, h1_s, gate_s, p_sample[0].reshape(n_dec, PLE_DIM), yb, *ple_args, tm=n_dec)

    y_prompt = out_p.reshape(batch, seq, D_MODEL)
    y_sample = out_s.reshape(n_dec, 1, D_MODEL)
    return (y_prompt, y_sample,
            conv_p[None], c_p[None], n_p[None], m_p[None, :, :, 0],
            conv_s.reshape(1, n_dec, 2, CONV_DIM), c_s[None], n_s[None], m_s[None])
```
